```python
import math
import jax, jax.numpy as jnp
from jax import lax
import numpy as np

D_MODEL = 1024
BATCH = 8
SEQ = 4096
DEPTH = 2

FOX_WIDTH = D_MODEL // 2
FOX_HEADS = 8
FOX_HEAD_DIM = FOX_WIDTH // FOX_HEADS
S5_WIDTH = D_MODEL - FOX_WIDTH
S5_GROUP = 16
S5_GROUPS = S5_WIDTH // S5_GROUP
S5_STATE = 64
Q_BLOCK = 128
EVEN_IN = 3 * FOX_WIDTH + FOX_HEADS + S5_WIDTH
CONV_WIDTH = D_MODEL // 2
CONV_K = 3
RET_WIDTH = D_MODEL - CONV_WIDTH
RET_HEADS = 4
RET_HEAD_DIM = RET_WIDTH // RET_HEADS
RET_CHUNK = 128
ODD_IN = 3 * CONV_WIDTH + 4 * RET_WIDTH
D_FF = 4 * D_MODEL
EPS = 1e-6
N_EVEN = (DEPTH + 1) // 2
N_ODD = DEPTH // 2

kernel_name = "hybrid_fox_s5_shortconv_retention"

F32 = jnp.float32


def rms_norm(x, g):
    xf = x.astype(F32)
    y = xf * lax.rsqrt(jnp.mean(xf * xf, axis=-1, keepdims=True) + EPS)
    return (y * g.astype(F32)).astype(x.dtype)


def forgetting_attention(q, k, v, log_f):
    b, l, h, dh = q.shape
    nblk = l // Q_BLOCK
    c = jnp.cumsum(log_f.astype(F32), axis=1).transpose(0, 2, 1)
    scale = dh ** -0.5
    q_blocks = q.reshape(b, nblk, Q_BLOCK, h, dh).transpose(1, 0, 3, 2, 4)
    c_blocks = c.reshape(b, h, nblk, Q_BLOCK).transpose(2, 0, 1, 3)
    k_pos = jnp.arange(l)

    def block(args):
        i, q_i, cq_i = args
        s = jnp.einsum('bhqd,bkhd->bhqk', q_i, k).astype(F32) * scale
        s = s + cq_i[..., None] - c[:, :, None, :]
        q_pos = i * Q_BLOCK + jnp.arange(Q_BLOCK)
        causal = k_pos[None, :] <= q_pos[:, None]
        s = jnp.where(causal, s, -jnp.inf)
        p = jax.nn.softmax(s, axis=-1).astype(v.dtype)
        return jnp.einsum('bhqk,bkhd->bqhd', p, v)

    out = lax.map(block, (jnp.arange(nblk), q_blocks, c_blocks))
    return out.transpose(1, 0, 2, 3, 4).reshape(b, l, h * dh)


def s5_layer(u, log_dt, lam_re, lam_im, b_re, b_im, c_re, c_im, d_skip, w_glu, b_glu):
    b, l, _ = u.shape
    uf = u.astype(F32).reshape(b, l, S5_GROUPS, S5_GROUP)
    lam = lax.complex(lam_re.astype(F32), lam_im.astype(F32))
    dt = jnp.exp(log_dt.astype(F32))[:, None]
    lam_bar = jnp.exp(lam * dt)
    b_mat = lax.complex(b_re.astype(F32), b_im.astype(F32))
    b_bar = ((lam_bar - 1.0) / lam)[..., None] * b_mat
    c_mat = lax.complex(c_re.astype(F32), c_im.astype(F32))
    bu = jnp.einsum('blgh,gph->blgp', uf, b_bar)
    a = jnp.broadcast_to(lam_bar, (1, l, S5_GROUPS, S5_STATE))

    def combine(e1, e2):
        a1, b1 = e1
        a2, b2 = e2
        return a1 * a2, a2 * b1 + b2

    _, states = lax.associative_scan(combine, (a, bu), axis=1)
    y = jnp.real(jnp.einsum('blgp,ghp->blgh', states, c_mat))
    y = y + d_skip.astype(F32).reshape(S5_GROUPS, S5_GROUP) * uf
    y = jax.nn.gelu(y.reshape(b, l, S5_WIDTH))
    y = y * jax.nn.sigmoid(y @ w_glu.astype(F32) + b_glu.astype(F32))
    return y.astype(u.dtype)


def causal_depthwise_conv(x, w):
    ch = x.shape[-1]
    return lax.conv_general_dilated(
        x, w[:, None, :].astype(x.dtype), window_strides=(1,),
        padding=[(CONV_K - 1, 0)], dimension_numbers=('NWC', 'WIO', 'NWC'),
        feature_group_count=ch)


def rotate_every_two(x, pos):
    d = x.shape[-1]
    inv = 1.0 / (10000.0 ** jnp.linspace(0.0, 1.0, d // 2, dtype=F32))
    ang = pos.astype(F32)[:, None] * inv[None, :]
    cos = jnp.cos(ang)[None, :, None, :]
    sin = jnp.sin(ang)[None, :, None, :]
    x1, x2 = x[..., 0::2], x[..., 1::2]
    return jnp.stack([x1 * cos - x2 * sin, x1 * sin + x2 * cos], axis=-1).reshape(x.shape)


def retention_chunkwise(q, k, v):
    b, l, h, d = q.shape
    n = l // RET_CHUNK
    log_gamma = jnp.log(1.0 - 2.0 ** (-5.0 - jnp.arange(h, dtype=F32)))
    idx = jnp.arange(RET_CHUNK, dtype=F32)
    rel = idx[:, None] - idx[None, :]
    inner_decay = jnp.where(rel >= 0,
                            jnp.exp(log_gamma[:, None, None] * jnp.maximum(rel, 0.0)), 0.0)
    query_decay = jnp.exp(log_gamma[:, None] * (idx + 1.0))[None, :, :, None]
    key_decay = jnp.exp(log_gamma[:, None] * (RET_CHUNK - 1.0 - idx))[None, :, :, None]
    chunk_decay = jnp.exp(log_gamma * RET_CHUNK)[None, :, None, None]

    def to_chunks(t):
        return t.reshape(b, n, RET_CHUNK, h, t.shape[-1]).transpose(1, 0, 3, 2, 4)

    qc, kc, vc = to_chunks(q), to_chunks(k * d ** -0.5), to_chunks(v)

    def step(state, inp):
        q_i, k_i, v_i = inp
        s = jnp.einsum('bhqd,bhkd->bhqk', q_i, k_i) * inner_decay
        inner = jnp.einsum('bhqk,bhkv->bhqv', s, v_i)
        cross = jnp.einsum('bhqd,bhdv->bhqv', q_i, state) * query_decay
        new_state = state * chunk_decay + jnp.einsum('bhkd,bhkv->bhdv', k_i * key_decay, v_i)
        return new_state, inner + cross

    state0 = jnp.zeros((b, h, d, v.shape[-1]), F32)
    _, out = lax.scan(step, state0, (qc, kc, vc))
    return out.transpose(1, 0, 3, 2, 4).reshape(b, l, h, v.shape[-1])


def even_mixer(h, w_in, b_forget, log_dt, lam_re, lam_im, b_re, b_im, c_re, c_im,
               d_skip, w_glu, b_glu, w_out):
    b, l, _ = h.shape
    proj = h @ w_in
    q, k, v, f_logit, u = jnp.split(
        proj, [FOX_WIDTH, 2 * FOX_WIDTH, 3 * FOX_WIDTH, 3 * FOX_WIDTH + FOX_HEADS], axis=-1)
    heads = lambda t: t.reshape(b, l, FOX_HEADS, FOX_HEAD_DIM)
    log_f = jax.nn.log_sigmoid(f_logit.astype(F32) + b_forget.astype(F32))
    fox = forgetting_attention(heads(q), heads(k), heads(v), log_f)
    s5 = s5_layer(u, log_dt, lam_re, lam_im, b_re, b_im, c_re, c_im, d_skip, w_glu, b_glu)
    return jnp.concatenate([fox.astype(h.dtype), s5], axis=-1) @ w_out


def odd_mixer(h, w_in, conv_w, w_out):
    b, l, _ = h.shape
    proj = h @ w_in
    cw, rw = CONV_WIDTH, RET_WIDTH
    hc, gate_b, gate_c, q, k, v, g = jnp.split(
        proj, [cw, 2 * cw, 3 * cw, 3 * cw + rw, 3 * cw + 2 * rw, 3 * cw + 3 * rw], axis=-1)
    conv_out = gate_b * causal_depthwise_conv(gate_c * hc, conv_w)
    pos = jnp.arange(l)
    heads = lambda t: t.astype(F32).reshape(b, l, RET_HEADS, RET_HEAD_DIM)
    ret = retention_chunkwise(rotate_every_two(heads(q), pos), rotate_every_two(heads(k), pos), heads(v))
    mu = jnp.mean(ret, axis=-1, keepdims=True)
    var = jnp.mean(jnp.square(ret - mu), axis=-1, keepdims=True)
    ret = ((ret - mu) * lax.rsqrt(var + EPS)).reshape(b, l, rw)
    ret_out = (jax.nn.silu(g.astype(F32)) * ret).astype(h.dtype)
    return jnp.concatenate([conv_out, ret_out], axis=-1) @ w_out


def squared_relu_mlp(h, w_up, w_down):
    return jnp.square(jax.nn.relu(h @ w_up)) @ w_down


def setup_inputs(seed: int = 0) -> dict:
    key = jax.random.key(seed)
    ks = jax.random.split(key, 32)
    nrm = lambda k, shape, s: jax.random.normal(k, shape, F32) * s
    gain = lambda k, shape: 1.0 + 0.02 * jax.random.normal(k, shape, F32)
    x = jax.random.normal(ks[0], (BATCH, SEQ, D_MODEL), F32)
    lam_im_base = math.pi * jnp.arange(S5_STATE, dtype=F32)
    return {
        "x": x,
        "even_norm_mix": gain(ks[1], (N_EVEN, D_MODEL)),
        "even_w_in": nrm(ks[2], (N_EVEN, D_MODEL, EVEN_IN), D_MODEL ** -0.5),
        "even_b_forget": 3.0 + nrm(ks[3], (N_EVEN, FOX_HEADS), 0.1),
        "even_s5_log_dt": jax.random.uniform(ks[4], (N_EVEN, S5_GROUPS), F32,
                                              math.log(1e-3), math.log(1e-1)),
        "even_s5_lambda_re": -0.5 * jnp.exp(nrm(ks[5], (N_EVEN, S5_GROUPS, S5_STATE), 0.02)),
        "even_s5_lambda_im": lam_im_base + nrm(ks[6], (N_EVEN, S5_GROUPS, S5_STATE), 0.01),
        "even_s5_b_re": nrm(ks[7], (N_EVEN, S5_GROUPS, S5_STATE, S5_GROUP), (2 * S5_GROUP) ** -0.5),
        "even_s5_b_im": nrm(ks[8], (N_EVEN, S5_GROUPS, S5_STATE, S5_GROUP), (2 * S5_GROUP) ** -0.5),
        "even_s5_c_re": nrm(ks[9], (N_EVEN, S5_GROUPS, S5_GROUP, S5_STATE), (2 * S5_STATE) ** -0.5 * 4.0),
        "even_s5_c_im": nrm(ks[10], (N_EVEN, S5_GROUPS, S5_GROUP, S5_STATE), (2 * S5_STATE) ** -0.5 * 4.0),
        "even_s5_d": nrm(ks[11], (N_EVEN, S5_WIDTH), 1.0),
        "even_s5_w_glu": nrm(ks[12], (N_EVEN, S5_WIDTH, S5_WIDTH), S5_WIDTH ** -0.5),
        "even_s5_b_glu": nrm(ks[13], (N_EVEN, S5_WIDTH), 0.01),
        "even_w_out": nrm(ks[14], (N_EVEN, D_MODEL, D_MODEL), D_MODEL ** -0.5),
        "odd_norm_mix": gain(ks[15], (N_ODD, D_MODEL)),
        "odd_w_in": nrm(ks[16], (N_ODD, D_MODEL, ODD_IN), D_MODEL ** -0.5),
        "odd_conv_w": nrm(ks[17], (N_ODD, CONV_K, CONV_WIDTH), CONV_K ** -0.5),
        "odd_w_out": nrm(ks[18], (N_ODD, D_MODEL, D_MODEL), D_MODEL ** -0.5),
        "mlp_norm": gain(ks[19], (DEPTH, D_MODEL)),
        "mlp_w_up": nrm(ks[20], (DEPTH, D_MODEL, D_FF), D_MODEL ** -0.5),
        "mlp_w_down": nrm(ks[21], (DEPTH, D_FF, D_MODEL), D_FF ** -0.5),
        "final_norm": gain(ks[22], (D_MODEL,)),
    }


def reference(x, even_norm_mix, even_w_in, even_b_forget, even_s5_log_dt, even_s5_lambda_re,
              even_s5_lambda_im, even_s5_b_re, even_s5_b_im, even_s5_c_re, even_s5_c_im,
              even_s5_d, even_s5_w_glu, even_s5_b_glu, even_w_out, odd_norm_mix, odd_w_in,
              odd_conv_w, odd_w_out, mlp_norm, mlp_w_up, mlp_w_down, final_norm):
    for layer in range(DEPTH):
        j = layer // 2
        if layer % 2 == 0:
            x = x + even_mixer(rms_norm(x, even_norm_mix[j]), even_w_in[j], even_b_forget[j],
                               even_s5_log_dt[j], even_s5_lambda_re[j], even_s5_lambda_im[j],
                               even_s5_b_re[j], even_s5_b_im[j], even_s5_c_re[j], even_s5_c_im[j],
                               even_s5_d[j], even_s5_w_glu[j], even_s5_b_glu[j], even_w_out[j])
        else:
            x = x + odd_mixer(rms_norm(x, odd_norm_mix[j]), odd_w_in[j], odd_conv_w[j], odd_w_out[j])
        x = x + squared_relu_mlp(rms_norm(x, mlp_norm[layer]), mlp_w_up[layer], mlp_w_down[layer])
    return rms_norm(x, final_norm)
```

```python
import functools
import math

import jax
import jax.numpy as jnp
from jax import lax
from jax.experimental import pallas as pl
from jax.experimental.pallas import tpu as pltpu

F32 = jnp.float32
BF16 = jnp.bfloat16
EPS = 1e-6

LANES = 128
SUBLANES = 8
FOX_HEAD_DIM = 64
RET_HEADS = 4
RET_CHUNK = 128
ROW_BLOCK = 512
S5_TIME_BLOCK = 64
RET_ROW_BLOCK = 512
MLP_FF_BLOCK = 1024
MIB = 1024 * 1024


def _params(semantics, vmem_mib):
    return pltpu.CompilerParams(dimension_semantics=semantics, vmem_limit_bytes=vmem_mib * MIB)


def _resident(shape):
    return pl.BlockSpec(shape, lambda *_: (0,) * len(shape), pipeline_mode=pl.Buffered(1))


def _rms_norm(x, g):
    return x * lax.rsqrt(jnp.mean(x * x, axis=-1, keepdims=True) + EPS) * g


def _sigmoid(x):
    return 1.0 / (1.0 + jnp.exp(-x))


def _dot(a, b):
    return jnp.dot(a, b, preferred_element_type=F32)


def _dot_nt(a, b):
    return lax.dot_general(a, b, (((1,), (1,)), ((), ())), preferred_element_type=F32)


def _dot_tn(a, b):
    return lax.dot_general(a, b, (((0,), (0,)), ((), ())), preferred_element_type=F32)


def _even_in_kernel(x_ref, g_ref, wqkv_ref, wu_ref, wf_ref, bf_ref, tri_ref,
                    qkv_ref, u_ref, c_ref, carry_ref, *, blocks_per_seq):
    i = pl.program_id(0)

    @pl.when(i % blocks_per_seq == 0)
    def _():
        carry_ref[...] = jnp.zeros_like(carry_ref)

    h = _rms_norm(x_ref[...], g_ref[...]).astype(BF16)
    qkv_ref[...] = _dot(h, wqkv_ref[...]).astype(BF16)
    u_ref[...] = _dot(h, wu_ref[...])
    fl = _dot(h, wf_ref[...]) + bf_ref[...]
    logf = jnp.minimum(fl, 0.0) - jnp.log(1.0 + jnp.exp(-jnp.abs(fl)))
    hi = logf.astype(BF16)
    r1 = logf - hi.astype(F32)
    mid = r1.astype(BF16)
    lo = (r1 - mid.astype(F32)).astype(BF16)
    tri = tri_ref[...]
    cs = _dot(tri, hi) + _dot(tri, mid) + _dot(tri, lo) + carry_ref[...]
    carry_ref[...] = cs[cs.shape[0] - 1:, :]
    c_ref[0, 0] = cs.T[:SUBLANES, :]


def _even_in(x2, g, wqkv, wu, wf, bf, seq):
    n, d = x2.shape
    tm = ROW_BLOCK
    nb = seq // tm
    tri = jnp.tril(jnp.ones((tm, tm), F32)).astype(BF16)
    row = lambda i: (i, 0)
    return pl.pallas_call(
        functools.partial(_even_in_kernel, blocks_per_seq=nb),
        grid=(n // tm,),
        in_specs=[
            pl.BlockSpec((tm, d), row),
            _resident((1, d)),
            _resident(wqkv.shape),
            _resident(wu.shape),
            _resident(wf.shape),
            _resident((1, LANES)),
            _resident((tm, tm)),
        ],
        out_specs=[
            pl.BlockSpec((tm, wqkv.shape[1]), row),
            pl.BlockSpec((tm, wu.shape[1]), row),
            pl.BlockSpec((1, 1, SUBLANES, tm), lambda i: (i // nb, i % nb, 0, 0)),
        ],
        out_shape=[
            jax.ShapeDtypeStruct((n, wqkv.shape[1]), BF16),
            jax.ShapeDtypeStruct((n, wu.shape[1]), F32),
            jax.ShapeDtypeStruct((n // seq, nb, SUBLANES, tm), F32),
        ],
        scratch_shapes=[pltpu.VMEM((1, LANES), F32)],
        compiler_params=_params(("arbitrary",), 40),
        name="even_in",
    )(x2, g, wqkv, wu, wf, bf, tri)


def _fox_kernel(q_ref, k_ref, v_ref, c_ref, o_ref, m_scr, l_scr, acc_scr, *, blk):
    hp = pl.program_id(1)
    qi = pl.program_id(2)
    lane = lax.broadcasted_iota(jnp.int32, (1, LANES), 1)
    scale = FOX_HEAD_DIM ** -0.5
    q = q_ref[0]
    qm = [jnp.where((lane >= FOX_HEAD_DIM * hh) & (lane < FOX_HEAD_DIM * (hh + 1)), q, 0) * scale
          for hh in range(2)]
    row = lax.broadcasted_iota(jnp.int32, (blk, blk), 0)
    col = lax.broadcasted_iota(jnp.int32, (blk, blk), 1)

    m_scr[...] = jnp.full_like(m_scr, -jnp.inf)
    l_scr[...] = jnp.zeros_like(l_scr)
    acc_scr[...] = jnp.zeros_like(acc_scr)

    def block(j, masked):
        k0 = pl.multiple_of(j * blk, blk)
        k = k_ref[0, pl.ds(k0, blk), :]
        v = v_ref[0, pl.ds(k0, blk), :]
        for hh in range(2):
            h = 2 * hp + hh
            c_q = c_ref[0, qi, pl.ds(h, 1), :][:, 0:1]
            c_k = c_ref[0, j, pl.ds(h, 1), :]
            s = _dot_nt(qm[hh], k) + (c_q - c_k)
            if masked:
                s = jnp.where(row >= col, s, -jnp.inf)
            m_prev = m_scr[hh]
            m_new = jnp.maximum(m_prev, jnp.max(s, axis=-1, keepdims=True))
            p = jnp.exp(s - m_new)
            alpha = jnp.exp(m_prev - m_new)
            l_scr[hh] = alpha * l_scr[hh] + jnp.sum(p, axis=-1, keepdims=True)
            acc_scr[hh] = alpha * acc_scr[hh] + _dot(p.astype(BF16), v)
            m_scr[hh] = m_new

    def off_diagonal(j, carry):
        block(j, False)
        return carry

    lax.fori_loop(0, qi, off_diagonal, 0)
    block(qi, True)
    out = jnp.where(lane < FOX_HEAD_DIM, acc_scr[0] / l_scr[0], acc_scr[1] / l_scr[1])
    o_ref[0] = out.astype(o_ref.dtype)


def _fox(qkv3, c4, heads):
    b, seq, _ = qkv3.shape
    blk = ROW_BLOCK
    pairs = heads * FOX_HEAD_DIM // LANES
    return pl.pallas_call(
        functools.partial(_fox_kernel, blk=blk),
        grid=(b, pairs, seq // blk),
        in_specs=[
            pl.BlockSpec((1, blk, LANES), lambda bi, hp, qi: (bi, qi, hp)),
            pl.BlockSpec((1, seq, LANES), lambda bi, hp, qi: (bi, 0, pairs + hp)),
            pl.BlockSpec((1, seq, LANES), lambda bi, hp, qi: (bi, 0, 2 * pairs + hp)),
            pl.BlockSpec((1,) + c4.shape[1:], lambda bi, hp, qi: (bi, 0, 0, 0)),
        ],
        out_specs=pl.BlockSpec((1, blk, LANES), lambda bi, hp, qi: (bi, qi, hp)),
        out_shape=jax.ShapeDtypeStruct((b, seq, pairs * LANES), BF16),
        scratch_shapes=[
            pltpu.VMEM((2, blk, 1), F32),
            pltpu.VMEM((2, blk, 1), F32),
            pltpu.VMEM((2, blk, LANES), F32),
        ],
        compiler_params=_params(("arbitrary", "arbitrary", "arbitrary"), 40),
        name="fox_attention",
    )(qkv3, qkv3, qkv3, c4)


def _s5_param_kernel(log_dt_ref, lr_ref, li_ref, br_ref, bi_ref, ar_ref, ai_ref, bbr_ref, bbi_ref):
    dt = jnp.exp(log_dt_ref[...])
    lr = lr_ref[...]
    li = li_ref[...]
    mag = jnp.exp(lr * dt)
    a_re = mag * jnp.cos(li * dt)
    a_im = mag * jnp.sin(li * dt)
    den = lr * lr + li * li
    n_re = a_re - 1.0
    coef_re = (n_re * lr + a_im * li) / den
    coef_im = (a_im * lr - n_re * li) / den
    br = br_ref[...]
    bi = bi_ref[...]
    ar_ref[...] = a_re
    ai_ref[...] = a_im
    bbr_ref[...] = coef_re * br - coef_im * bi
    bbi_ref[...] = coef_re * bi + coef_im * br


def _s5_params(log_dt, lam_re, lam_im, b_re, b_im):
    groups, state, width = b_re.shape
    rep = lambda t: jnp.repeat(t, width, axis=1)
    flat = (groups, state * width)
    out = jax.ShapeDtypeStruct(flat, F32)
    a_re, a_im, bb_re, bb_im = pl.pallas_call(
        _s5_param_kernel,
        out_shape=[out, out, out, out],
        name="s5_discretise",
    )(rep(jnp.broadcast_to(log_dt[:, None], (groups, state))), rep(lam_re), rep(lam_im),
      b_re.reshape(flat), b_im.reshape(flat))
    unrep = lambda t: t.reshape(groups, state, width)[:, :, 0].reshape(1, groups * state)
    return unrep(a_re), unrep(a_im), bb_re.reshape(b_re.shape), bb_im.reshape(b_re.shape)


def _s5_kernel(u_ref, bmat_ref, cre_ref, cim_ref, are_ref, aim_ref, d_ref, wglu_ref, bglu_ref,
               o_ref, ub_scr, ut_scr, v_scr, st_scr, yt_scr, *, tl, batch, width, half):
    ci = pl.program_id(0)

    @pl.when(ci == 0)
    def _():
        st_scr[...] = jnp.zeros_like(st_scr)

    planes = width // LANES
    for b in range(batch):
        for p in range(planes):
            ub_scr[p, b * tl:(b + 1) * tl, :] = u_ref[b, :, p * LANES:(p + 1) * LANES]
    for t in range(tl):
        for p in range(planes):
            ut_scr[t * batch:(t + 1) * batch, p * LANES:(p + 1) * LANES] = (
                ub_scr[p, pl.ds(t, batch, stride=tl), :])

    ut = ut_scr[...]
    utb = ut.astype(BF16)
    tile = 2 * LANES
    groups_per_tile_cols = tile * width // half
    for part in range(2):
        for jt in range(half // tile):
            u0 = (jt * groups_per_tile_cols) // LANES * LANES
            c0 = part * half + jt * tile
            v_scr[:, c0:c0 + tile] = _dot(utb[:, u0:u0 + LANES], bmat_ref[u0:u0 + LANES, c0:c0 + tile])

    cg_w = 4 * LANES
    for cg in range(half // cg_w):
        re_cols = slice(cg * cg_w, (cg + 1) * cg_w)
        im_cols = slice(half + cg * cg_w, half + (cg + 1) * cg_w)
        a_re = jnp.broadcast_to(are_ref[:, re_cols], (batch, cg_w))
        a_im = jnp.broadcast_to(aim_ref[:, re_cols], (batch, cg_w))

        def step(t, carry, re_cols=re_cols, im_cols=im_cols, a_re=a_re, a_im=a_im):
            s_re, s_im = carry
            r0 = pl.multiple_of(t * batch, batch)
            n_re = a_re * s_re - a_im * s_im + v_scr[pl.ds(r0, batch), re_cols]
            n_im = a_re * s_im + a_im * s_re + v_scr[pl.ds(r0, batch), im_cols]
            v_scr[pl.ds(r0, batch), re_cols] = n_re
            v_scr[pl.ds(r0, batch), im_cols] = n_im
            return n_re, n_im

        s_re, s_im = lax.fori_loop(0, tl, step, (st_scr[:, re_cols], st_scr[:, im_cols]), unroll=8)
        st_scr[:, re_cols] = s_re
        st_scr[:, im_cols] = s_im

    kw = half * tile // width
    for nt in range(width // tile):
        k0 = nt * kw
        y = (_dot(v_scr[:, k0:k0 + kw].astype(BF16), cre_ref[k0:k0 + kw, nt * tile:(nt + 1) * tile])
             - _dot(v_scr[:, half + k0:half + k0 + kw].astype(BF16),
                    cim_ref[k0:k0 + kw, nt * tile:(nt + 1) * tile]))
        yt_scr[:, nt * tile:(nt + 1) * tile] = y
    y = yt_scr[...] + d_ref[...] * ut
    y = 0.5 * y * (1.0 + jnp.tanh(math.sqrt(2.0 / math.pi) * (y + 0.044715 * (y * y * y))))
    y = y * _sigmoid(_dot(y.astype(BF16), wglu_ref[...]) + bglu_ref[...])
    for p in range(planes):
        ub_scr[p] = y[:, p * LANES:(p + 1) * LANES]
    for b in range(batch):
        for p in range(planes):
            o_ref[b, :, p * LANES:(p + 1) * LANES] = (
                ub_scr[p, pl.ds(b, tl, stride=batch), :].astype(o_ref.dtype))


def _s5(u3, bmat, cre, cim, a_re, a_im, d_skip, w_glu, b_glu):
    batch, seq, width = u3.shape
    half = a_re.shape[1]
    tl = S5_TIME_BLOCK
    rows = tl * batch
    blk = pl.BlockSpec((batch, tl, width), lambda ci: (0, ci, 0))
    return pl.pallas_call(
        functools.partial(_s5_kernel, tl=tl, batch=batch, width=width, half=half),
        grid=(seq // tl,),
        in_specs=[blk, _resident(bmat.shape), _resident(cre.shape), _resident(cim.shape),
                  _resident(a_re.shape), _resident(a_im.shape), _resident((1, width)),
                  _resident(w_glu.shape), _resident((1, width))],
        out_specs=blk,
        out_shape=jax.ShapeDtypeStruct(u3.shape, BF16),
        scratch_shapes=[
            pltpu.VMEM((width // LANES, rows, LANES), F32),
            pltpu.VMEM((rows, width), F32),
            pltpu.VMEM((rows, 2 * half), F32),
            pltpu.VMEM((batch, 2 * half), F32),
            pltpu.VMEM((rows, width), F32),
        ],
        compiler_params=_params(("arbitrary",), 40),
        name="s5_scan",
    )(u3, bmat, cre, cim, a_re, a_im, d_skip, w_glu, b_glu)


def _out_mlp_kernel(a_ref, b_ref, x_ref, wa_ref, wb_ref, g_ref, wup_ref, wdn_ref, gf_ref, o_ref,
                    acc_scr, h_scr, *, final_norm):
    x1 = x_ref[...] + _dot(a_ref[...], wa_ref[...]) + _dot(b_ref[...], wb_ref[...])
    h_scr[...] = _rms_norm(x1, g_ref[...]).astype(BF16)
    acc_scr[...] = x1

    def ff_chunk(c, carry):
        t = jnp.maximum(_dot(h_scr[...], wup_ref[c]), 0.0)
        acc_scr[...] += _dot((t * t).astype(BF16), wdn_ref[c])
        return carry

    lax.fori_loop(0, wup_ref.shape[0], ff_chunk, 0)
    acc = acc_scr[...]
    if final_norm:
        acc = _rms_norm(acc, gf_ref[...])
    o_ref[...] = acc


def _out_mlp(a, b, x2, wa, wb, g, w_up, w_down, g_final, final_norm):
    n, d = x2.shape
    tm = ROW_BLOCK
    row = lambda i: (i, 0)
    chunks = w_up.shape[1] // MLP_FF_BLOCK
    w_up = w_up.reshape(d, chunks, MLP_FF_BLOCK).transpose(1, 0, 2)
    w_down = w_down.reshape(chunks, MLP_FF_BLOCK, d)
    return pl.pallas_call(
        functools.partial(_out_mlp_kernel, final_norm=final_norm),
        grid=(n // tm,),
        in_specs=[
            pl.BlockSpec((tm, a.shape[1]), row),
            pl.BlockSpec((tm, b.shape[1]), row),
            pl.BlockSpec((tm, d), row),
            _resident(wa.shape), _resident(wb.shape), _resident((1, d)),
            _resident(w_up.shape), _resident(w_down.shape), _resident((1, d)),
        ],
        out_specs=pl.BlockSpec((tm, d), row),
        out_shape=jax.ShapeDtypeStruct((n, d), F32),
        scratch_shapes=[pltpu.VMEM((tm, d), F32), pltpu.VMEM((tm, d), BF16)],
        compiler_params=_params(("arbitrary",), 48),
        name="out_mlp",
    )(a, b, x2, wa, wb, g, w_up, w_down, g_final)


def _rope_table_kernel(inv_ref, cos_ref, sin_ref):
    rows, lanes = cos_ref.shape
    pos = lax.broadcasted_iota(jnp.int32, (rows, lanes), 0).astype(F32)
    lane = lax.broadcasted_iota(jnp.int32, (rows, lanes), 1)
    ang = pos * inv_ref[...]
    cos_ref[...] = jnp.cos(ang)
    sin = jnp.sin(ang)
    sin_ref[...] = jnp.where(lane < lanes // 2, -sin, sin)


def _rope_tables(seq, head_dim):
    inv = 1.0 / (10000.0 ** jnp.linspace(0.0, 1.0, head_dim // 2, dtype=F32))
    inv2 = jnp.concatenate([inv, inv])[None, :]
    out = jax.ShapeDtypeStruct((seq, head_dim), F32)
    return pl.pallas_call(_rope_table_kernel, out_shape=[out, out], name="rope_tables")(inv2)


def _odd_in_kernel(x_ref, g_ref, wconv_ref, wq_ref, wk_ref, wv_ref, wg_ref, cw_ref, cos_ref, sin_ref,
                   conv_ref, q_ref, k_ref, v_ref, gate_ref, z_scr, *, blocks_per_seq, taps, head_dim):
    i = pl.program_id(0)
    tm = x_ref.shape[0]
    cw = conv_ref.shape[1]
    pad = SUBLANES

    @pl.when(i % blocks_per_seq == 0)
    def _():
        z_scr[0:pad, :] = jnp.zeros((pad, cw), F32)

    h = _rms_norm(x_ref[...], g_ref[...]).astype(BF16)
    hc = _dot(h, wconv_ref[:, 0:cw])
    gate_b = _dot(h, wconv_ref[:, cw:2 * cw])
    gate_c = _dot(h, wconv_ref[:, 2 * cw:3 * cw])
    z_scr[pad:pad + tm, :] = gate_c * hc
    conv = cw_ref[taps - 1:taps, :] * z_scr[pad:pad + tm, :]
    for j in range(taps - 1):
        shift = taps - 1 - j
        conv = conv + cw_ref[j:j + 1, :] * z_scr[pad - shift:pad - shift + tm, :]
    conv_ref[...] = (gate_b * conv).astype(conv_ref.dtype)
    z_scr[0:pad, :] = z_scr[tm:tm + pad, :]

    cos = cos_ref[...]
    sin = sin_ref[...]

    def rotate(w_ref, out_ref, scale):
        x = _dot(h, w_ref[...])
        for hd in range(x.shape[1] // head_dim):
            xh = x[:, hd * head_dim:(hd + 1) * head_dim]
            r = xh * cos + pltpu.roll(xh, head_dim // 2, 1) * sin
            out_ref[:, hd * head_dim:(hd + 1) * head_dim] = (r * scale).astype(out_ref.dtype)

    rotate(wq_ref, q_ref, 1.0)
    rotate(wk_ref, k_ref, head_dim ** -0.5)
    v_ref[...] = _dot(h, wv_ref[...]).astype(v_ref.dtype)
    gate_ref[...] = _dot(h, wg_ref[...]).astype(gate_ref.dtype)


def _odd_in(x2, g, wconv, wq, wk, wv, wg, conv_w, cos, sin, seq):
    n, d = x2.shape
    tm = ROW_BLOCK
    nb = seq // tm
    cw = conv_w.shape[1]
    rw = wq.shape[1]
    head_dim = cos.shape[1]
    row = lambda i: (i, 0)
    tab = pl.BlockSpec((tm, head_dim), lambda i: (i % nb, 0))
    o = lambda w: jax.ShapeDtypeStruct((n, w), BF16)
    return pl.pallas_call(
        functools.partial(_odd_in_kernel, blocks_per_seq=nb, taps=conv_w.shape[0], head_dim=head_dim),
        grid=(n // tm,),
        in_specs=[pl.BlockSpec((tm, d), row), _resident((1, d)), _resident(wconv.shape),
                  _resident(wq.shape), _resident(wk.shape), _resident(wv.shape), _resident(wg.shape),
                  _resident(conv_w.shape), tab, tab],
        out_specs=[pl.BlockSpec((tm, cw), row)] + [pl.BlockSpec((tm, rw), row)] * 4,
        out_shape=[o(cw), o(rw), o(rw), o(rw), o(rw)],
        scratch_shapes=[pltpu.VMEM((tm + 2 * SUBLANES, cw), F32)],
        compiler_params=_params(("arbitrary",), 40),
        name="odd_in",
    )(x2, g, wconv, wq, wk, wv, wg, conv_w, cos, sin)


def _retention_kernel(q_ref, k_ref, v_ref, g_ref, o_ref, state_scr, *, chunk):
    hd = pl.program_id(1)
    ri = pl.program_id(2)

    @pl.when(ri == 0)
    def _():
        state_scr[...] = jnp.zeros_like(state_scr)

    hf = jnp.full((1, 1), hd, jnp.int32).astype(F32)
    log_gamma = jnp.log(1.0 - jnp.exp2(-5.0 - hf))
    ridx = lax.broadcasted_iota(jnp.int32, (chunk, chunk), 0)
    cidx = lax.broadcasted_iota(jnp.int32, (chunk, chunk), 1)
    rel = (ridx - cidx).astype(F32)
    inner_decay = jnp.where(rel >= 0, jnp.exp(log_gamma * jnp.maximum(rel, 0.0)), 0.0)
    idx = lax.broadcasted_iota(jnp.int32, (chunk, 1), 0).astype(F32)
    query_decay = jnp.exp(log_gamma * (idx + 1.0))
    key_decay = jnp.exp(log_gamma * (chunk - 1.0 - idx))
    chunk_decay = jnp.exp(log_gamma * chunk)

    for c in range(q_ref.shape[1] // chunk):
        rows = slice(c * chunk, (c + 1) * chunk)
        q = q_ref[0, rows, :]
        k = k_ref[0, rows, :]
        v = v_ref[0, rows, :]
        state = state_scr[...]
        s = _dot_nt(q, k) * inner_decay
        ret = _dot(s.astype(BF16), v) + _dot(q, state.astype(BF16)) * query_decay
        kd = (k.astype(F32) * key_decay).astype(BF16)
        state_scr[...] = state * chunk_decay + _dot_tn(kd, v)
        mu = jnp.mean(ret, axis=-1, keepdims=True)
        dev = ret - mu
        var = jnp.mean(dev * dev, axis=-1, keepdims=True)
        g = g_ref[0, rows, :].astype(F32)
        o_ref[0, rows, :] = (g * _sigmoid(g) * (dev * lax.rsqrt(var + EPS))).astype(o_ref.dtype)


def _retention(q3, k3, v3, g3):
    b, seq, rw = q3.shape
    head_dim = rw // RET_HEADS
    tr = RET_ROW_BLOCK
    blk = pl.BlockSpec((1, tr, head_dim), lambda bi, hd, ri: (bi, ri, hd))
    return pl.pallas_call(
        functools.partial(_retention_kernel, chunk=RET_CHUNK),
        grid=(b, RET_HEADS, seq // tr),
        in_specs=[blk, blk, blk, blk],
        out_specs=blk,
        out_shape=jax.ShapeDtypeStruct(q3.shape, BF16),
        scratch_shapes=[pltpu.VMEM((head_dim, head_dim), F32)],
        compiler_params=_params(("arbitrary", "arbitrary", "arbitrary"), 32),
        name="retention",
    )(q3, k3, v3, g3)


def _block_diag(blocks):
    g, r, c = blocks.shape
    eye = jnp.eye(g, dtype=blocks.dtype)
    return (blocks[:, :, None, :] * eye[:, None, :, None]).reshape(g * r, g * c)


def _even_layer(x2, batch, seq, norm_g, w_in, b_forget, log_dt, lam_re, lam_im, b_re, b_im, c_re, c_im,
                d_skip, w_glu, b_glu, w_out, mlp_g, w_up, w_down, g_final, final_norm):
    d = x2.shape[1]
    heads = b_forget.shape[0]
    fw = heads * FOX_HEAD_DIM
    sw = d_skip.shape[0]
    wqkv = w_in[:, :3 * fw].astype(BF16)
    wf = jnp.pad(w_in[:, 3 * fw:3 * fw + heads], ((0, 0), (0, LANES - heads))).astype(BF16)
    wu = w_in[:, 3 * fw + heads:].astype(BF16)
    bf = jnp.pad(b_forget.astype(F32), (0, LANES - heads))[None, :]
    qkv, u, c4 = _even_in(x2, norm_g[None, :], wqkv, wu, wf, bf, seq)
    fox = _fox(qkv.reshape(batch, seq, 3 * fw), c4, heads)

    a_re, a_im, bb_re, bb_im = _s5_params(log_dt, lam_re, lam_im, b_re, b_im)
    bmat = jnp.concatenate([_block_diag(bb_re.transpose(0, 2, 1)), _block_diag(bb_im.transpose(0, 2, 1))],
                           axis=1).astype(BF16)
    cre = _block_diag(c_re.transpose(0, 2, 1)).astype(BF16)
    cim = _block_diag(c_im.transpose(0, 2, 1)).astype(BF16)
    s5 = _s5(u.reshape(batch, seq, sw), bmat, cre, cim, a_re, a_im, d_skip[None, :],
             w_glu.astype(BF16), b_glu[None, :])

    w_out = w_out.astype(BF16)
    return _out_mlp(fox.reshape(batch * seq, fw), s5.reshape(batch * seq, sw), x2, w_out[:fw], w_out[fw:],
                    mlp_g[None, :], w_up.astype(BF16), w_down.astype(BF16), g_final[None, :], final_norm)


def _odd_layer(x2, batch, seq, norm_g, w_in, conv_w, w_out, mlp_g, w_up, w_down, g_final, final_norm):
    d = x2.shape[1]
    cw = conv_w.shape[1]
    rw = d - cw
    head_dim = rw // RET_HEADS
    perm = jnp.concatenate([jnp.arange(0, head_dim, 2), jnp.arange(1, head_dim, 2)])
    perm = (jnp.arange(RET_HEADS)[:, None] * head_dim + perm[None, :]).reshape(-1)
    w_in = w_in.astype(BF16)
    wconv = w_in[:, :3 * cw]
    wq = w_in[:, 3 * cw:3 * cw + rw][:, perm]
    wk = w_in[:, 3 * cw + rw:3 * cw + 2 * rw][:, perm]
    wv = w_in[:, 3 * cw + 2 * rw:3 * cw + 3 * rw]
    wg = w_in[:, 3 * cw + 3 * rw:]
    cos, sin = _rope_tables(seq, head_dim)
    conv, q, k, v, gate = _odd_in(x2, norm_g[None, :], wconv, wq, wk, wv, wg, conv_w.astype(F32), cos, sin, seq)
    to3 = lambda t: t.reshape(batch, seq, rw)
    ret = _retention(to3(q), to3(k), to3(v), to3(gate))
    w_out = w_out.astype(BF16)
    return _out_mlp(conv, ret.reshape(batch * seq, rw), x2, w_out[:cw], w_out[cw:],
                    mlp_g[None, :], w_up.astype(BF16), w_down.astype(BF16), g_final[None, :], final_norm)


def kernel(x, even_norm_mix, even_w_in, even_b_forget, even_s5_log_dt, even_s5_lambda_re, even_s5_lambda_im, even_s5_b_re, even_s5_b_im, even_s5_c_re, even_s5_c_im, even_s5_d, even_s5_w_glu, even_s5_b_glu, even_w_out, odd_norm_mix, odd_w_in, odd_conv_w, odd_w_out, mlp_norm, mlp_w_up, mlp_w_down, final_norm):
    batch, seq, d = x.shape
    depth = mlp_norm.shape[0]
    x2 = x.reshape(batch * seq, d)
    for layer in range(depth):
        j = layer // 2
        last = layer == depth - 1
        if layer % 2 == 0:
            x2 = _even_layer(x2, batch, seq, even_norm_mix[j], even_w_in[j], even_b_forget[j],
                             even_s5_log_dt[j], even_s5_lambda_re[j], even_s5_lambda_im[j],
                             even_s5_b_re[j], even_s5_b_im[j], even_s5_c_re[j], even_s5_c_im[j],
                             even_s5_d[j], even_s5_w_glu[j], even_s5_b_glu[j], even_w_out[j],
                             mlp_norm[layer], mlp_w_up[layer], mlp_w_down[layer], final_norm, last)
        else:
            x2 = _odd_layer(x2, batch, seq, odd_norm_mix[j], odd_w_in[j], odd_conv_w[j], odd_w_out[j],
                            mlp_norm[layer], mlp_w_up[layer], mlp_w_down[layer], final_norm, last)
    return x2.reshape(batch, seq, d)
```

```python
import functools
import math

import jax
import jax.numpy as jnp
from jax import lax
from jax.experimental import pallas as pl
from jax.experimental.pallas import tpu as pltpu

F32 = jnp.float32
BF16 = jnp.bfloat16
EPS = 1e-6

LANES = 128
SUBLANES = 8
FOX_HEAD_DIM = 64
RET_HEADS = 4
RET_CHUNK = 128
ROW_BLOCK = 512
S5_TIME_BLOCK = 64
RET_ROW_BLOCK = 512
MLP_FF_BLOCK = 1024
MIB = 1024 * 1024


def _params(semantics, vmem_mib):
    return pltpu.CompilerParams(dimension_semantics=semantics, vmem_limit_bytes=vmem_mib * MIB)


def _resident(shape):
    return pl.BlockSpec(shape, lambda *_: (0,) * len(shape), pipeline_mode=pl.Buffered(1))


def _rms_norm(x, g):
    return x * lax.rsqrt(jnp.mean(x * x, axis=-1, keepdims=True) + EPS) * g


def _sigmoid(x):
    return 1.0 / (1.0 + jnp.exp(-x))


def _dot(a, b):
    return jnp.dot(a, b, preferred_element_type=F32)


def _dot_nt(a, b):
    return lax.dot_general(a, b, (((1,), (1,)), ((), ())), preferred_element_type=F32)


def _dot_tn(a, b):
    return lax.dot_general(a, b, (((0,), (0,)), ((), ())), preferred_element_type=F32)


def _even_in_kernel(x_ref, g_ref, wqkv_ref, wu_ref, wf_ref, bf_ref, tri_ref,
                    qkv_ref, u_ref, c_ref, carry_ref, *, blocks_per_seq):
    i = pl.program_id(0)

    @pl.when(i % blocks_per_seq == 0)
    def _():
        carry_ref[...] = jnp.zeros_like(carry_ref)

    h = _rms_norm(x_ref[...], g_ref[...]).astype(BF16)
    qkv_ref[...] = _dot(h, wqkv_ref[...]).astype(BF16)
    u_ref[...] = _dot(h, wu_ref[...])
    fl = _dot(h, wf_ref[...]) + bf_ref[...]
    logf = jnp.minimum(fl, 0.0) - jnp.log(1.0 + jnp.exp(-jnp.abs(fl)))
    hi = logf.astype(BF16)
    r1 = logf - hi.astype(F32)
    mid = r1.astype(BF16)
    lo = (r1 - mid.astype(F32)).astype(BF16)
    tri = tri_ref[...]
    cs = _dot(tri, hi) + _dot(tri, mid) + _dot(tri, lo) + carry_ref[...]
    carry_ref[...] = cs[cs.shape[0] - 1:, :]
    c_ref[0, 0] = cs.T[:SUBLANES, :]


def _even_in(x2, g, wqkv, wu, wf, bf, seq):
    n, d = x2.shape
    tm = ROW_BLOCK
    nb = seq // tm
    tri = jnp.tril(jnp.ones((tm, tm), F32)).astype(BF16)
    row = lambda i: (i, 0)
    return pl.pallas_call(
        functools.partial(_even_in_kernel, blocks_per_seq=nb),
        grid=(n // tm,),
        in_specs=[
            pl.BlockSpec((tm, d), row),
            _resident((1, d)),
            _resident(wqkv.shape),
            _resident(wu.shape),
            _resident(wf.shape),
            _resident((1, LANES)),
            _resident((tm, tm)),
        ],
        out_specs=[
            pl.BlockSpec((tm, wqkv.shape[1]), row),
            pl.BlockSpec((tm, wu.shape[1]), row),
            pl.BlockSpec((1, 1, SUBLANES, tm), lambda i: (i // nb, i % nb, 0, 0)),
        ],
        out_shape=[
            jax.ShapeDtypeStruct((n, wqkv.shape[1]), BF16),
            jax.ShapeDtypeStruct((n, wu.shape[1]), F32),
            jax.ShapeDtypeStruct((n // seq, nb, SUBLANES, tm), F32),
        ],
        scratch_shapes=[pltpu.VMEM((1, LANES), F32)],
        compiler_params=_params(("arbitrary",), 40),
        name="even_in",
    )(x2, g, wqkv, wu, wf, bf, tri)


def _fox_kernel(q_ref, k_ref, v_ref, c_ref, o_ref, m_scr, acc_scr, *, blk):
    hp = pl.program_id(1)
    qi = pl.program_id(2)
    lane = lax.broadcasted_iota(jnp.int32, (1, LANES), 1)
    in_head = [(lane >= FOX_HEAD_DIM * hh) & (lane < FOX_HEAD_DIM * (hh + 1)) for hh in range(2)]
    scale = FOX_HEAD_DIM ** -0.5
    q = q_ref[0]
    qm = [jnp.where(in_head[hh], q, 0) * scale for hh in range(2)]
    row = lax.broadcasted_iota(jnp.int32, (blk, blk), 0)
    col = lax.broadcasted_iota(jnp.int32, (blk, blk), 1)
    reps = blk // LANES

    m_scr[...] = jnp.full_like(m_scr, -jnp.inf)
    acc_scr[...] = jnp.zeros_like(acc_scr)

    def block(j, masked):
        k0 = pl.multiple_of(j * blk, blk)
        k = k_ref[0, pl.ds(k0, blk), :]
        v = v_ref[0, pl.ds(k0, blk), :]
        for hh in range(2):
            h = 2 * hp + hh
            c_q = c_ref[0, qi, pl.ds(h, 1), :][:, 0:1]
            c_k = c_ref[0, j, pl.ds(h, 1), :]
            s = _dot_nt(qm[hh], k) + (c_q - c_k)
            if masked:
                s = jnp.where(row >= col, s, -jnp.inf)
            m_prev = m_scr[hh]
            m_new = jnp.maximum(m_prev, jnp.max(s, axis=-1, keepdims=True))
            p = jnp.exp(s - jnp.concatenate([m_new] * reps, axis=1))
            v_ones = jnp.where(in_head[hh], v, 1)
            acc_scr[hh] = jnp.exp(m_prev - m_new) * acc_scr[hh] + _dot(p.astype(BF16), v_ones)
            m_scr[hh] = m_new

    def off_diagonal(j, carry):
        block(j, False)
        return carry

    lax.fori_loop(0, qi, off_diagonal, 0)
    block(qi, True)
    first = lane < FOX_HEAD_DIM
    num = jnp.where(first, acc_scr[0], acc_scr[1])
    den = pltpu.roll(jnp.where(first, acc_scr[1], acc_scr[0]), FOX_HEAD_DIM, 1)
    o_ref[0] = (num / den).astype(o_ref.dtype)


def _fox(qkv3, c4, heads):
    b, seq, _ = qkv3.shape
    blk = ROW_BLOCK
    pairs = heads * FOX_HEAD_DIM // LANES
    return pl.pallas_call(
        functools.partial(_fox_kernel, blk=blk),
        grid=(b, pairs, seq // blk),
        in_specs=[
            pl.BlockSpec((1, blk, LANES), lambda bi, hp, qi: (bi, qi, hp)),
            pl.BlockSpec((1, seq, LANES), lambda bi, hp, qi: (bi, 0, pairs + hp)),
            pl.BlockSpec((1, seq, LANES), lambda bi, hp, qi: (bi, 0, 2 * pairs + hp)),
            pl.BlockSpec((1,) + c4.shape[1:], lambda bi, hp, qi: (bi, 0, 0, 0)),
        ],
        out_specs=pl.BlockSpec((1, blk, LANES), lambda bi, hp, qi: (bi, qi, hp)),
        out_shape=jax.ShapeDtypeStruct((b, seq, pairs * LANES), BF16),
        scratch_shapes=[
            pltpu.VMEM((2, blk, LANES), F32),
            pltpu.VMEM((2, blk, LANES), F32),
        ],
        compiler_params=_params(("arbitrary", "arbitrary", "arbitrary"), 40),
        name="fox_attention",
    )(qkv3, qkv3, qkv3, c4)


def _s5_param_kernel(log_dt_ref, lr_ref, li_ref, br_ref, bi_ref, ar_ref, ai_ref, bbr_ref, bbi_ref):
    dt = jnp.exp(log_dt_ref[...])
    lr = lr_ref[...]
    li = li_ref[...]
    mag = jnp.exp(lr * dt)
    a_re = mag * jnp.cos(li * dt)
    a_im = mag * jnp.sin(li * dt)
    den = lr * lr + li * li
    n_re = a_re - 1.0
    coef_re = (n_re * lr + a_im * li) / den
    coef_im = (a_im * lr - n_re * li) / den
    br = br_ref[...]
    bi = bi_ref[...]
    ar_ref[...] = a_re
    ai_ref[...] = a_im
    bbr_ref[...] = coef_re * br - coef_im * bi
    bbi_ref[...] = coef_re * bi + coef_im * br


def _s5_params(log_dt, lam_re, lam_im, b_re, b_im):
    groups, state, width = b_re.shape
    rep = lambda t: jnp.repeat(t, width, axis=1)
    flat = (groups, state * width)
    out = jax.ShapeDtypeStruct(flat, F32)
    a_re, a_im, bb_re, bb_im = pl.pallas_call(
        _s5_param_kernel,
        out_shape=[out, out, out, out],
        name="s5_discretise",
    )(rep(jnp.broadcast_to(log_dt[:, None], (groups, state))), rep(lam_re), rep(lam_im),
      b_re.reshape(flat), b_im.reshape(flat))
    unrep = lambda t: t.reshape(groups, state, width)[:, :, 0].reshape(1, groups * state)
    return unrep(a_re), unrep(a_im), bb_re.reshape(b_re.shape), bb_im.reshape(b_re.shape)


def _s5_kernel(u_ref, bmat_ref, cre_ref, cim_ref, are_ref, aim_ref, d_ref, wglu_ref, bglu_ref,
               o_ref, ub_scr, ut_scr, v_scr, st_scr, yt_scr, *, tl, batch, width, half):
    ci = pl.program_id(0)

    @pl.when(ci == 0)
    def _():
        st_scr[...] = jnp.zeros_like(st_scr)

    planes = width // LANES
    for b in range(batch):
        for p in range(planes):
            ub_scr[p, b * tl:(b + 1) * tl, :] = u_ref[b, :, p * LANES:(p + 1) * LANES]
    for t in range(tl):
        for p in range(planes):
            ut_scr[t * batch:(t + 1) * batch, p * LANES:(p + 1) * LANES] = (
                ub_scr[p, pl.ds(t, batch, stride=tl), :])

    ut = ut_scr[...]
    utb = ut.astype(BF16)
    tile = 2 * LANES
    groups_per_tile_cols = tile * width // half
    for part in range(2):
        for jt in range(half // tile):
            u0 = (jt * groups_per_tile_cols) // LANES * LANES
            c0 = part * half + jt * tile
            v_scr[:, c0:c0 + tile] = _dot(utb[:, u0:u0 + LANES], bmat_ref[u0:u0 + LANES, c0:c0 + tile])

    cg_w = 4 * LANES
    for cg in range(half // cg_w):
        re_cols = slice(cg * cg_w, (cg + 1) * cg_w)
        im_cols = slice(half + cg * cg_w, half + (cg + 1) * cg_w)
        a_re = jnp.broadcast_to(are_ref[:, re_cols], (batch, cg_w))
        a_im = jnp.broadcast_to(aim_ref[:, re_cols], (batch, cg_w))

        def step(t, carry, re_cols=re_cols, im_cols=im_cols, a_re=a_re, a_im=a_im):
            s_re, s_im = carry
            r0 = pl.multiple_of(t * batch, batch)
            n_re = a_re * s_re - a_im * s_im + v_scr[pl.ds(r0, batch), re_cols]
            n_im = a_re * s_im + a_im * s_re + v_scr[pl.ds(r0, batch), im_cols]
            v_scr[pl.ds(r0, batch), re_cols] = n_re
            v_scr[pl.ds(r0, batch), im_cols] = n_im
            return n_re, n_im

        s_re, s_im = lax.fori_loop(0, tl, step, (st_scr[:, re_cols], st_scr[:, im_cols]), unroll=8)
        st_scr[:, re_cols] = s_re
        st_scr[:, im_cols] = s_im

    kw = half * tile // width
    for nt in range(width // tile):
        k0 = nt * kw
        y = (_dot(v_scr[:, k0:k0 + kw].astype(BF16), cre_ref[k0:k0 + kw, nt * tile:(nt + 1) * tile])
             - _dot(v_scr[:, half + k0:half + k0 + kw].astype(BF16),
                    cim_ref[k0:k0 + kw, nt * tile:(nt + 1) * tile]))
        yt_scr[:, nt * tile:(nt + 1) * tile] = y
    y = yt_scr[...] + d_ref[...] * ut
    y = 0.5 * y * (1.0 + jnp.tanh(math.sqrt(2.0 / math.pi) * (y + 0.044715 * (y * y * y))))
    y = y * _sigmoid(_dot(y.astype(BF16), wglu_ref[...]) + bglu_ref[...])
    for p in range(planes):
        ub_scr[p] = y[:, p * LANES:(p + 1) * LANES]
    for b in range(batch):
        for p in range(planes):
            o_ref[b, :, p * LANES:(p + 1) * LANES] = (
                ub_scr[p, pl.ds(b, tl, stride=batch), :].astype(o_ref.dtype))


def _s5(u3, bmat, cre, cim, a_re, a_im, d_skip, w_glu, b_glu):
    batch, seq, width = u3.shape
    half = a_re.shape[1]
    tl = S5_TIME_BLOCK
    rows = tl * batch
    blk = pl.BlockSpec((batch, tl, width), lambda ci: (0, ci, 0))
    return pl.pallas_call(
        functools.partial(_s5_kernel, tl=tl, batch=batch, width=width, half=half),
        grid=(seq // tl,),
        in_specs=[blk, _resident(bmat.shape), _resident(cre.shape), _resident(cim.shape),
                  _resident(a_re.shape), _resident(a_im.shape), _resident((1, width)),
                  _resident(w_glu.shape), _resident((1, width))],
        out_specs=blk,
        out_shape=jax.ShapeDtypeStruct(u3.shape, BF16),
        scratch_shapes=[
            pltpu.VMEM((width // LANES, rows, LANES), F32),
            pltpu.VMEM((rows, width), F32),
            pltpu.VMEM((rows, 2 * half), F32),
            pltpu.VMEM((batch, 2 * half), F32),
            pltpu.VMEM((rows, width), F32),
        ],
        compiler_params=_params(("arbitrary",), 40),
        name="s5_scan",
    )(u3, bmat, cre, cim, a_re, a_im, d_skip, w_glu, b_glu)


def _out_mlp_kernel(a_ref, b_ref, x_ref, wa_ref, wb_ref, g_ref, wup_ref, wdn_ref, gf_ref, o_ref,
                    acc_scr, h_scr, *, final_norm):
    x1 = x_ref[...] + _dot(a_ref[...], wa_ref[...]) + _dot(b_ref[...], wb_ref[...])
    h_scr[...] = _rms_norm(x1, g_ref[...]).astype(BF16)
    acc_scr[...] = x1

    def ff_chunk(c, carry):
        t = jnp.maximum(_dot(h_scr[...], wup_ref[c]), 0.0)
        acc_scr[...] += _dot((t * t).astype(BF16), wdn_ref[c])
        return carry

    lax.fori_loop(0, wup_ref.shape[0], ff_chunk, 0)
    acc = acc_scr[...]
    if final_norm:
        acc = _rms_norm(acc, gf_ref[...])
    o_ref[...] = acc


def _out_mlp(a, b, x2, wa, wb, g, w_up, w_down, g_final, final_norm):
    n, d = x2.shape
    tm = ROW_BLOCK
    row = lambda i: (i, 0)
    chunks = w_up.shape[1] // MLP_FF_BLOCK
    w_up = w_up.reshape(d, chunks, MLP_FF_BLOCK).transpose(1, 0, 2)
    w_down = w_down.reshape(chunks, MLP_FF_BLOCK, d)
    return pl.pallas_call(
        functools.partial(_out_mlp_kernel, final_norm=final_norm),
        grid=(n // tm,),
        in_specs=[
            pl.BlockSpec((tm, a.shape[1]), row),
            pl.BlockSpec((tm, b.shape[1]), row),
            pl.BlockSpec((tm, d), row),
            _resident(wa.shape), _resident(wb.shape), _resident((1, d)),
            _resident(w_up.shape), _resident(w_down.shape), _resident((1, d)),
        ],
        out_specs=pl.BlockSpec((tm, d), row),
        out_shape=jax.ShapeDtypeStruct((n, d), F32),
        scratch_shapes=[pltpu.VMEM((tm, d), F32), pltpu.VMEM((tm, d), BF16)],
        compiler_params=_params(("arbitrary",), 48),
        name="out_mlp",
    )(a, b, x2, wa, wb, g, w_up, w_down, g_final)


def _rope_table_kernel(inv_ref, cos_ref, sin_ref):
    rows, lanes = cos_ref.shape
    pos = lax.broadcasted_iota(jnp.int32, (rows, lanes), 0).astype(F32)
    lane = lax.broadcasted_iota(jnp.int32, (rows, lanes), 1)
    ang = pos * inv_ref[...]
    cos_ref[...] = jnp.cos(ang)
    sin = jnp.sin(ang)
    sin_ref[...] = jnp.where(lane < lanes // 2, -sin, sin)


def _rope_tables(seq, head_dim):
    inv = 1.0 / (10000.0 ** jnp.linspace(0.0, 1.0, head_dim // 2, dtype=F32))
    inv2 = jnp.concatenate([inv, inv])[None, :]
    out = jax.ShapeDtypeStruct((seq, head_dim), F32)
    return pl.pallas_call(_rope_table_kernel, out_shape=[out, out], name="rope_tables")(inv2)


def _odd_in_kernel(x_ref, g_ref, wconv_ref, wq_ref, wk_ref, wv_ref, wg_ref, cw_ref, cos_ref, sin_ref,
                   conv_ref, q_ref, k_ref, v_ref, gate_ref, z_scr, *, blocks_per_seq, taps, head_dim):
    i = pl.program_id(0)
    tm = x_ref.shape[0]
    cw = conv_ref.shape[1]
    pad = SUBLANES

    @pl.when(i % blocks_per_seq == 0)
    def _():
        z_scr[0:pad, :] = jnp.zeros((pad, cw), F32)

    h = _rms_norm(x_ref[...], g_ref[...]).astype(BF16)
    hc = _dot(h, wconv_ref[:, 0:cw])
    gate_b = _dot(h, wconv_ref[:, cw:2 * cw])
    gate_c = _dot(h, wconv_ref[:, 2 * cw:3 * cw])
    z_scr[pad:pad + tm, :] = gate_c * hc
    conv = cw_ref[taps - 1:taps, :] * z_scr[pad:pad + tm, :]
    for j in range(taps - 1):
        shift = taps - 1 - j
        conv = conv + cw_ref[j:j + 1, :] * z_scr[pad - shift:pad - shift + tm, :]
    conv_ref[...] = (gate_b * conv).astype(conv_ref.dtype)
    z_scr[0:pad, :] = z_scr[tm:tm + pad, :]

    cos = cos_ref[...]
    sin = sin_ref[...]

    def rotate(w_ref, out_ref, scale):
        x = _dot(h, w_ref[...])
        for hd in range(x.shape[1] // head_dim):
            xh = x[:, hd * head_dim:(hd + 1) * head_dim]
            r = xh * cos + pltpu.roll(xh, head_dim // 2, 1) * sin
            out_ref[:, hd * head_dim:(hd + 1) * head_dim] = (r * scale).astype(out_ref.dtype)

    rotate(wq_ref, q_ref, 1.0)
    rotate(wk_ref, k_ref, head_dim ** -0.5)
    v_ref[...] = _dot(h, wv_ref[...]).astype(v_ref.dtype)
    gate_ref[...] = _dot(h, wg_ref[...]).astype(gate_ref.dtype)


def _odd_in(x2, g, wconv, wq, wk, wv, wg, conv_w, cos, sin, seq):
    n, d = x2.shape
    tm = ROW_BLOCK
    nb = seq // tm
    cw = conv_w.shape[1]
    rw = wq.shape[1]
    head_dim = cos.shape[1]
    row = lambda i: (i, 0)
    tab = pl.BlockSpec((tm, head_dim), lambda i: (i % nb, 0))
    o = lambda w: jax.ShapeDtypeStruct((n, w), BF16)
    return pl.pallas_call(
        functools.partial(_odd_in_kernel, blocks_per_seq=nb, taps=conv_w.shape[0], head_dim=head_dim),
        grid=(n // tm,),
        in_specs=[pl.BlockSpec((tm, d), row), _resident((1, d)), _resident(wconv.shape),
                  _resident(wq.shape), _resident(wk.shape), _resident(wv.shape), _resident(wg.shape),
                  _resident(conv_w.shape), tab, tab],
        out_specs=[pl.BlockSpec((tm, cw), row)] + [pl.BlockSpec((tm, rw), row)] * 4,
        out_shape=[o(cw), o(rw), o(rw), o(rw), o(rw)],
        scratch_shapes=[pltpu.VMEM((tm + 2 * SUBLANES, cw), F32)],
        compiler_params=_params(("arbitrary",), 40),
        name="odd_in",
    )(x2, g, wconv, wq, wk, wv, wg, conv_w, cos, sin)


def _retention_kernel(q_ref, k_ref, v_ref, g_ref, o_ref, state_scr, *, chunk):
    hd = pl.program_id(1)
    ri = pl.program_id(2)

    @pl.when(ri == 0)
    def _():
        state_scr[...] = jnp.zeros_like(state_scr)

    hf = jnp.full((1, 1), hd, jnp.int32).astype(F32)
    log_gamma = jnp.log(1.0 - jnp.exp2(-5.0 - hf))
    ridx = lax.broadcasted_iota(jnp.int32, (chunk, chunk), 0)
    cidx = lax.broadcasted_iota(jnp.int32, (chunk, chunk), 1)
    rel = (ridx - cidx).astype(F32)
    inner_decay = jnp.where(rel >= 0, jnp.exp(log_gamma * jnp.maximum(rel, 0.0)), 0.0)
    idx = lax.broadcasted_iota(jnp.int32, (chunk, 1), 0).astype(F32)
    query_decay = jnp.exp(log_gamma * (idx + 1.0))
    key_decay = jnp.exp(log_gamma * (chunk - 1.0 - idx))
    chunk_decay = jnp.exp(log_gamma * chunk)

    for c in range(q_ref.shape[1] // chunk):
        rows = slice(c * chunk, (c + 1) * chunk)
        q = q_ref[0, rows, :]
        k = k_ref[0, rows, :]
        v = v_ref[0, rows, :]
        state = state_scr[...]
        s = _dot_nt(q, k) * inner_decay
        ret = _dot(s.astype(BF16), v) + _dot(q, state.astype(BF16)) * query_decay
        kd = (k.astype(F32) * key_decay).astype(BF16)
        state_scr[...] = state * chunk_decay + _dot_tn(kd, v)
        mu = jnp.mean(ret, axis=-1, keepdims=True)
        dev = ret - mu
        var = jnp.mean(dev * dev, axis=-1, keepdims=True)
        g = g_ref[0, rows, :].astype(F32)
        o_ref[0, rows, :] = (g * _sigmoid(g) * (dev * lax.rsqrt(var + EPS))).astype(o_ref.dtype)


def _retention(q3, k3, v3, g3):
    b, seq, rw = q3.shape
    head_dim = rw // RET_HEADS
    tr = RET_ROW_BLOCK
    blk = pl.BlockSpec((1, tr, head_dim), lambda bi, hd, ri: (bi, ri, hd))
    return pl.pallas_call(
        functools.partial(_retention_kernel, chunk=RET_CHUNK),
        grid=(b, RET_HEADS, seq // tr),
        in_specs=[blk, blk, blk, blk],
        out_specs=blk,
        out_shape=jax.ShapeDtypeStruct(q3.shape, BF16),
        scratch_shapes=[pltpu.VMEM((head_dim, head_dim), F32)],
        compiler_params=_params(("arbitrary", "arbitrary", "arbitrary"), 32),
        name="retention",
    )(q3, k3, v3, g3)


def _block_diag(blocks):
    g, r, c = blocks.shape
    eye = jnp.eye(g, dtype=blocks.dtype)
    return (blocks[:, :, None, :] * eye[:, None, :, None]).reshape(g * r, g * c)


def _even_layer(x2, batch, seq, norm_g, w_in, b_forget, log_dt, lam_re, lam_im, b_re, b_im, c_re, c_im,
                d_skip, w_glu, b_glu, w_out, mlp_g, w_up, w_down, g_final, final_norm):
    d = x2.shape[1]
    heads = b_forget.shape[0]
    fw = heads * FOX_HEAD_DIM
    sw = d_skip.shape[0]
    wqkv = w_in[:, :3 * fw].astype(BF16)
    wf = jnp.pad(w_in[:, 3 * fw:3 * fw + heads], ((0, 0), (0, LANES - heads))).astype(BF16)
    wu = w_in[:, 3 * fw + heads:].astype(BF16)
    bf = jnp.pad(b_forget.astype(F32), (0, LANES - heads))[None, :]
    qkv, u, c4 = _even_in(x2, norm_g[None, :], wqkv, wu, wf, bf, seq)
    fox = _fox(qkv.reshape(batch, seq, 3 * fw), c4, heads)

    a_re, a_im, bb_re, bb_im = _s5_params(log_dt, lam_re, lam_im, b_re, b_im)
    bmat = jnp.concatenate([_block_diag(bb_re.transpose(0, 2, 1)), _block_diag(bb_im.transpose(0, 2, 1))],
                           axis=1).astype(BF16)
    cre = _block_diag(c_re.transpose(0, 2, 1)).astype(BF16)
    cim = _block_diag(c_im.transpose(0, 2, 1)).astype(BF16)
    s5 = _s5(u.reshape(batch, seq, sw), bmat, cre, cim, a_re, a_im, d_skip[None, :],
             w_glu.astype(BF16), b_glu[None, :])

    w_out = w_out.astype(BF16)
    return _out_mlp(fox.reshape(batch * seq, fw), s5.reshape(batch * seq, sw), x2, w_out[:fw], w_out[fw:],
                    mlp_g[None, :], w_up.astype(BF16), w_down.astype(BF16), g_final[None, :], final_norm)


def _odd_layer(x2, batch, seq, norm_g, w_in, conv_w, w_out, mlp_g, w_up, w_down, g_final, final_norm):
    d = x2.shape[1]
    cw = conv_w.shape[1]
    rw = d - cw
    head_dim = rw // RET_HEADS
    perm = jnp.concatenate([jnp.arange(0, head_dim, 2), jnp.arange(1, head_dim, 2)])
    perm = (jnp.arange(RET_HEADS)[:, None] * head_dim + perm[None, :]).reshape(-1)
    w_in = w_in.astype(BF16)
    wconv = w_in[:, :3 * cw]
    wq = w_in[:, 3 * cw:3 * cw + rw][:, perm]
    wk = w_in[:, 3 * cw + rw:3 * cw + 2 * rw][:, perm]
    wv = w_in[:, 3 * cw + 2 * rw:3 * cw + 3 * rw]
    wg = w_in[:, 3 * cw + 3 * rw:]
    cos, sin = _rope_tables(seq, head_dim)
    conv, q, k, v, gate = _odd_in(x2, norm_g[None, :], wconv, wq, wk, wv, wg, conv_w.astype(F32), cos, sin, seq)
    to3 = lambda t: t.reshape(batch, seq, rw)
    ret = _retention(to3(q), to3(k), to3(v), to3(gate))
    w_out = w_out.astype(BF16)
    return _out_mlp(conv, ret.reshape(batch * seq, rw), x2, w_out[:cw], w_out[cw:],
                    mlp_g[None, :], w_up.astype(BF16), w_down.astype(BF16), g_final[None, :], final_norm)


def kernel(x, even_norm_mix, even_w_in, even_b_forget, even_s5_log_dt, even_s5_lambda_re, even_s5_lambda_im, even_s5_b_re, even_s5_b_im, even_s5_c_re, even_s5_c_im, even_s5_d, even_s5_w_glu, even_s5_b_glu, even_w_out, odd_norm_mix, odd_w_in, odd_conv_w, odd_w_out, mlp_norm, mlp_w_up, mlp_w_down, final_norm):
    batch, seq, d = x.shape
    depth = mlp_norm.shape[0]
    x2 = x.reshape(batch * seq, d)
    for layer in range(depth):
        j = layer // 2
        last = layer == depth - 1
        if layer % 2 == 0:
            x2 = _even_layer(x2, batch, seq, even_norm_mix[j], even_w_in[j], even_b_forget[j],
                             even_s5_log_dt[j], even_s5_lambda_re[j], even_s5_lambda_im[j],
                             even_s5_b_re[j], even_s5_b_im[j], even_s5_c_re[j], even_s5_c_im[j],
                             even_s5_d[j], even_s5_w_glu[j], even_s5_b_glu[j], even_w_out[j],
                             mlp_norm[layer], mlp_w_up[layer], mlp_w_down[layer], final_norm, last)
        else:
            x2 = _odd_layer(x2, batch, seq, odd_norm_mix[j], odd_w_in[j], odd_conv_w[j], odd_w_out[j],
                            mlp_norm[layer], mlp_w_up[layer], mlp_w_down[layer], final_norm, last)
    return x2.reshape(batch, seq, d)
```

```python
import functools
import math

import jax
import jax.numpy as jnp
from jax import lax
from jax.experimental import pallas as pl
from jax.experimental.pallas import tpu as pltpu

F32 = jnp.float32
BF16 = jnp.bfloat16
EPS = 1e-6

LANES = 128
SUBLANES = 8
FOX_HEAD_DIM = 64
RET_HEADS = 4
RET_CHUNK = 128
ROW_BLOCK = 512
S5_TIME_BLOCK = 64
RET_ROW_BLOCK = 512
MLP_FF_BLOCK = 1024
MIB = 1024 * 1024


def _params(semantics, vmem_mib):
    return pltpu.CompilerParams(dimension_semantics=semantics, vmem_limit_bytes=vmem_mib * MIB)


def _resident(shape):
    return pl.BlockSpec(shape, lambda *_: (0,) * len(shape), pipeline_mode=pl.Buffered(1))


def _rms_norm(x, g):
    return x * lax.rsqrt(jnp.mean(x * x, axis=-1, keepdims=True) + EPS) * g


def _sigmoid(x):
    return 1.0 / (1.0 + jnp.exp(-x))


def _dot(a, b):
    return jnp.dot(a, b, preferred_element_type=F32)


def _dot_nt(a, b):
    return lax.dot_general(a, b, (((1,), (1,)), ((), ())), preferred_element_type=F32)


def _dot_tn(a, b):
    return lax.dot_general(a, b, (((0,), (0,)), ((), ())), preferred_element_type=F32)


def _even_in_kernel(x_ref, g_ref, wqkv_ref, wu_ref, wf_ref, bf_ref, tri_ref,
                    qkv_ref, u_ref, c_ref, carry_ref, *, blocks_per_seq):
    i = pl.program_id(0)

    @pl.when(i % blocks_per_seq == 0)
    def _():
        carry_ref[...] = jnp.zeros_like(carry_ref)

    h = _rms_norm(x_ref[...], g_ref[...]).astype(BF16)
    qkv_ref[...] = _dot(h, wqkv_ref[...]).astype(BF16)
    u_ref[...] = _dot(h, wu_ref[...])
    fl = _dot(h, wf_ref[...]) + bf_ref[...]
    logf = jnp.minimum(fl, 0.0) - jnp.log(1.0 + jnp.exp(-jnp.abs(fl)))
    hi = logf.astype(BF16)
    r1 = logf - hi.astype(F32)
    mid = r1.astype(BF16)
    lo = (r1 - mid.astype(F32)).astype(BF16)
    tri = tri_ref[...]
    cs = _dot(tri, hi) + _dot(tri, mid) + _dot(tri, lo) + carry_ref[...]
    carry_ref[...] = cs[cs.shape[0] - 1:, :]
    c_ref[0, 0] = cs.T[:SUBLANES, :]


def _even_in(x2, g, wqkv, wu, wf, bf, seq):
    n, d = x2.shape
    tm = ROW_BLOCK
    nb = seq // tm
    tri = jnp.tril(jnp.ones((tm, tm), F32)).astype(BF16)
    row = lambda i: (i, 0)
    return pl.pallas_call(
        functools.partial(_even_in_kernel, blocks_per_seq=nb),
        grid=(n // tm,),
        in_specs=[
            pl.BlockSpec((tm, d), row),
            _resident((1, d)),
            _resident(wqkv.shape),
            _resident(wu.shape),
            _resident(wf.shape),
            _resident((1, LANES)),
            _resident((tm, tm)),
        ],
        out_specs=[
            pl.BlockSpec((tm, wqkv.shape[1]), row),
            pl.BlockSpec((tm, wu.shape[1]), row),
            pl.BlockSpec((1, 1, SUBLANES, tm), lambda i: (i // nb, i % nb, 0, 0)),
        ],
        out_shape=[
            jax.ShapeDtypeStruct((n, wqkv.shape[1]), BF16),
            jax.ShapeDtypeStruct((n, wu.shape[1]), F32),
            jax.ShapeDtypeStruct((n // seq, nb, SUBLANES, tm), F32),
        ],
        scratch_shapes=[pltpu.VMEM((1, LANES), F32)],
        compiler_params=_params(("arbitrary",), 40),
        name="even_in",
    )(x2, g, wqkv, wu, wf, bf, tri)


def _fox_kernel(q_ref, k_ref, v_ref, c_ref, o_ref, m_scr, acc_scr, *, blk):
    hp = pl.program_id(1)
    qi = pl.program_id(2)
    lane = lax.broadcasted_iota(jnp.int32, (1, LANES), 1)
    in_head = [(lane >= FOX_HEAD_DIM * hh) & (lane < FOX_HEAD_DIM * (hh + 1)) for hh in range(2)]
    scale = FOX_HEAD_DIM ** -0.5
    q = q_ref[0]
    qm = [jnp.where(in_head[hh], q, 0) * scale for hh in range(2)]
    row = lax.broadcasted_iota(jnp.int32, (blk, blk), 0)
    col = lax.broadcasted_iota(jnp.int32, (blk, blk), 1)
    reps = blk // LANES

    m_scr[...] = jnp.full_like(m_scr, -jnp.inf)
    acc_scr[...] = jnp.zeros_like(acc_scr)

    def block(j, masked):
        k0 = pl.multiple_of(j * blk, blk)
        k = k_ref[0, pl.ds(k0, blk), :]
        v = v_ref[0, pl.ds(k0, blk), :]
        for hh in range(2):
            h = 2 * hp + hh
            c_q = c_ref[0, qi, pl.ds(h, 1), :][:, 0:1]
            c_k = c_ref[0, j, pl.ds(h, 1), :]
            s = _dot_nt(qm[hh], k) + (c_q - c_k)
            if masked:
                s = jnp.where(row >= col, s, -jnp.inf)
            m_prev = m_scr[hh]
            m_new = jnp.maximum(m_prev, jnp.max(s, axis=-1, keepdims=True))
            p = jnp.exp(s - jnp.concatenate([m_new] * reps, axis=1))
            v_ones = jnp.where(in_head[hh], v, 1)
            acc_scr[hh] = jnp.exp(m_prev - m_new) * acc_scr[hh] + _dot(p.astype(BF16), v_ones)
            m_scr[hh] = m_new

    def off_diagonal(j, carry):
        block(j, False)
        return carry

    lax.fori_loop(0, qi, off_diagonal, 0)
    block(qi, True)
    first = lane < FOX_HEAD_DIM
    num = jnp.where(first, acc_scr[0], acc_scr[1])
    den = pltpu.roll(jnp.where(first, acc_scr[1], acc_scr[0]), FOX_HEAD_DIM, 1)
    o_ref[0] = (num / den).astype(o_ref.dtype)


def _fox(qkv3, c4, heads):
    b, seq, _ = qkv3.shape
    blk = ROW_BLOCK
    pairs = heads * FOX_HEAD_DIM // LANES
    return pl.pallas_call(
        functools.partial(_fox_kernel, blk=blk),
        grid=(b, pairs, seq // blk),
        in_specs=[
            pl.BlockSpec((1, blk, LANES), lambda bi, hp, qi: (bi, qi, hp)),
            pl.BlockSpec((1, seq, LANES), lambda bi, hp, qi: (bi, 0, pairs + hp)),
            pl.BlockSpec((1, seq, LANES), lambda bi, hp, qi: (bi, 0, 2 * pairs + hp)),
            pl.BlockSpec((1,) + c4.shape[1:], lambda bi, hp, qi: (bi, 0, 0, 0)),
        ],
        out_specs=pl.BlockSpec((1, blk, LANES), lambda bi, hp, qi: (bi, qi, hp)),
        out_shape=jax.ShapeDtypeStruct((b, seq, pairs * LANES), BF16),
        scratch_shapes=[
            pltpu.VMEM((2, blk, LANES), F32),
            pltpu.VMEM((2, blk, LANES), F32),
        ],
        compiler_params=_params(("arbitrary", "arbitrary", "arbitrary"), 40),
        name="fox_attention",
    )(qkv3, qkv3, qkv3, c4)


def _s5_param_kernel(log_dt_ref, lr_ref, li_ref, br_ref, bi_ref, ar_ref, ai_ref, bbr_ref, bbi_ref):
    dt = jnp.exp(log_dt_ref[...])
    lr = lr_ref[...]
    li = li_ref[...]
    mag = jnp.exp(lr * dt)
    a_re = mag * jnp.cos(li * dt)
    a_im = mag * jnp.sin(li * dt)
    den = lr * lr + li * li
    n_re = a_re - 1.0
    coef_re = (n_re * lr + a_im * li) / den
    coef_im = (a_im * lr - n_re * li) / den
    br = br_ref[...]
    bi = bi_ref[...]
    ar_ref[...] = a_re
    ai_ref[...] = a_im
    bbr_ref[...] = coef_re * br - coef_im * bi
    bbi_ref[...] = coef_re * bi + coef_im * br


def _s5_params(log_dt, lam_re, lam_im, b_re, b_im):
    groups, state, width = b_re.shape
    rep = lambda t: jnp.repeat(t, width, axis=1)
    flat = (groups, state * width)
    out = jax.ShapeDtypeStruct(flat, F32)
    a_re, a_im, bb_re, bb_im = pl.pallas_call(
        _s5_param_kernel,
        out_shape=[out, out, out, out],
        name="s5_discretise",
    )(rep(jnp.broadcast_to(log_dt[:, None], (groups, state))), rep(lam_re), rep(lam_im),
      b_re.reshape(flat), b_im.reshape(flat))
    unrep = lambda t: t.reshape(groups, state, width)[:, :, 0].reshape(1, groups * state)
    return unrep(a_re), unrep(a_im), bb_re.reshape(b_re.shape), bb_im.reshape(b_re.shape)


def _s5_kernel(u_ref, bmat_ref, cre_ref, cim_ref, are_ref, aim_ref, d_ref, wglu_ref, bglu_ref,
               o_ref, ub_scr, ut_scr, v_scr, st_scr, yt_scr, *, tl, batch, width, half):
    ci = pl.program_id(0)

    @pl.when(ci == 0)
    def _():
        st_scr[...] = jnp.zeros_like(st_scr)

    planes = width // LANES
    for b in range(batch):
        for p in range(planes):
            ub_scr[p, b * tl:(b + 1) * tl, :] = u_ref[b, :, p * LANES:(p + 1) * LANES]
    for t in range(tl):
        for p in range(planes):
            ut_scr[t * batch:(t + 1) * batch, p * LANES:(p + 1) * LANES] = (
                ub_scr[p, pl.ds(t, batch, stride=tl), :])

    ut = ut_scr[...]
    utb = ut.astype(BF16)
    tile = 2 * LANES
    groups_per_tile_cols = tile * width // half
    for part in range(2):
        for jt in range(half // tile):
            u0 = (jt * groups_per_tile_cols) // LANES * LANES
            c0 = part * half + jt * tile
            v_scr[:, c0:c0 + tile] = _dot(utb[:, u0:u0 + LANES], bmat_ref[u0:u0 + LANES, c0:c0 + tile])

    cg_w = 4 * LANES
    for cg in range(half // cg_w):
        re_cols = slice(cg * cg_w, (cg + 1) * cg_w)
        im_cols = slice(half + cg * cg_w, half + (cg + 1) * cg_w)
        a_re = jnp.broadcast_to(are_ref[:, re_cols], (batch, cg_w))
        a_im = jnp.broadcast_to(aim_ref[:, re_cols], (batch, cg_w))

        s_re = st_scr[:, re_cols]
        s_im = st_scr[:, im_cols]
        for t in range(tl):
            rows = slice(t * batch, (t + 1) * batch)
            s_re, s_im = (a_re * s_re - a_im * s_im + v_scr[rows, re_cols],
                          a_re * s_im + a_im * s_re + v_scr[rows, im_cols])
            v_scr[rows, re_cols] = s_re
            v_scr[rows, im_cols] = s_im
        st_scr[:, re_cols] = s_re
        st_scr[:, im_cols] = s_im

    kw = half * tile // width
    for nt in range(width // tile):
        k0 = nt * kw
        y = (_dot(v_scr[:, k0:k0 + kw].astype(BF16), cre_ref[k0:k0 + kw, nt * tile:(nt + 1) * tile])
             - _dot(v_scr[:, half + k0:half + k0 + kw].astype(BF16),
                    cim_ref[k0:k0 + kw, nt * tile:(nt + 1) * tile]))
        yt_scr[:, nt * tile:(nt + 1) * tile] = y
    y = yt_scr[...] + d_ref[...] * ut
    y = 0.5 * y * (1.0 + jnp.tanh(math.sqrt(2.0 / math.pi) * (y + 0.044715 * (y * y * y))))
    y = y * _sigmoid(_dot(y.astype(BF16), wglu_ref[...]) + bglu_ref[...])
    for p in range(planes):
        ub_scr[p] = y[:, p * LANES:(p + 1) * LANES]
    for b in range(batch):
        for p in range(planes):
            o_ref[b, :, p * LANES:(p + 1) * LANES] = (
                ub_scr[p, pl.ds(b, tl, stride=batch), :].astype(o_ref.dtype))


def _s5(u3, bmat, cre, cim, a_re, a_im, d_skip, w_glu, b_glu):
    batch, seq, width = u3.shape
    half = a_re.shape[1]
    tl = S5_TIME_BLOCK
    rows = tl * batch
    blk = pl.BlockSpec((batch, tl, width), lambda ci: (0, ci, 0))
    return pl.pallas_call(
        functools.partial(_s5_kernel, tl=tl, batch=batch, width=width, half=half),
        grid=(seq // tl,),
        in_specs=[blk, _resident(bmat.shape), _resident(cre.shape), _resident(cim.shape),
                  _resident(a_re.shape), _resident(a_im.shape), _resident((1, width)),
                  _resident(w_glu.shape), _resident((1, width))],
        out_specs=blk,
        out_shape=jax.ShapeDtypeStruct(u3.shape, BF16),
        scratch_shapes=[
            pltpu.VMEM((width // LANES, rows, LANES), F32),
            pltpu.VMEM((rows, width), F32),
            pltpu.VMEM((rows, 2 * half), F32),
            pltpu.VMEM((batch, 2 * half), F32),
            pltpu.VMEM((rows, width), F32),
        ],
        compiler_params=_params(("arbitrary",), 40),
        name="s5_scan",
    )(u3, bmat, cre, cim, a_re, a_im, d_skip, w_glu, b_glu)


def _out_mlp_kernel(a_ref, b_ref, x_ref, wa_ref, wb_ref, g_ref, wup_ref, wdn_ref, gf_ref, o_ref,
                    acc_scr, h_scr, *, final_norm):
    x1 = x_ref[...] + _dot(a_ref[...], wa_ref[...]) + _dot(b_ref[...], wb_ref[...])
    h_scr[...] = _rms_norm(x1, g_ref[...]).astype(BF16)
    acc_scr[...] = x1

    def ff_chunk(c, carry):
        t = jnp.maximum(_dot(h_scr[...], wup_ref[c]), 0.0)
        acc_scr[...] += _dot((t * t).astype(BF16), wdn_ref[c])
        return carry

    lax.fori_loop(0, wup_ref.shape[0], ff_chunk, 0)
    acc = acc_scr[...]
    if final_norm:
        acc = _rms_norm(acc, gf_ref[...])
    o_ref[...] = acc


def _out_mlp(a, b, x2, wa, wb, g, w_up, w_down, g_final, final_norm):
    n, d = x2.shape
    tm = ROW_BLOCK
    row = lambda i: (i, 0)
    chunks = w_up.shape[1] // MLP_FF_BLOCK
    w_up = w_up.reshape(d, chunks, MLP_FF_BLOCK).transpose(1, 0, 2)
    w_down = w_down.reshape(chunks, MLP_FF_BLOCK, d)
    return pl.pallas_call(
        functools.partial(_out_mlp_kernel, final_norm=final_norm),
        grid=(n // tm,),
        in_specs=[
            pl.BlockSpec((tm, a.shape[1]), row),
            pl.BlockSpec((tm, b.shape[1]), row),
            pl.BlockSpec((tm, d), row),
            _resident(wa.shape), _resident(wb.shape), _resident((1, d)),
            _resident(w_up.shape), _resident(w_down.shape), _resident((1, d)),
        ],
        out_specs=pl.BlockSpec((tm, d), row),
        out_shape=jax.ShapeDtypeStruct((n, d), F32),
        scratch_shapes=[pltpu.VMEM((tm, d), F32), pltpu.VMEM((tm, d), BF16)],
        compiler_params=_params(("arbitrary",), 48),
        name="out_mlp",
    )(a, b, x2, wa, wb, g, w_up, w_down, g_final)


def _rope_table_kernel(inv_ref, cos_ref, sin_ref):
    rows, lanes = cos_ref.shape
    pos = lax.broadcasted_iota(jnp.int32, (rows, lanes), 0).astype(F32)
    lane = lax.broadcasted_iota(jnp.int32, (rows, lanes), 1)
    ang = pos * inv_ref[...]
    cos_ref[...] = jnp.cos(ang)
    sin = jnp.sin(ang)
    sin_ref[...] = jnp.where(lane < lanes // 2, -sin, sin)


def _rope_tables(seq, head_dim):
    inv = 1.0 / (10000.0 ** jnp.linspace(0.0, 1.0, head_dim // 2, dtype=F32))
    inv2 = jnp.concatenate([inv, inv])[None, :]
    out = jax.ShapeDtypeStruct((seq, head_dim), F32)
    return pl.pallas_call(_rope_table_kernel, out_shape=[out, out], name="rope_tables")(inv2)


def _odd_in_kernel(x_ref, g_ref, wconv_ref, wq_ref, wk_ref, wv_ref, wg_ref, cw_ref, cos_ref, sin_ref,
                   conv_ref, q_ref, k_ref, v_ref, gate_ref, z_scr, *, blocks_per_seq, taps, head_dim):
    i = pl.program_id(0)
    tm = x_ref.shape[0]
    cw = conv_ref.shape[1]
    pad = SUBLANES

    @pl.when(i % blocks_per_seq == 0)
    def _():
        z_scr[0:pad, :] = jnp.zeros((pad, cw), F32)

    h = _rms_norm(x_ref[...], g_ref[...]).astype(BF16)
    hc = _dot(h, wconv_ref[:, 0:cw])
    gate_b = _dot(h, wconv_ref[:, cw:2 * cw])
    gate_c = _dot(h, wconv_ref[:, 2 * cw:3 * cw])
    z_scr[pad:pad + tm, :] = gate_c * hc
    conv = cw_ref[taps - 1:taps, :] * z_scr[pad:pad + tm, :]
    for j in range(taps - 1):
        shift = taps - 1 - j
        conv = conv + cw_ref[j:j + 1, :] * z_scr[pad - shift:pad - shift + tm, :]
    conv_ref[...] = (gate_b * conv).astype(conv_ref.dtype)
    z_scr[0:pad, :] = z_scr[tm:tm + pad, :]

    cos = cos_ref[...]
    sin = sin_ref[...]

    def rotate(w_ref, out_ref, scale):
        x = _dot(h, w_ref[...])
        for hd in range(x.shape[1] // head_dim):
            xh = x[:, hd * head_dim:(hd + 1) * head_dim]
            r = xh * cos + pltpu.roll(xh, head_dim // 2, 1) * sin
            out_ref[:, hd * head_dim:(hd + 1) * head_dim] = (r * scale).astype(out_ref.dtype)

    rotate(wq_ref, q_ref, 1.0)
    rotate(wk_ref, k_ref, head_dim ** -0.5)
    v_ref[...] = _dot(h, wv_ref[...]).astype(v_ref.dtype)
    gate_ref[...] = _dot(h, wg_ref[...]).astype(gate_ref.dtype)


def _odd_in(x2, g, wconv, wq, wk, wv, wg, conv_w, cos, sin, seq):
    n, d = x2.shape
    tm = ROW_BLOCK
    nb = seq // tm
    cw = conv_w.shape[1]
    rw = wq.shape[1]
    head_dim = cos.shape[1]
    row = lambda i: (i, 0)
    tab = pl.BlockSpec((tm, head_dim), lambda i: (i % nb, 0))
    o = lambda w: jax.ShapeDtypeStruct((n, w), BF16)
    return pl.pallas_call(
        functools.partial(_odd_in_kernel, blocks_per_seq=nb, taps=conv_w.shape[0], head_dim=head_dim),
        grid=(n // tm,),
        in_specs=[pl.BlockSpec((tm, d), row), _resident((1, d)), _resident(wconv.shape),
                  _resident(wq.shape), _resident(wk.shape), _resident(wv.shape), _resident(wg.shape),
                  _resident(conv_w.shape), tab, tab],
        out_specs=[pl.BlockSpec((tm, cw), row)] + [pl.BlockSpec((tm, rw), row)] * 4,
        out_shape=[o(cw), o(rw), o(rw), o(rw), o(rw)],
        scratch_shapes=[pltpu.VMEM((tm + 2 * SUBLANES, cw), F32)],
        compiler_params=_params(("arbitrary",), 40),
        name="odd_in",
    )(x2, g, wconv, wq, wk, wv, wg, conv_w, cos, sin)


def _retention_kernel(q_ref, k_ref, v_ref, g_ref, o_ref, state_scr, *, chunk, heads):
    ri = pl.program_id(1)

    @pl.when(ri == 0)
    def _():
        state_scr[...] = jnp.zeros_like(state_scr)

    head_dim = q_ref.shape[2] // heads
    ridx = lax.broadcasted_iota(jnp.int32, (chunk, chunk), 0)
    cidx = lax.broadcasted_iota(jnp.int32, (chunk, chunk), 1)
    rel = (ridx - cidx).astype(F32)
    idx = lax.broadcasted_iota(jnp.int32, (chunk, 1), 0).astype(F32)

    for h in range(heads):
        log_gamma = math.log(1.0 - 2.0 ** (-5.0 - h))
        inner_decay = jnp.where(rel >= 0, jnp.exp(log_gamma * jnp.maximum(rel, 0.0)), 0.0)
        query_decay = jnp.exp(log_gamma * (idx + 1.0))
        key_decay = jnp.exp(log_gamma * (chunk - 1.0 - idx))
        chunk_decay = math.exp(log_gamma * chunk)
        cols = slice(h * head_dim, (h + 1) * head_dim)
        state = state_scr[h]
        for c in range(q_ref.shape[1] // chunk):
            rows = slice(c * chunk, (c + 1) * chunk)
            q = q_ref[0, rows, cols]
            k = k_ref[0, rows, cols]
            v = v_ref[0, rows, cols]
            s = _dot_nt(q, k) * inner_decay
            ret = _dot(s.astype(BF16), v) + _dot(q, state.astype(BF16)) * query_decay
            kd = (k.astype(F32) * key_decay).astype(BF16)
            state = state * chunk_decay + _dot_tn(kd, v)
            mu = jnp.mean(ret, axis=-1, keepdims=True)
            dev = ret - mu
            var = jnp.mean(dev * dev, axis=-1, keepdims=True)
            g = g_ref[0, rows, cols].astype(F32)
            o_ref[0, rows, cols] = (g * _sigmoid(g) * (dev * lax.rsqrt(var + EPS))).astype(o_ref.dtype)
        state_scr[h] = state


def _retention(q3, k3, v3, g3):
    b, seq, rw = q3.shape
    head_dim = rw // RET_HEADS
    tr = RET_ROW_BLOCK
    blk = pl.BlockSpec((1, tr, rw), lambda bi, ri: (bi, ri, 0))
    return pl.pallas_call(
        functools.partial(_retention_kernel, chunk=RET_CHUNK, heads=RET_HEADS),
        grid=(b, seq // tr),
        in_specs=[blk, blk, blk, blk],
        out_specs=blk,
        out_shape=jax.ShapeDtypeStruct(q3.shape, BF16),
        scratch_shapes=[pltpu.VMEM((RET_HEADS, head_dim, head_dim), F32)],
        compiler_params=_params(("arbitrary", "arbitrary"), 32),
        name="retention",
    )(q3, k3, v3, g3)


def _block_diag(blocks):
    g, r, c = blocks.shape
    eye = jnp.eye(g, dtype=blocks.dtype)
    return (blocks[:, :, None, :] * eye[:, None, :, None]).reshape(g * r, g * c)


def _even_layer(x2, batch, seq, norm_g, w_in, b_forget, log_dt, lam_re, lam_im, b_re, b_im, c_re, c_im,
                d_skip, w_glu, b_glu, w_out, mlp_g, w_up, w_down, g_final, final_norm):
    d = x2.shape[1]
    heads = b_forget.shape[0]
    fw = heads * FOX_HEAD_DIM
    sw = d_skip.shape[0]
    wqkv = w_in[:, :3 * fw].astype(BF16)
    wf = jnp.pad(w_in[:, 3 * fw:3 * fw + heads], ((0, 0), (0, LANES - heads))).astype(BF16)
    wu = w_in[:, 3 * fw + heads:].astype(BF16)
    bf = jnp.pad(b_forget.astype(F32), (0, LANES - heads))[None, :]
    qkv, u, c4 = _even_in(x2, norm_g[None, :], wqkv, wu, wf, bf, seq)
    fox = _fox(qkv.reshape(batch, seq, 3 * fw), c4, heads)

    a_re, a_im, bb_re, bb_im = _s5_params(log_dt, lam_re, lam_im, b_re, b_im)
    bmat = jnp.concatenate([_block_diag(bb_re.transpose(0, 2, 1)), _block_diag(bb_im.transpose(0, 2, 1))],
                           axis=1).astype(BF16)
    cre = _block_diag(c_re.transpose(0, 2, 1)).astype(BF16)
    cim = _block_diag(c_im.transpose(0, 2, 1)).astype(BF16)
    s5 = _s5(u.reshape(batch, seq, sw), bmat, cre, cim, a_re, a_im, d_skip[None, :],
             w_glu.astype(BF16), b_glu[None, :])

    w_out = w_out.astype(BF16)
    return _out_mlp(fox.reshape(batch * seq, fw), s5.reshape(batch * seq, sw), x2, w_out[:fw], w_out[fw:],
                    mlp_g[None, :], w_up.astype(BF16), w_down.astype(BF16), g_final[None, :], final_norm)


def _odd_layer(x2, batch, seq, norm_g, w_in, conv_w, w_out, mlp_g, w_up, w_down, g_final, final_norm):
    d = x2.shape[1]
    cw = conv_w.shape[1]
    rw = d - cw
    head_dim = rw // RET_HEADS
    perm = jnp.concatenate([jnp.arange(0, head_dim, 2), jnp.arange(1, head_dim, 2)])
    perm = (jnp.arange(RET_HEADS)[:, None] * head_dim + perm[None, :]).reshape(-1)
    w_in = w_in.astype(BF16)
    wconv = w_in[:, :3 * cw]
    wq = w_in[:, 3 * cw:3 * cw + rw][:, perm]
    wk = w_in[:, 3 * cw + rw:3 * cw + 2 * rw][:, perm]
    wv = w_in[:, 3 * cw + 2 * rw:3 * cw + 3 * rw]
    wg = w_in[:, 3 * cw + 3 * rw:]
    cos, sin = _rope_tables(seq, head_dim)
    conv, q, k, v, gate = _odd_in(x2, norm_g[None, :], wconv, wq, wk, wv, wg, conv_w.astype(F32), cos, sin, seq)
    to3 = lambda t: t.reshape(batch, seq, rw)
    ret = _retention(to3(q), to3(k), to3(v), to3(gate))
    w_out = w_out.astype(BF16)
    return _out_mlp(conv, ret.reshape(batch * seq, rw), x2, w_out[:cw], w_out[cw:],
                    mlp_g[None, :], w_up.astype(BF16), w_down.astype(BF16), g_final[None, :], final_norm)


def kernel(x, even_norm_mix, even_w_in, even_b_forget, even_s5_log_dt, even_s5_lambda_re, even_s5_lambda_im, even_s5_b_re, even_s5_b_im, even_s5_c_re, even_s5_c_im, even_s5_d, even_s5_w_glu, even_s5_b_glu, even_w_out, odd_norm_mix, odd_w_in, odd_conv_w, odd_w_out, mlp_norm, mlp_w_up, mlp_w_down, final_norm):
    batch, seq, d = x.shape
    depth = mlp_norm.shape[0]
    x2 = x.reshape(batch * seq, d)
    for layer in range(depth):
        j = layer // 2
        last = layer == depth - 1
        if layer % 2 == 0:
            x2 = _even_layer(x2, batch, seq, even_norm_mix[j], even_w_in[j], even_b_forget[j],
                             even_s5_log_dt[j], even_s5_lambda_re[j], even_s5_lambda_im[j],
                             even_s5_b_re[j], even_s5_b_im[j], even_s5_c_re[j], even_s5_c_im[j],
                             even_s5_d[j], even_s5_w_glu[j], even_s5_b_glu[j], even_w_out[j],
                             mlp_norm[layer], mlp_w_up[layer], mlp_w_down[layer], final_norm, last)
        else:
            x2 = _odd_layer(x2, batch, seq, odd_norm_mix[j], odd_w_in[j], odd_conv_w[j], odd_w_out[j],
                            mlp_norm[layer], mlp_w_up[layer], mlp_w_down[layer], final_norm, last)
    return x2.reshape(batch, seq, d)
```

```python
import functools
import math

import jax
import jax.numpy as jnp
from jax import lax
from jax.experimental import pallas as pl
from jax.experimental.pallas import tpu as pltpu

F32 = jnp.float32
BF16 = jnp.bfloat16
EPS = 1e-6

LANES = 128
SUBLANES = 8
FOX_HEAD_DIM = 64
RET_HEADS = 4
RET_CHUNK = 256
ROW_BLOCK = 512
S5_TIME_BLOCK = 64
RET_ROW_BLOCK = 512
MLP_FF_BLOCK = 1024
MIB = 1024 * 1024
LOG2_E = math.log2(math.e)
FOX_PAIRS_PER_STEP = 4


def _params(semantics, vmem_mib):
    return pltpu.CompilerParams(dimension_semantics=semantics, vmem_limit_bytes=vmem_mib * MIB)


def _resident(shape):
    return pl.BlockSpec(shape, lambda *_: (0,) * len(shape), pipeline_mode=pl.Buffered(1))


def _rms_norm(x, g):
    return x * lax.rsqrt(jnp.mean(x * x, axis=-1, keepdims=True) + EPS) * g


def _sigmoid(x):
    return 1.0 / (1.0 + jnp.exp(-x))


def _dot(a, b):
    return jnp.dot(a, b, preferred_element_type=F32)


def _dot_nt(a, b):
    return lax.dot_general(a, b, (((1,), (1,)), ((), ())), preferred_element_type=F32)


def _dot_tn(a, b):
    return lax.dot_general(a, b, (((0,), (0,)), ((), ())), preferred_element_type=F32)


def _even_in_kernel(x_ref, g_ref, wqkv_ref, qscale_ref, wu_ref, wf_ref, bf_ref,
                    qkv_ref, u_ref, c_ref, carry_ref, *, blocks_per_seq):
    i = pl.program_id(0)

    @pl.when(i % blocks_per_seq == 0)
    def _():
        carry_ref[...] = jnp.zeros_like(carry_ref)

    h = _rms_norm(x_ref[...], g_ref[...]).astype(BF16)
    qkv_ref[...] = (_dot(h, wqkv_ref[...]) * qscale_ref[...]).astype(BF16)
    u_ref[...] = _dot(h, wu_ref[...])
    fl = _dot(h, wf_ref[...]) + bf_ref[...]
    logf = jnp.minimum(fl, 0.0) - jnp.log(1.0 + jnp.exp(-jnp.abs(fl)))
    x = logf.T[:SUBLANES, :]
    tm = x.shape[1]
    lane = lax.broadcasted_iota(jnp.int32, x.shape, 1)
    shift = 1
    while shift < tm:
        x = x + jnp.where(lane >= shift, pltpu.roll(x, shift, 1), 0.0)
        shift *= 2
    cs = x + jnp.concatenate([carry_ref[...]] * (tm // LANES), axis=1)
    carry_ref[...] = jnp.broadcast_to(cs[:, tm - 1:], carry_ref.shape)
    c_ref[0, 0] = cs


def _even_in(x2, g, wqkv, qscale, wu, wf, bf, seq):
    n, d = x2.shape
    tm = ROW_BLOCK
    nb = seq // tm
    row = lambda i: (i, 0)
    return pl.pallas_call(
        functools.partial(_even_in_kernel, blocks_per_seq=nb),
        grid=(n // tm,),
        in_specs=[
            pl.BlockSpec((tm, d), row),
            _resident((1, d)),
            _resident(wqkv.shape),
            _resident((1, wqkv.shape[1])),
            _resident(wu.shape),
            _resident(wf.shape),
            _resident((1, LANES)),
        ],
        out_specs=[
            pl.BlockSpec((tm, wqkv.shape[1]), row),
            pl.BlockSpec((tm, wu.shape[1]), row),
            pl.BlockSpec((1, 1, SUBLANES, tm), lambda i: (i // nb, i % nb, 0, 0)),
        ],
        out_shape=[
            jax.ShapeDtypeStruct((n, wqkv.shape[1]), BF16),
            jax.ShapeDtypeStruct((n, wu.shape[1]), F32),
            jax.ShapeDtypeStruct((n // seq, nb, SUBLANES, tm), F32),
        ],
        scratch_shapes=[pltpu.VMEM((SUBLANES, LANES), F32)],
        compiler_params=_params(("arbitrary",), 40),
        name="even_in",
    )(x2, g, wqkv, qscale, wu, wf, bf)


def _fox_kernel(q_ref, k_ref, v_ref, c_ref, o_ref, m_scr, acc_scr, *, blk, pairs):
    grp = pl.program_id(1)
    qi = pl.program_id(2)
    lane = lax.broadcasted_iota(jnp.int32, (1, LANES), 1)
    in_head = [(lane >= FOX_HEAD_DIM * hh) & (lane < FOX_HEAD_DIM * (hh + 1)) for hh in range(2)]
    row = lax.broadcasted_iota(jnp.int32, (blk, blk), 0)
    col = lax.broadcasted_iota(jnp.int32, (blk, blk), 1)
    reps = blk // LANES
    qm = [[jnp.where(in_head[hh], q_ref[0, :, pr * LANES:(pr + 1) * LANES], 0) for hh in range(2)]
          for pr in range(pairs)]

    m_scr[...] = jnp.full_like(m_scr, -jnp.inf)
    acc_scr[...] = jnp.zeros_like(acc_scr)
    first = lane < FOX_HEAD_DIM
    ones = [jnp.where(in_head[hh], 1.0, 0.0).astype(BF16) for hh in range(2)]

    def block(j, masked):
        k0 = pl.multiple_of(j * blk, blk)
        for pr in range(pairs):
            k = k_ref[0, pl.ds(k0, blk), pr * LANES:(pr + 1) * LANES]
            v = v_ref[0, pl.ds(k0, blk), pr * LANES:(pr + 1) * LANES]
            probs, alphas, weights = [], [], []
            for hh in range(2):
                slot = 2 * pr + hh
                h = 2 * (grp * pairs + pr) + hh
                c_q = c_ref[0, qi, pl.ds(h, 1), :][:, 0:1]
                c_k = c_ref[0, j, pl.ds(h, 1), :]
                s = _dot_nt(qm[pr][hh], k) + (c_q - c_k) * LOG2_E
                if masked:
                    s = jnp.where(row >= col, s, -jnp.inf)
                m_prev = m_scr[slot]
                m_new = jnp.maximum(m_prev, jnp.max(s, axis=-1, keepdims=True))
                probs.append(jnp.exp2((s - jnp.concatenate([m_new] * reps, axis=1)).astype(BF16)))
                alphas.append(jnp.exp2(m_prev - m_new))
                m_scr[slot] = m_new
                weights.append(jnp.concatenate(
                    [jnp.where(in_head[hh], v, 0), jnp.broadcast_to(ones[hh], (blk, LANES))], axis=1))
            alpha = jnp.where(first, alphas[0], alphas[1])
            pv = _dot(jnp.concatenate(probs, axis=1), jnp.concatenate(weights, axis=0))
            acc_scr[pr] = jnp.concatenate([alpha, alpha], axis=1) * acc_scr[pr] + pv

    def off_diagonal(j, carry):
        block(j, False)
        return carry

    lax.fori_loop(0, qi, off_diagonal, 0)
    block(qi, True)
    for pr in range(pairs):
        acc = acc_scr[pr]
        o_ref[0, :, pr * LANES:(pr + 1) * LANES] = (acc[:, :LANES] / acc[:, LANES:]).astype(o_ref.dtype)


def _fox(qkv3, c4, heads):
    b, seq, _ = qkv3.shape
    blk = ROW_BLOCK
    pairs = FOX_PAIRS_PER_STEP
    width = pairs * LANES
    groups = heads * FOX_HEAD_DIM // width
    return pl.pallas_call(
        functools.partial(_fox_kernel, blk=blk, pairs=pairs),
        grid=(b, groups, seq // blk),
        in_specs=[
            pl.BlockSpec((1, blk, width), lambda bi, g, qi: (bi, qi, g)),
            pl.BlockSpec((1, seq, width), lambda bi, g, qi: (bi, 0, groups + g)),
            pl.BlockSpec((1, seq, width), lambda bi, g, qi: (bi, 0, 2 * groups + g)),
            pl.BlockSpec((1,) + c4.shape[1:], lambda bi, g, qi: (bi, 0, 0, 0)),
        ],
        out_specs=pl.BlockSpec((1, blk, width), lambda bi, g, qi: (bi, qi, g)),
        out_shape=jax.ShapeDtypeStruct((b, seq, groups * width), BF16),
        scratch_shapes=[
            pltpu.VMEM((2 * pairs, blk, LANES), F32),
            pltpu.VMEM((pairs, blk, 2 * LANES), F32),
        ],
        compiler_params=_params(("arbitrary", "arbitrary", "arbitrary"), 48),
        name="fox_attention",
    )(qkv3, qkv3, qkv3, c4)


def _s5_param_kernel(log_dt_ref, lr_ref, li_ref, br_ref, bi_ref, ar_ref, ai_ref, bbr_ref, bbi_ref):
    dt = jnp.exp(log_dt_ref[...])
    lr = lr_ref[...]
    li = li_ref[...]
    mag = jnp.exp(lr * dt)
    a_re = mag * jnp.cos(li * dt)
    a_im = mag * jnp.sin(li * dt)
    den = lr * lr + li * li
    n_re = a_re - 1.0
    coef_re = (n_re * lr + a_im * li) / den
    coef_im = (a_im * lr - n_re * li) / den
    br = br_ref[...]
    bi = bi_ref[...]
    ar_ref[...] = a_re
    ai_ref[...] = a_im
    bbr_ref[...] = coef_re * br - coef_im * bi
    bbi_ref[...] = coef_re * bi + coef_im * br


def _s5_params(log_dt, lam_re, lam_im, b_re, b_im):
    groups, state, width = b_re.shape
    rep = lambda t: jnp.repeat(t, width, axis=1)
    flat = (groups, state * width)
    out = jax.ShapeDtypeStruct(flat, F32)
    a_re, a_im, bb_re, bb_im = pl.pallas_call(
        _s5_param_kernel,
        out_shape=[out, out, out, out],
        name="s5_discretise",
    )(rep(jnp.broadcast_to(log_dt[:, None], (groups, state))), rep(lam_re), rep(lam_im),
      b_re.reshape(flat), b_im.reshape(flat))
    unrep = lambda t: t.reshape(groups, state, width)[:, :, 0].reshape(1, groups * state)
    return unrep(a_re), unrep(a_im), bb_re.reshape(b_re.shape), bb_im.reshape(b_re.shape)


def _s5_kernel(u_ref, bmat_ref, cre_ref, cim_ref, are_ref, aim_ref, d_ref, wglu_ref, bglu_ref,
               o_ref, ub_scr, ut_scr, v_scr, st_scr, yt_scr, *, tl, batch, width, half):
    ci = pl.program_id(0)

    @pl.when(ci == 0)
    def _():
        st_scr[...] = jnp.zeros_like(st_scr)

    planes = width // LANES
    for b in range(batch):
        for p in range(planes):
            ub_scr[p, b * tl:(b + 1) * tl, :] = u_ref[b, :, p * LANES:(p + 1) * LANES]
    for t in range(tl):
        for p in range(planes):
            ut_scr[t * batch:(t + 1) * batch, p * LANES:(p + 1) * LANES] = (
                ub_scr[p, pl.ds(t, batch, stride=tl), :])

    ut = ut_scr[...]
    utb = ut.astype(BF16)
    tile = 2 * LANES
    groups_per_tile_cols = tile * width // half
    for part in range(2):
        for jt in range(half // tile):
            u0 = (jt * groups_per_tile_cols) // LANES * LANES
            c0 = part * half + jt * tile
            v_scr[:, c0:c0 + tile] = _dot(utb[:, u0:u0 + LANES], bmat_ref[u0:u0 + LANES, c0:c0 + tile])

    cg_w = 4 * LANES
    for cg in range(half // cg_w):
        re_cols = slice(cg * cg_w, (cg + 1) * cg_w)
        im_cols = slice(half + cg * cg_w, half + (cg + 1) * cg_w)
        a_re = jnp.broadcast_to(are_ref[:, re_cols], (batch, cg_w))
        a_im = jnp.broadcast_to(aim_ref[:, re_cols], (batch, cg_w))

        s_re = st_scr[:, re_cols]
        s_im = st_scr[:, im_cols]
        for t in range(tl):
            rows = slice(t * batch, (t + 1) * batch)
            s_re, s_im = (a_re * s_re - a_im * s_im + v_scr[rows, re_cols],
                          a_re * s_im + a_im * s_re + v_scr[rows, im_cols])
            v_scr[rows, re_cols] = s_re
            v_scr[rows, im_cols] = s_im
        st_scr[:, re_cols] = s_re
        st_scr[:, im_cols] = s_im

    kw = half * tile // width
    for nt in range(width // tile):
        k0 = nt * kw
        y = (_dot(v_scr[:, k0:k0 + kw].astype(BF16), cre_ref[k0:k0 + kw, nt * tile:(nt + 1) * tile])
             - _dot(v_scr[:, half + k0:half + k0 + kw].astype(BF16),
                    cim_ref[k0:k0 + kw, nt * tile:(nt + 1) * tile]))
        yt_scr[:, nt * tile:(nt + 1) * tile] = y
    y = yt_scr[...] + d_ref[...] * ut
    y = 0.5 * y * (1.0 + jnp.tanh(math.sqrt(2.0 / math.pi) * (y + 0.044715 * (y * y * y))))
    y = y * _sigmoid(_dot(y.astype(BF16), wglu_ref[...]) + bglu_ref[...])
    for p in range(planes):
        ub_scr[p] = y[:, p * LANES:(p + 1) * LANES]
    for b in range(batch):
        for p in range(planes):
            o_ref[b, :, p * LANES:(p + 1) * LANES] = (
                ub_scr[p, pl.ds(b, tl, stride=batch), :].astype(o_ref.dtype))


def _s5(u3, bmat, cre, cim, a_re, a_im, d_skip, w_glu, b_glu):
    batch, seq, width = u3.shape
    half = a_re.shape[1]
    tl = S5_TIME_BLOCK
    rows = tl * batch
    blk = pl.BlockSpec((batch, tl, width), lambda ci: (0, ci, 0))
    return pl.pallas_call(
        functools.partial(_s5_kernel, tl=tl, batch=batch, width=width, half=half),
        grid=(seq // tl,),
        in_specs=[blk, _resident(bmat.shape), _resident(cre.shape), _resident(cim.shape),
                  _resident(a_re.shape), _resident(a_im.shape), _resident((1, width)),
                  _resident(w_glu.shape), _resident((1, width))],
        out_specs=blk,
        out_shape=jax.ShapeDtypeStruct(u3.shape, BF16),
        scratch_shapes=[
            pltpu.VMEM((width // LANES, rows, LANES), F32),
            pltpu.VMEM((rows, width), F32),
            pltpu.VMEM((rows, 2 * half), F32),
            pltpu.VMEM((batch, 2 * half), F32),
            pltpu.VMEM((rows, width), F32),
        ],
        compiler_params=_params(("arbitrary",), 40),
        name="s5_scan",
    )(u3, bmat, cre, cim, a_re, a_im, d_skip, w_glu, b_glu)


def _out_mlp_kernel(a_ref, b_ref, x_ref, wa_ref, wb_ref, g_ref, wup_ref, wdn_ref, gf_ref, o_ref,
                    acc_scr, h_scr, *, final_norm):
    x1 = x_ref[...] + _dot(a_ref[...], wa_ref[...]) + _dot(b_ref[...], wb_ref[...])
    h_scr[...] = _rms_norm(x1, g_ref[...]).astype(BF16)
    acc_scr[...] = x1

    def ff_chunk(c, carry):
        t = jnp.maximum(_dot(h_scr[...], wup_ref[c]), 0.0)
        acc_scr[...] += _dot((t * t).astype(BF16), wdn_ref[c])
        return carry

    lax.fori_loop(0, wup_ref.shape[0], ff_chunk, 0)
    acc = acc_scr[...]
    if final_norm:
        acc = _rms_norm(acc, gf_ref[...])
    o_ref[...] = acc


def _out_mlp(a, b, x2, wa, wb, g, w_up, w_down, g_final, final_norm):
    n, d = x2.shape
    tm = ROW_BLOCK
    row = lambda i: (i, 0)
    chunks = w_up.shape[1] // MLP_FF_BLOCK
    w_up = w_up.reshape(d, chunks, MLP_FF_BLOCK).transpose(1, 0, 2)
    w_down = w_down.reshape(chunks, MLP_FF_BLOCK, d)
    return pl.pallas_call(
        functools.partial(_out_mlp_kernel, final_norm=final_norm),
        grid=(n // tm,),
        in_specs=[
            pl.BlockSpec((tm, a.shape[1]), row),
            pl.BlockSpec((tm, b.shape[1]), row),
            pl.BlockSpec((tm, d), row),
            _resident(wa.shape), _resident(wb.shape), _resident((1, d)),
            _resident(w_up.shape), _resident(w_down.shape), _resident((1, d)),
        ],
        out_specs=pl.BlockSpec((tm, d), row),
        out_shape=jax.ShapeDtypeStruct((n, d), F32),
        scratch_shapes=[pltpu.VMEM((tm, d), F32), pltpu.VMEM((tm, d), BF16)],
        compiler_params=_params(("arbitrary",), 48),
        name="out_mlp",
    )(a, b, x2, wa, wb, g, w_up, w_down, g_final)


def _rope_table_kernel(inv_ref, cos_ref, sin_ref):
    rows, lanes = cos_ref.shape
    pos = lax.broadcasted_iota(jnp.int32, (rows, lanes), 0).astype(F32)
    lane = lax.broadcasted_iota(jnp.int32, (rows, lanes), 1)
    ang = pos * inv_ref[...]
    cos_ref[...] = jnp.cos(ang)
    sin = jnp.sin(ang)
    sin_ref[...] = jnp.where(lane < lanes // 2, -sin, sin)


def _rope_tables(seq, head_dim):
    inv = 1.0 / (10000.0 ** jnp.linspace(0.0, 1.0, head_dim // 2, dtype=F32))
    inv2 = jnp.concatenate([inv, inv])[None, :]
    out = jax.ShapeDtypeStruct((seq, head_dim), F32)
    return pl.pallas_call(_rope_table_kernel, out_shape=[out, out], name="rope_tables")(inv2)


def _odd_in_kernel(x_ref, g_ref, wconv_ref, wq_ref, wk_ref, wv_ref, wg_ref, cw_ref, cos_ref, sin_ref,
                   conv_ref, q_ref, k_ref, v_ref, gate_ref, z_scr, *, blocks_per_seq, taps, head_dim):
    i = pl.program_id(0)
    tm = x_ref.shape[0]
    cw = conv_ref.shape[1]
    pad = SUBLANES

    @pl.when(i % blocks_per_seq == 0)
    def _():
        z_scr[0:pad, :] = jnp.zeros((pad, cw), F32)

    h = _rms_norm(x_ref[...], g_ref[...]).astype(BF16)
    hc = _dot(h, wconv_ref[:, 0:cw])
    gate_b = _dot(h, wconv_ref[:, cw:2 * cw])
    gate_c = _dot(h, wconv_ref[:, 2 * cw:3 * cw])
    z_scr[pad:pad + tm, :] = gate_c * hc
    conv = cw_ref[taps - 1:taps, :] * z_scr[pad:pad + tm, :]
    for j in range(taps - 1):
        shift = taps - 1 - j
        conv = conv + cw_ref[j:j + 1, :] * z_scr[pad - shift:pad - shift + tm, :]
    conv_ref[...] = (gate_b * conv).astype(conv_ref.dtype)
    z_scr[0:pad, :] = z_scr[tm:tm + pad, :]

    cos = cos_ref[...]
    sin = sin_ref[...]

    def rotate(w_ref, out_ref, scale):
        x = _dot(h, w_ref[...])
        for hd in range(x.shape[1] // head_dim):
            xh = x[:, hd * head_dim:(hd + 1) * head_dim]
            r = xh * cos + pltpu.roll(xh, head_dim // 2, 1) * sin
            out_ref[:, hd * head_dim:(hd + 1) * head_dim] = (r * scale).astype(out_ref.dtype)

    rotate(wq_ref, q_ref, 1.0)
    rotate(wk_ref, k_ref, head_dim ** -0.5)
    v_ref[...] = _dot(h, wv_ref[...]).astype(v_ref.dtype)
    gate_ref[...] = _dot(h, wg_ref[...]).astype(gate_ref.dtype)


def _odd_in(x2, g, wconv, wq, wk, wv, wg, conv_w, cos, sin, seq):
    n, d = x2.shape
    tm = ROW_BLOCK
    nb = seq // tm
    cw = conv_w.shape[1]
    rw = wq.shape[1]
    head_dim = cos.shape[1]
    row = lambda i: (i, 0)
    tab = pl.BlockSpec((tm, head_dim), lambda i: (i % nb, 0))
    o = lambda w: jax.ShapeDtypeStruct((n, w), BF16)
    return pl.pallas_call(
        functools.partial(_odd_in_kernel, blocks_per_seq=nb, taps=conv_w.shape[0], head_dim=head_dim),
        grid=(n // tm,),
        in_specs=[pl.BlockSpec((tm, d), row), _resident((1, d)), _resident(wconv.shape),
                  _resident(wq.shape), _resident(wk.shape), _resident(wv.shape), _resident(wg.shape),
                  _resident(conv_w.shape), tab, tab],
        out_specs=[pl.BlockSpec((tm, cw), row)] + [pl.BlockSpec((tm, rw), row)] * 4,
        out_shape=[o(cw), o(rw), o(rw), o(rw), o(rw)],
        scratch_shapes=[pltpu.VMEM((tm + 2 * SUBLANES, cw), F32)],
        compiler_params=_params(("arbitrary",), 40),
        name="odd_in",
    )(x2, g, wconv, wq, wk, wv, wg, conv_w, cos, sin)


def _retention_kernel(q_ref, k_ref, v_ref, g_ref, o_ref, state_scr, decay_scr, *, chunk, heads):
    ri = pl.program_id(1)
    head_dim = q_ref.shape[2] // heads
    log_gamma = [math.log(1.0 - 2.0 ** (-5.0 - h)) for h in range(heads)]

    @pl.when(ri == 0)
    def _():
        state_scr[...] = jnp.zeros_like(state_scr)

    @pl.when((pl.program_id(0) == 0) & (ri == 0))
    def _():
        ridx = lax.broadcasted_iota(jnp.int32, (chunk, chunk), 0)
        cidx = lax.broadcasted_iota(jnp.int32, (chunk, chunk), 1)
        rel = (ridx - cidx).astype(F32)
        for h in range(heads):
            decay_scr[h] = jnp.where(rel >= 0, jnp.exp(log_gamma[h] * jnp.maximum(rel, 0.0)), 0.0)

    idx = lax.broadcasted_iota(jnp.int32, (chunk, 1), 0).astype(F32)
    for h in range(heads):
        query_decay = jnp.exp(log_gamma[h] * (idx + 1.0))
        key_decay = jnp.exp(log_gamma[h] * (chunk - 1.0 - idx))
        chunk_decay = math.exp(log_gamma[h] * chunk)
        cols = slice(h * head_dim, (h + 1) * head_dim)
        state = state_scr[h]
        for c in range(q_ref.shape[1] // chunk):
            rows = slice(c * chunk, (c + 1) * chunk)
            q = q_ref[0, rows, cols]
            k = k_ref[0, rows, cols]
            v = v_ref[0, rows, cols]
            s = _dot_nt(q, k) * decay_scr[h]
            ret = _dot(s.astype(BF16), v) + _dot(q, state.astype(BF16)) * query_decay
            kd = (k.astype(F32) * key_decay).astype(BF16)
            state = state * chunk_decay + _dot_tn(kd, v)
            mu = jnp.mean(ret, axis=-1, keepdims=True)
            dev = ret - mu
            var = jnp.mean(dev * dev, axis=-1, keepdims=True)
            g = g_ref[0, rows, cols].astype(F32)
            o_ref[0, rows, cols] = (g * _sigmoid(g) * (dev * lax.rsqrt(var + EPS))).astype(o_ref.dtype)
        state_scr[h] = state


def _retention(q3, k3, v3, g3):
    b, seq, rw = q3.shape
    head_dim = rw // RET_HEADS
    tr = RET_ROW_BLOCK
    blk = pl.BlockSpec((1, tr, rw), lambda bi, ri: (bi, ri, 0))
    return pl.pallas_call(
        functools.partial(_retention_kernel, chunk=RET_CHUNK, heads=RET_HEADS),
        grid=(b, seq // tr),
        in_specs=[blk, blk, blk, blk],
        out_specs=blk,
        out_shape=jax.ShapeDtypeStruct(q3.shape, BF16),
        scratch_shapes=[pltpu.VMEM((RET_HEADS, head_dim, head_dim), F32),
                        pltpu.VMEM((RET_HEADS, RET_CHUNK, RET_CHUNK), F32)],
        compiler_params=_params(("arbitrary", "arbitrary"), 32),
        name="retention",
    )(q3, k3, v3, g3)


def _block_diag(blocks):
    g, r, c = blocks.shape
    eye = jnp.eye(g, dtype=blocks.dtype)
    return (blocks[:, :, None, :] * eye[:, None, :, None]).reshape(g * r, g * c)


def _even_layer(x2, batch, seq, norm_g, w_in, b_forget, log_dt, lam_re, lam_im, b_re, b_im, c_re, c_im,
                d_skip, w_glu, b_glu, w_out, mlp_g, w_up, w_down, g_final, final_norm):
    d = x2.shape[1]
    heads = b_forget.shape[0]
    fw = heads * FOX_HEAD_DIM
    sw = d_skip.shape[0]
    wqkv = w_in[:, :3 * fw].astype(BF16)
    wf = jnp.pad(w_in[:, 3 * fw:3 * fw + heads], ((0, 0), (0, LANES - heads))).astype(BF16)
    wu = w_in[:, 3 * fw + heads:].astype(BF16)
    bf = jnp.pad(b_forget.astype(F32), (0, LANES - heads))[None, :]
    qscale = jnp.concatenate([jnp.full((fw,), LOG2_E * FOX_HEAD_DIM ** -0.5, F32), jnp.ones((2 * fw,), F32)])[None, :]
    qkv, u, c4 = _even_in(x2, norm_g[None, :], wqkv, qscale, wu, wf, bf, seq)
    fox = _fox(qkv.reshape(batch, seq, 3 * fw), c4, heads)

    a_re, a_im, bb_re, bb_im = _s5_params(log_dt, lam_re, lam_im, b_re, b_im)
    bmat = jnp.concatenate([_block_diag(bb_re.transpose(0, 2, 1)), _block_diag(bb_im.transpose(0, 2, 1))],
                           axis=1).astype(BF16)
    cre = _block_diag(c_re.transpose(0, 2, 1)).astype(BF16)
    cim = _block_diag(c_im.transpose(0, 2, 1)).astype(BF16)
    s5 = _s5(u.reshape(batch, seq, sw), bmat, cre, cim, a_re, a_im, d_skip[None, :],
             w_glu.astype(BF16), b_glu[None, :])

    w_out = w_out.astype(BF16)
    return _out_mlp(fox.reshape(batch * seq, fw), s5.reshape(batch * seq, sw), x2, w_out[:fw], w_out[fw:],
                    mlp_g[None, :], w_up.astype(BF16), w_down.astype(BF16), g_final[None, :], final_norm)


def _odd_layer(x2, batch, seq, norm_g, w_in, conv_w, w_out, mlp_g, w_up, w_down, g_final, final_norm):
    d = x2.shape[1]
    cw = conv_w.shape[1]
    rw = d - cw
    head_dim = rw // RET_HEADS
    perm = jnp.concatenate([jnp.arange(0, head_dim, 2), jnp.arange(1, head_dim, 2)])
    perm = (jnp.arange(RET_HEADS)[:, None] * head_dim + perm[None, :]).reshape(-1)
    w_in = w_in.astype(BF16)
    wconv = w_in[:, :3 * cw]
    wq = w_in[:, 3 * cw:3 * cw + rw][:, perm]
    wk = w_in[:, 3 * cw + rw:3 * cw + 2 * rw][:, perm]
    wv = w_in[:, 3 * cw + 2 * rw:3 * cw + 3 * rw]
    wg = w_in[:, 3 * cw + 3 * rw:]
    cos, sin = _rope_tables(seq, head_dim)
    conv, q, k, v, gate = _odd_in(x2, norm_g[None, :], wconv, wq, wk, wv, wg, conv_w.astype(F32), cos, sin, seq)
    to3 = lambda t: t.reshape(batch, seq, rw)
    ret = _retention(to3(q), to3(k), to3(v), to3(gate))
    w_out = w_out.astype(BF16)
    return _out_mlp(conv, ret.reshape(batch * seq, rw), x2, w_out[:cw], w_out[cw:],
                    mlp_g[None, :], w_up.astype(BF16), w_down.astype(BF16), g_final[None, :], final_norm)


def kernel(x, even_norm_mix, even_w_in, even_b_forget, even_s5_log_dt, even_s5_lambda_re, even_s5_lambda_im, even_s5_b_re, even_s5_b_im, even_s5_c_re, even_s5_c_im, even_s5_d, even_s5_w_glu, even_s5_b_glu, even_w_out, odd_norm_mix, odd_w_in, odd_conv_w, odd_w_out, mlp_norm, mlp_w_up, mlp_w_down, final_norm):
    batch, seq, d = x.shape
    depth = mlp_norm.shape[0]
    x2 = x.reshape(batch * seq, d)
    for layer in range(depth):
        j = layer // 2
        last = layer == depth - 1
        if layer % 2 == 0:
            x2 = _even_layer(x2, batch, seq, even_norm_mix[j], even_w_in[j], even_b_forget[j],
                             even_s5_log_dt[j], even_s5_lambda_re[j], even_s5_lambda_im[j],
                             even_s5_b_re[j], even_s5_b_im[j], even_s5_c_re[j], even_s5_c_im[j],
                             even_s5_d[j], even_s5_w_glu[j], even_s5_b_glu[j], even_w_out[j],
                             mlp_norm[layer], mlp_w_up[layer], mlp_w_down[layer], final_norm, last)
        else:
            x2 = _odd_layer(x2, batch, seq, odd_norm_mix[j], odd_w_in[j], odd_conv_w[j], odd_w_out[j],
                            mlp_norm[layer], mlp_w_up[layer], mlp_w_down[layer], final_norm, last)
    return x2.reshape(batch, seq, d)
```

```python
import functools
import math

import jax
import jax.numpy as jnp
from jax import lax
from jax.experimental import pallas as pl
from jax.experimental.pallas import tpu as pltpu

F32 = jnp.float32
BF16 = jnp.bfloat16
EPS = 1e-6

LANES = 128
SUBLANES = 8
FOX_HEAD_DIM = 64
RET_HEADS = 4
RET_CHUNK = 256
ROW_BLOCK = 512
S5_TIME_BLOCK = 64
RET_ROW_BLOCK = 512
MLP_FF_BLOCK = 1024
MIB = 1024 * 1024
LOG2_E = math.log2(math.e)
FOX_PAIRS_PER_STEP = 4


def _params(semantics, vmem_mib):
    return pltpu.CompilerParams(dimension_semantics=semantics, vmem_limit_bytes=vmem_mib * MIB)


def _resident(shape):
    return pl.BlockSpec(shape, lambda *_: (0,) * len(shape), pipeline_mode=pl.Buffered(1))


def _rms_norm(x, g):
    return x * lax.rsqrt(jnp.mean(x * x, axis=-1, keepdims=True) + EPS) * g


def _sigmoid(x):
    return 1.0 / (1.0 + jnp.exp(-x))


def _dot(a, b):
    return jnp.dot(a, b, preferred_element_type=F32)


def _dot_nt(a, b):
    return lax.dot_general(a, b, (((1,), (1,)), ((), ())), preferred_element_type=F32)


def _dot_tn(a, b):
    return lax.dot_general(a, b, (((0,), (0,)), ((), ())), preferred_element_type=F32)


def _even_in_kernel(x_ref, g_ref, wqkv_ref, qscale_ref, wu_ref, wf_ref, bf_ref,
                    qkv_ref, u_ref, c_ref, carry_ref, *, blocks_per_seq):
    i = pl.program_id(0)

    @pl.when(i % blocks_per_seq == 0)
    def _():
        carry_ref[...] = jnp.zeros_like(carry_ref)

    h = _rms_norm(x_ref[...], g_ref[...]).astype(BF16)
    qkv_ref[...] = (_dot(h, wqkv_ref[...]) * qscale_ref[...]).astype(BF16)
    u_ref[...] = _dot(h, wu_ref[...])
    fl = _dot(h, wf_ref[...]) + bf_ref[...]
    logf = jnp.minimum(fl, 0.0) - jnp.log(1.0 + jnp.exp(-jnp.abs(fl)))
    x = logf.T[:SUBLANES, :]
    tm = x.shape[1]
    lane = lax.broadcasted_iota(jnp.int32, x.shape, 1)
    shift = 1
    while shift < tm:
        x = x + jnp.where(lane >= shift, pltpu.roll(x, shift, 1), 0.0)
        shift *= 2
    cs = x + jnp.concatenate([carry_ref[...]] * (tm // LANES), axis=1)
    carry_ref[...] = jnp.broadcast_to(cs[:, tm - 1:], carry_ref.shape)
    c_ref[0, 0] = cs


def _even_in(x2, g, wqkv, qscale, wu, wf, bf, seq):
    n, d = x2.shape
    tm = ROW_BLOCK
    nb = seq // tm
    row = lambda i: (i, 0)
    return pl.pallas_call(
        functools.partial(_even_in_kernel, blocks_per_seq=nb),
        grid=(n // tm,),
        in_specs=[
            pl.BlockSpec((tm, d), row),
            _resident((1, d)),
            _resident(wqkv.shape),
            _resident((1, wqkv.shape[1])),
            _resident(wu.shape),
            _resident(wf.shape),
            _resident((1, LANES)),
        ],
        out_specs=[
            pl.BlockSpec((tm, wqkv.shape[1]), row),
            pl.BlockSpec((tm, wu.shape[1]), row),
            pl.BlockSpec((1, 1, SUBLANES, tm), lambda i: (i // nb, i % nb, 0, 0)),
        ],
        out_shape=[
            jax.ShapeDtypeStruct((n, wqkv.shape[1]), BF16),
            jax.ShapeDtypeStruct((n, wu.shape[1]), F32),
            jax.ShapeDtypeStruct((n // seq, nb, SUBLANES, tm), F32),
        ],
        scratch_shapes=[pltpu.VMEM((SUBLANES, LANES), F32)],
        compiler_params=_params(("arbitrary",), 40),
        name="even_in",
    )(x2, g, wqkv, qscale, wu, wf, bf)


def _fox_kernel(q_ref, k_ref, v_ref, c_ref, o_ref, m_scr, acc_scr, qm_scr, mask_scr, *, blk, pairs):
    grp = pl.program_id(1)
    qi = pl.program_id(2)
    lane = lax.broadcasted_iota(jnp.int32, (1, LANES), 1)
    in_head = [(lane >= FOX_HEAD_DIM * hh) & (lane < FOX_HEAD_DIM * (hh + 1)) for hh in range(2)]
    for pr in range(pairs):
        for hh in range(2):
            qm_scr[2 * pr + hh] = jnp.where(in_head[hh], q_ref[0, :, pr * LANES:(pr + 1) * LANES], 0)

    first = lane < FOX_HEAD_DIM
    ones = [jnp.where(in_head[hh], 1.0, 0.0).astype(BF16) for hh in range(2)]
    half = blk // 2

    @pl.when((pl.program_id(0) == 0) & (grp == 0) & (qi == 0))
    def _():
        row = lax.broadcasted_iota(jnp.int32, (half, half), 0)
        col = lax.broadcasted_iota(jnp.int32, (half, half), 1)
        mask_scr[...] = jnp.where(row >= col, 0.0, -jnp.inf)

    def block(j, r0, nr, nk, diagonal):
        k0 = pl.multiple_of(j * blk, blk)
        rows = slice(r0, r0 + nr)
        for pr in range(pairs):
            k = k_ref[0, pl.ds(k0, nk), pr * LANES:(pr + 1) * LANES]
            v = v_ref[0, pl.ds(k0, nk), pr * LANES:(pr + 1) * LANES]
            probs, alphas, weights = [], [], []
            for hh in range(2):
                slot = 2 * pr + hh
                h = 2 * (grp * pairs + pr) + hh
                c_q = c_ref[0, qi, pl.ds(h, 1), :][:, 0:1]
                c_k = c_ref[0, j, pl.ds(h, 1), pl.ds(0, nk)]
                s = _dot_nt(qm_scr[slot, rows, :], k) + (c_q - c_k) * LOG2_E
                if diagonal:
                    tail = s[:, nk - half:] + mask_scr[...]
                    s = tail if nk == half else jnp.concatenate([s[:, :nk - half], tail], axis=1)
                    m_new = jnp.broadcast_to(jnp.max(s, axis=-1, keepdims=True), (nr, LANES))
                else:
                    m_prev = m_scr[slot, rows, :]
                    m_new = jnp.maximum(m_prev, jnp.max(s, axis=-1, keepdims=True))
                    alphas.append(jnp.exp2(m_prev - m_new))
                probs.append(jnp.exp2((s - jnp.concatenate([m_new] * (nk // LANES), axis=1)).astype(BF16)))
                m_scr[slot, rows, :] = m_new
                weights.append(jnp.concatenate(
                    [jnp.where(in_head[hh], v, 0), jnp.broadcast_to(ones[hh], (nk, LANES))], axis=1))
            pv = _dot(jnp.concatenate(probs, axis=1), jnp.concatenate(weights, axis=0))
            if diagonal:
                acc_scr[pr, rows, :] = pv
            else:
                alpha = jnp.where(first, alphas[0], alphas[1])
                acc_scr[pr, rows, :] = jnp.concatenate([alpha, alpha], axis=1) * acc_scr[pr, rows, :] + pv

    block(qi, 0, half, half, True)
    block(qi, half, half, blk, True)

    def off_diagonal(j, carry):
        block(j, 0, blk, blk, False)
        return carry

    lax.fori_loop(0, qi, off_diagonal, 0)
    for pr in range(pairs):
        acc = acc_scr[pr]
        o_ref[0, :, pr * LANES:(pr + 1) * LANES] = (acc[:, :LANES] / acc[:, LANES:]).astype(o_ref.dtype)


def _fox(qkv3, c4, heads):
    b, seq, _ = qkv3.shape
    blk = ROW_BLOCK
    pairs = FOX_PAIRS_PER_STEP
    width = pairs * LANES
    groups = heads * FOX_HEAD_DIM // width
    return pl.pallas_call(
        functools.partial(_fox_kernel, blk=blk, pairs=pairs),
        grid=(b, groups, seq // blk),
        in_specs=[
            pl.BlockSpec((1, blk, width), lambda bi, g, qi: (bi, qi, g)),
            pl.BlockSpec((1, seq, width), lambda bi, g, qi: (bi, 0, groups + g)),
            pl.BlockSpec((1, seq, width), lambda bi, g, qi: (bi, 0, 2 * groups + g)),
            pl.BlockSpec((1,) + c4.shape[1:], lambda bi, g, qi: (bi, 0, 0, 0)),
        ],
        out_specs=pl.BlockSpec((1, blk, width), lambda bi, g, qi: (bi, qi, g)),
        out_shape=jax.ShapeDtypeStruct((b, seq, groups * width), BF16),
        scratch_shapes=[
            pltpu.VMEM((2 * pairs, blk, LANES), F32),
            pltpu.VMEM((pairs, blk, 2 * LANES), F32),
            pltpu.VMEM((2 * pairs, blk, LANES), BF16),
            pltpu.VMEM((blk // 2, blk // 2), F32),
        ],
        compiler_params=_params(("arbitrary", "arbitrary", "arbitrary"), 48),
        name="fox_attention",
    )(qkv3, qkv3, qkv3, c4)


def _s5_param_kernel(log_dt_ref, lr_ref, li_ref, br_ref, bi_ref, ar_ref, ai_ref, bbr_ref, bbi_ref):
    dt = jnp.exp(log_dt_ref[...])
    lr = lr_ref[...]
    li = li_ref[...]
    mag = jnp.exp(lr * dt)
    a_re = mag * jnp.cos(li * dt)
    a_im = mag * jnp.sin(li * dt)
    den = lr * lr + li * li
    n_re = a_re - 1.0
    coef_re = (n_re * lr + a_im * li) / den
    coef_im = (a_im * lr - n_re * li) / den
    br = br_ref[...]
    bi = bi_ref[...]
    ar_ref[...] = a_re
    ai_ref[...] = a_im
    bbr_ref[...] = coef_re * br - coef_im * bi
    bbi_ref[...] = coef_re * bi + coef_im * br


def _s5_params(log_dt, lam_re, lam_im, b_re, b_im):
    groups, state, width = b_re.shape
    rep = lambda t: jnp.repeat(t, width, axis=1)
    flat = (groups, state * width)
    out = jax.ShapeDtypeStruct(flat, F32)
    a_re, a_im, bb_re, bb_im = pl.pallas_call(
        _s5_param_kernel,
        out_shape=[out, out, out, out],
        name="s5_discretise",
    )(rep(jnp.broadcast_to(log_dt[:, None], (groups, state))), rep(lam_re), rep(lam_im),
      b_re.reshape(flat), b_im.reshape(flat))
    unrep = lambda t: t.reshape(groups, state, width)[:, :, 0].reshape(1, groups * state)
    return unrep(a_re), unrep(a_im), bb_re.reshape(b_re.shape), bb_im.reshape(b_re.shape)


def _s5_kernel(u_ref, bmat_ref, cre_ref, cim_ref, are_ref, aim_ref, d_ref, wglu_ref, bglu_ref,
               o_ref, ub_scr, ut_scr, v_scr, st_scr, yt_scr, *, tl, batch, width, half):
    ci = pl.program_id(0)

    @pl.when(ci == 0)
    def _():
        st_scr[...] = jnp.zeros_like(st_scr)

    planes = width // LANES
    for b in range(batch):
        for p in range(planes):
            ub_scr[p, b * tl:(b + 1) * tl, :] = u_ref[b, :, p * LANES:(p + 1) * LANES]
    for t in range(tl):
        for p in range(planes):
            ut_scr[t * batch:(t + 1) * batch, p * LANES:(p + 1) * LANES] = (
                ub_scr[p, pl.ds(t, batch, stride=tl), :])

    ut = ut_scr[...]
    utb = ut.astype(BF16)
    tile = 2 * LANES
    groups_per_tile_cols = tile * width // half
    for part in range(2):
        for jt in range(half // tile):
            u0 = (jt * groups_per_tile_cols) // LANES * LANES
            c0 = part * half + jt * tile
            v_scr[:, c0:c0 + tile] = _dot(utb[:, u0:u0 + LANES], bmat_ref[u0:u0 + LANES, c0:c0 + tile])

    cg_w = 4 * LANES
    for cg in range(half // cg_w):
        re_cols = slice(cg * cg_w, (cg + 1) * cg_w)
        im_cols = slice(half + cg * cg_w, half + (cg + 1) * cg_w)
        a_re = jnp.broadcast_to(are_ref[:, re_cols], (batch, cg_w))
        a_im = jnp.broadcast_to(aim_ref[:, re_cols], (batch, cg_w))

        s_re = st_scr[:, re_cols]
        s_im = st_scr[:, im_cols]
        for t in range(tl):
            rows = slice(t * batch, (t + 1) * batch)
            s_re, s_im = (a_re * s_re - a_im * s_im + v_scr[rows, re_cols],
                          a_re * s_im + a_im * s_re + v_scr[rows, im_cols])
            v_scr[rows, re_cols] = s_re
            v_scr[rows, im_cols] = s_im
        st_scr[:, re_cols] = s_re
        st_scr[:, im_cols] = s_im

    kw = half * tile // width
    for nt in range(width // tile):
        k0 = nt * kw
        y = (_dot(v_scr[:, k0:k0 + kw].astype(BF16), cre_ref[k0:k0 + kw, nt * tile:(nt + 1) * tile])
             - _dot(v_scr[:, half + k0:half + k0 + kw].astype(BF16),
                    cim_ref[k0:k0 + kw, nt * tile:(nt + 1) * tile]))
        yt_scr[:, nt * tile:(nt + 1) * tile] = y
    y = yt_scr[...] + d_ref[...] * ut
    y = 0.5 * y * (1.0 + jnp.tanh(math.sqrt(2.0 / math.pi) * (y + 0.044715 * (y * y * y))))
    y = y * _sigmoid(_dot(y.astype(BF16), wglu_ref[...]) + bglu_ref[...])
    for p in range(planes):
        ub_scr[p] = y[:, p * LANES:(p + 1) * LANES]
    for b in range(batch):
        for p in range(planes):
            o_ref[b, :, p * LANES:(p + 1) * LANES] = (
                ub_scr[p, pl.ds(b, tl, stride=batch), :].astype(o_ref.dtype))


def _s5(u3, bmat, cre, cim, a_re, a_im, d_skip, w_glu, b_glu):
    batch, seq, width = u3.shape
    half = a_re.shape[1]
    tl = S5_TIME_BLOCK
    rows = tl * batch
    blk = pl.BlockSpec((batch, tl, width), lambda ci: (0, ci, 0))
    return pl.pallas_call(
        functools.partial(_s5_kernel, tl=tl, batch=batch, width=width, half=half),
        grid=(seq // tl,),
        in_specs=[blk, _resident(bmat.shape), _resident(cre.shape), _resident(cim.shape),
                  _resident(a_re.shape), _resident(a_im.shape), _resident((1, width)),
                  _resident(w_glu.shape), _resident((1, width))],
        out_specs=blk,
        out_shape=jax.ShapeDtypeStruct(u3.shape, BF16),
        scratch_shapes=[
            pltpu.VMEM((width // LANES, rows, LANES), F32),
            pltpu.VMEM((rows, width), F32),
            pltpu.VMEM((rows, 2 * half), F32),
            pltpu.VMEM((batch, 2 * half), F32),
            pltpu.VMEM((rows, width), F32),
        ],
        compiler_params=_params(("arbitrary",), 40),
        name="s5_scan",
    )(u3, bmat, cre, cim, a_re, a_im, d_skip, w_glu, b_glu)


def _out_mlp_kernel(a_ref, b_ref, x_ref, wa_ref, wb_ref, g_ref, wup_ref, wdn_ref, gf_ref, o_ref,
                    acc_scr, h_scr, *, final_norm):
    x1 = x_ref[...] + _dot(a_ref[...], wa_ref[...]) + _dot(b_ref[...], wb_ref[...])
    h_scr[...] = _rms_norm(x1, g_ref[...]).astype(BF16)
    acc_scr[...] = x1

    def ff_chunk(c, carry):
        c0 = pl.multiple_of(c * MLP_FF_BLOCK, MLP_FF_BLOCK)
        t = jnp.maximum(_dot(h_scr[...], wup_ref[0, :, pl.ds(c0, MLP_FF_BLOCK)]), 0.0)
        acc_scr[...] += _dot((t * t).astype(BF16), wdn_ref[0, pl.ds(c0, MLP_FF_BLOCK), :])
        return carry

    lax.fori_loop(0, wup_ref.shape[2] // MLP_FF_BLOCK, ff_chunk, 0, unroll=2)
    acc = acc_scr[...]
    if final_norm:
        acc = _rms_norm(acc, gf_ref[...])
    o_ref[...] = acc


def _out_mlp(a, b, x2, wa, wb, g, w_up, w_down, layer, g_final, final_norm):
    n, d = x2.shape
    tm = ROW_BLOCK
    row = lambda i: (i, 0)
    slab = lambda shape: pl.BlockSpec((1,) + shape[1:], lambda i: (layer, 0, 0), pipeline_mode=pl.Buffered(1))
    return pl.pallas_call(
        functools.partial(_out_mlp_kernel, final_norm=final_norm),
        grid=(n // tm,),
        in_specs=[
            pl.BlockSpec((tm, a.shape[1]), row),
            pl.BlockSpec((tm, b.shape[1]), row),
            pl.BlockSpec((tm, d), row),
            _resident(wa.shape), _resident(wb.shape), _resident((1, d)),
            slab(w_up.shape), slab(w_down.shape), _resident((1, d)),
        ],
        out_specs=pl.BlockSpec((tm, d), row),
        out_shape=jax.ShapeDtypeStruct((n, d), F32),
        scratch_shapes=[pltpu.VMEM((tm, d), F32), pltpu.VMEM((tm, d), BF16)],
        compiler_params=_params(("arbitrary",), 48),
        name="out_mlp",
    )(a, b, x2, wa, wb, g, w_up, w_down, g_final)


def _rope_table_kernel(inv_ref, cos_ref, sin_ref):
    rows, lanes = cos_ref.shape
    pos = lax.broadcasted_iota(jnp.int32, (rows, lanes), 0).astype(F32)
    lane = lax.broadcasted_iota(jnp.int32, (rows, lanes), 1)
    ang = pos * inv_ref[...]
    cos_ref[...] = jnp.cos(ang)
    sin = jnp.sin(ang)
    sin_ref[...] = jnp.where(lane < lanes // 2, -sin, sin)


def _rope_tables(seq, head_dim):
    inv = 1.0 / (10000.0 ** jnp.linspace(0.0, 1.0, head_dim // 2, dtype=F32))
    inv2 = jnp.concatenate([inv, inv])[None, :]
    out = jax.ShapeDtypeStruct((seq, head_dim), F32)
    return pl.pallas_call(_rope_table_kernel, out_shape=[out, out], name="rope_tables")(inv2)


def _odd_in_kernel(x_ref, g_ref, wconv_ref, wq_ref, wk_ref, wv_ref, wg_ref, cw_ref, cos_ref, sin_ref,
                   conv_ref, q_ref, k_ref, v_ref, gate_ref, z_scr, *, blocks_per_seq, taps, head_dim):
    i = pl.program_id(0)
    tm = x_ref.shape[0]
    cw = conv_ref.shape[1]
    pad = SUBLANES

    @pl.when(i % blocks_per_seq == 0)
    def _():
        z_scr[0:pad, :] = jnp.zeros((pad, cw), F32)

    h = _rms_norm(x_ref[...], g_ref[...]).astype(BF16)
    hc = _dot(h, wconv_ref[:, 0:cw])
    gate_b = _dot(h, wconv_ref[:, cw:2 * cw])
    gate_c = _dot(h, wconv_ref[:, 2 * cw:3 * cw])
    z_scr[pad:pad + tm, :] = gate_c * hc
    conv = cw_ref[taps - 1:taps, :] * z_scr[pad:pad + tm, :]
    for j in range(taps - 1):
        shift = taps - 1 - j
        conv = conv + cw_ref[j:j + 1, :] * z_scr[pad - shift:pad - shift + tm, :]
    conv_ref[...] = (gate_b * conv).astype(conv_ref.dtype)
    z_scr[0:pad, :] = z_scr[tm:tm + pad, :]

    cos = cos_ref[...]
    sin = sin_ref[...]

    def rotate(w_ref, out_ref, scale):
        x = _dot(h, w_ref[...])
        for hd in range(x.shape[1] // head_dim):
            xh = x[:, hd * head_dim:(hd + 1) * head_dim]
            r = xh * cos + pltpu.roll(xh, head_dim // 2, 1) * sin
            out_ref[:, hd * head_dim:(hd + 1) * head_dim] = (r * scale).astype(out_ref.dtype)

    rotate(wq_ref, q_ref, 1.0)
    rotate(wk_ref, k_ref, head_dim ** -0.5)
    v_ref[...] = _dot(h, wv_ref[...]).astype(v_ref.dtype)
    gate_ref[...] = _dot(h, wg_ref[...]).astype(gate_ref.dtype)


def _odd_in(x2, g, wconv, wq, wk, wv, wg, conv_w, cos, sin, seq):
    n, d = x2.shape
    tm = ROW_BLOCK
    nb = seq // tm
    cw = conv_w.shape[1]
    rw = wq.shape[1]
    head_dim = cos.shape[1]
    row = lambda i: (i, 0)
    tab = pl.BlockSpec((tm, head_dim), lambda i: (i % nb, 0))
    o = lambda w: jax.ShapeDtypeStruct((n, w), BF16)
    return pl.pallas_call(
        functools.partial(_odd_in_kernel, blocks_per_seq=nb, taps=conv_w.shape[0], head_dim=head_dim),
        grid=(n // tm,),
        in_specs=[pl.BlockSpec((tm, d), row), _resident((1, d)), _resident(wconv.shape),
                  _resident(wq.shape), _resident(wk.shape), _resident(wv.shape), _resident(wg.shape),
                  _resident(conv_w.shape), tab, tab],
        out_specs=[pl.BlockSpec((tm, cw), row)] + [pl.BlockSpec((tm, rw), row)] * 4,
        out_shape=[o(cw), o(rw), o(rw), o(rw), o(rw)],
        scratch_shapes=[pltpu.VMEM((tm + 2 * SUBLANES, cw), F32)],
        compiler_params=_params(("arbitrary",), 40),
        name="odd_in",
    )(x2, g, wconv, wq, wk, wv, wg, conv_w, cos, sin)


def _retention_kernel(q_ref, k_ref, v_ref, g_ref, o_ref, state_scr, decay_scr, *, chunk, heads):
    ri = pl.program_id(1)
    head_dim = q_ref.shape[2] // heads
    log_gamma = [math.log(1.0 - 2.0 ** (-5.0 - h)) for h in range(heads)]

    @pl.when(ri == 0)
    def _():
        state_scr[...] = jnp.zeros_like(state_scr)

    @pl.when((pl.program_id(0) == 0) & (ri == 0))
    def _():
        ridx = lax.broadcasted_iota(jnp.int32, (chunk, chunk), 0)
        cidx = lax.broadcasted_iota(jnp.int32, (chunk, chunk), 1)
        rel = (ridx - cidx).astype(F32)
        for h in range(heads):
            decay_scr[h] = jnp.where(rel >= 0, jnp.exp(log_gamma[h] * jnp.maximum(rel, 0.0)), 0.0)

    idx = lax.broadcasted_iota(jnp.int32, (chunk, 1), 0).astype(F32)
    for h in range(heads):
        query_decay = jnp.exp(log_gamma[h] * (idx + 1.0))
        key_decay = jnp.exp(log_gamma[h] * (chunk - 1.0 - idx))
        chunk_decay = math.exp(log_gamma[h] * chunk)
        cols = slice(h * head_dim, (h + 1) * head_dim)
        state = state_scr[h]
        for c in range(q_ref.shape[1] // chunk):
            rows = slice(c * chunk, (c + 1) * chunk)
            q = q_ref[0, rows, cols]
            k = k_ref[0, rows, cols]
            v = v_ref[0, rows, cols]
            s = _dot_nt(q, k) * decay_scr[h]
            ret = _dot(s.astype(BF16), v) + _dot(q, state.astype(BF16)) * query_decay
            kd = (k.astype(F32) * key_decay).astype(BF16)
            state = state * chunk_decay + _dot_tn(kd, v)
            mu = jnp.mean(ret, axis=-1, keepdims=True)
            dev = ret - mu
            var = jnp.mean(dev * dev, axis=-1, keepdims=True)
            g = g_ref[0, rows, cols].astype(F32)
            o_ref[0, rows, cols] = (g * _sigmoid(g) * (dev * lax.rsqrt(var + EPS))).astype(o_ref.dtype)
        state_scr[h] = state


def _retention(q3, k3, v3, g3):
    b, seq, rw = q3.shape
    head_dim = rw // RET_HEADS
    tr = RET_ROW_BLOCK
    blk = pl.BlockSpec((1, tr, rw), lambda bi, ri: (bi, ri, 0))
    return pl.pallas_call(
        functools.partial(_retention_kernel, chunk=RET_CHUNK, heads=RET_HEADS),
        grid=(b, seq // tr),
        in_specs=[blk, blk, blk, blk],
        out_specs=blk,
        out_shape=jax.ShapeDtypeStruct(q3.shape, BF16),
        scratch_shapes=[pltpu.VMEM((RET_HEADS, head_dim, head_dim), F32),
                        pltpu.VMEM((RET_HEADS, RET_CHUNK, RET_CHUNK), F32)],
        compiler_params=_params(("arbitrary", "arbitrary"), 32),
        name="retention",
    )(q3, k3, v3, g3)


def _block_diag(blocks):
    g, r, c = blocks.shape
    eye = jnp.eye(g, dtype=blocks.dtype)
    return (blocks[:, :, None, :] * eye[:, None, :, None]).reshape(g * r, g * c)


def _even_layer(x2, batch, seq, norm_g, w_in, b_forget, log_dt, lam_re, lam_im, b_re, b_im, c_re, c_im,
                d_skip, w_glu, b_glu, w_out, mlp_g, w_up, w_down, layer, g_final, final_norm):
    d = x2.shape[1]
    heads = b_forget.shape[0]
    fw = heads * FOX_HEAD_DIM
    sw = d_skip.shape[0]
    wqkv = w_in[:, :3 * fw].astype(BF16)
    wf = jnp.pad(w_in[:, 3 * fw:3 * fw + heads], ((0, 0), (0, LANES - heads))).astype(BF16)
    wu = w_in[:, 3 * fw + heads:].astype(BF16)
    bf = jnp.pad(b_forget.astype(F32), (0, LANES - heads))[None, :]
    qscale = jnp.concatenate([jnp.full((fw,), LOG2_E * FOX_HEAD_DIM ** -0.5, F32), jnp.ones((2 * fw,), F32)])[None, :]
    qkv, u, c4 = _even_in(x2, norm_g[None, :], wqkv, qscale, wu, wf, bf, seq)
    fox = _fox(qkv.reshape(batch, seq, 3 * fw), c4, heads)

    a_re, a_im, bb_re, bb_im = _s5_params(log_dt, lam_re, lam_im, b_re, b_im)
    bmat = jnp.concatenate([_block_diag(bb_re.transpose(0, 2, 1)), _block_diag(bb_im.transpose(0, 2, 1))],
                           axis=1).astype(BF16)
    cre = _block_diag(c_re.transpose(0, 2, 1)).astype(BF16)
    cim = _block_diag(c_im.transpose(0, 2, 1)).astype(BF16)
    s5 = _s5(u.reshape(batch, seq, sw), bmat, cre, cim, a_re, a_im, d_skip[None, :],
             w_glu.astype(BF16), b_glu[None, :])

    w_out = w_out.astype(BF16)
    return _out_mlp(fox.reshape(batch * seq, fw), s5.reshape(batch * seq, sw), x2, w_out[:fw], w_out[fw:],
                    mlp_g[None, :], w_up, w_down, layer, g_final[None, :], final_norm)


def _odd_layer(x2, batch, seq, norm_g, w_in, conv_w, w_out, mlp_g, w_up, w_down, layer, g_final, final_norm):
    d = x2.shape[1]
    cw = conv_w.shape[1]
    rw = d - cw
    head_dim = rw // RET_HEADS
    perm = jnp.concatenate([jnp.arange(0, head_dim, 2), jnp.arange(1, head_dim, 2)])
    perm = (jnp.arange(RET_HEADS)[:, None] * head_dim + perm[None, :]).reshape(-1)
    w_in = w_in.astype(BF16)
    wconv = w_in[:, :3 * cw]
    wq = w_in[:, 3 * cw:3 * cw + rw][:, perm]
    wk = w_in[:, 3 * cw + rw:3 * cw + 2 * rw][:, perm]
    wv = w_in[:, 3 * cw + 2 * rw:3 * cw + 3 * rw]
    wg = w_in[:, 3 * cw + 3 * rw:]
    cos, sin = _rope_tables(seq, head_dim)
    conv, q, k, v, gate = _odd_in(x2, norm_g[None, :], wconv, wq, wk, wv, wg, conv_w.astype(F32), cos, sin, seq)
    to3 = lambda t: t.reshape(batch, seq, rw)
    ret = _retention(to3(q), to3(k), to3(v), to3(gate))
    w_out = w_out.astype(BF16)
    return _out_mlp(conv, ret.reshape(batch * seq, rw), x2, w_out[:cw], w_out[cw:],
                    mlp_g[None, :], w_up, w_down, layer, g_final[None, :], final_norm)


def kernel(x, even_norm_mix, even_w_in, even_b_forget, even_s5_log_dt, even_s5_lambda_re, even_s5_lambda_im, even_s5_b_re, even_s5_b_im, even_s5_c_re, even_s5_c_im, even_s5_d, even_s5_w_glu, even_s5_b_glu, even_w_out, odd_norm_mix, odd_w_in, odd_conv_w, odd_w_out, mlp_norm, mlp_w_up, mlp_w_down, final_norm):
    batch, seq, d = x.shape
    depth = mlp_norm.shape[0]
    x2 = x.reshape(batch * seq, d)
    w_up = mlp_w_up.astype(BF16)
    w_down = mlp_w_down.astype(BF16)
    for layer in range(depth):
        j = layer // 2
        last = layer == depth - 1
        if layer % 2 == 0:
            x2 = _even_layer(x2, batch, seq, even_norm_mix[j], even_w_in[j], even_b_forget[j],
                             even_s5_log_dt[j], even_s5_lambda_re[j], even_s5_lambda_im[j],
                             even_s5_b_re[j], even_s5_b_im[j], even_s5_c_re[j], even_s5_c_im[j],
                             even_s5_d[j], even_s5_w_glu[j], even_s5_b_glu[j], even_w_out[j],
                             mlp_norm[layer], w_up, w_down, layer, final_norm, last)
        else:
            x2 = _odd_layer(x2, batch, seq, odd_norm_mix[j], odd_w_in[j], odd_conv_w[j], odd_w_out[j],
                            mlp_norm[layer], w_up, w_down, layer, final_norm, last)
    return x2.reshape(batch, seq, d)
```

```python
import functools
import math

import jax
import jax.numpy as jnp
from jax import lax
from jax.experimental import pallas as pl
from jax.experimental.pallas import tpu as pltpu

F32 = jnp.float32
BF16 = jnp.bfloat16
EPS = 1e-6

LANES = 128
SUBLANES = 8
FOX_HEAD_DIM = 64
RET_HEADS = 4
RET_CHUNK = 256
ROW_BLOCK = 512
S5_TIME_BLOCK = 64
RET_ROW_BLOCK = 512
MLP_FF_BLOCK = 1024
MLP_ROW_BLOCK = 1024
MIB = 1024 * 1024
LOG2_E = math.log2(math.e)
FOX_PAIRS_PER_STEP = 4


def _params(semantics, vmem_mib):
    return pltpu.CompilerParams(dimension_semantics=semantics, vmem_limit_bytes=vmem_mib * MIB)


def _resident(shape):
    return pl.BlockSpec(shape, lambda *_: (0,) * len(shape), pipeline_mode=pl.Buffered(1))


def _rms_norm(x, g):
    return x * lax.rsqrt(jnp.mean(x * x, axis=-1, keepdims=True) + EPS) * g


def _sigmoid(x):
    return 1.0 / (1.0 + jnp.exp(-x))


def _dot(a, b):
    return jnp.dot(a, b, preferred_element_type=F32)


def _dot_nt(a, b):
    return lax.dot_general(a, b, (((1,), (1,)), ((), ())), preferred_element_type=F32)


def _dot_tn(a, b):
    return lax.dot_general(a, b, (((0,), (0,)), ((), ())), preferred_element_type=F32)


def _even_in_kernel(x_ref, g_ref, wqkv_ref, qscale_ref, wu_ref, wf_ref, bf_ref,
                    qkv_ref, u_ref, c_ref, carry_ref, *, blocks_per_seq):
    i = pl.program_id(0)

    @pl.when(i % blocks_per_seq == 0)
    def _():
        carry_ref[...] = jnp.zeros_like(carry_ref)

    h = _rms_norm(x_ref[...], g_ref[...]).astype(BF16)
    qkv_ref[...] = (_dot(h, wqkv_ref[...]) * qscale_ref[...]).astype(BF16)
    u_ref[...] = _dot(h, wu_ref[...])
    fl = _dot(h, wf_ref[...]) + bf_ref[...]
    logf = jnp.minimum(fl, 0.0) - jnp.log(1.0 + jnp.exp(-jnp.abs(fl)))
    x = logf.T[:SUBLANES, :]
    tm = x.shape[1]
    lane = lax.broadcasted_iota(jnp.int32, x.shape, 1)
    shift = 1
    while shift < tm:
        x = x + jnp.where(lane >= shift, pltpu.roll(x, shift, 1), 0.0)
        shift *= 2
    cs = x + jnp.concatenate([carry_ref[...]] * (tm // LANES), axis=1)
    carry_ref[...] = jnp.broadcast_to(cs[:, tm - 1:], carry_ref.shape)
    c_ref[0, 0] = cs


def _even_in(x2, g, wqkv, qscale, wu, wf, bf, seq):
    n, d = x2.shape
    tm = ROW_BLOCK
    nb = seq // tm
    row = lambda i: (i, 0)
    return pl.pallas_call(
        functools.partial(_even_in_kernel, blocks_per_seq=nb),
        grid=(n // tm,),
        in_specs=[
            pl.BlockSpec((tm, d), row),
            _resident((1, d)),
            _resident(wqkv.shape),
            _resident((1, wqkv.shape[1])),
            _resident(wu.shape),
            _resident(wf.shape),
            _resident((1, LANES)),
        ],
        out_specs=[
            pl.BlockSpec((tm, wqkv.shape[1]), row),
            pl.BlockSpec((tm, wu.shape[1]), row),
            pl.BlockSpec((1, 1, SUBLANES, tm), lambda i: (i // nb, i % nb, 0, 0)),
        ],
        out_shape=[
            jax.ShapeDtypeStruct((n, wqkv.shape[1]), BF16),
            jax.ShapeDtypeStruct((n, wu.shape[1]), F32),
            jax.ShapeDtypeStruct((n // seq, nb, SUBLANES, tm), F32),
        ],
        scratch_shapes=[pltpu.VMEM((SUBLANES, LANES), F32)],
        compiler_params=_params(("arbitrary",), 40),
        name="even_in",
    )(x2, g, wqkv, qscale, wu, wf, bf)


def _fox_kernel(q_ref, k_ref, v_ref, c_ref, o_ref, m_scr, acc_scr, qm_scr, mask_scr, *, blk, pairs):
    grp = pl.program_id(1)
    qi = pl.program_id(2)
    lane = lax.broadcasted_iota(jnp.int32, (1, LANES), 1)
    in_head = [(lane >= FOX_HEAD_DIM * hh) & (lane < FOX_HEAD_DIM * (hh + 1)) for hh in range(2)]
    for pr in range(pairs):
        for hh in range(2):
            qm_scr[2 * pr + hh] = jnp.where(in_head[hh], q_ref[0, :, pr * LANES:(pr + 1) * LANES], 0)

    first = lane < FOX_HEAD_DIM
    ones = [jnp.where(in_head[hh], 1.0, 0.0).astype(BF16) for hh in range(2)]
    half = blk // 2

    @pl.when((pl.program_id(0) == 0) & (grp == 0) & (qi == 0))
    def _():
        row = lax.broadcasted_iota(jnp.int32, (half, half), 0)
        col = lax.broadcasted_iota(jnp.int32, (half, half), 1)
        mask_scr[...] = jnp.where(row >= col, 0.0, -jnp.inf)

    def block(j, r0, nr, nk, diagonal):
        k0 = pl.multiple_of(j * blk, blk)
        rows = slice(r0, r0 + nr)
        for pr in range(pairs):
            k = k_ref[0, pl.ds(k0, nk), pr * LANES:(pr + 1) * LANES]
            v = v_ref[0, pl.ds(k0, nk), pr * LANES:(pr + 1) * LANES]
            probs, alphas, weights = [], [], []
            for hh in range(2):
                slot = 2 * pr + hh
                h = 2 * (grp * pairs + pr) + hh
                c_q = c_ref[0, qi, pl.ds(h, 1), :][:, 0:1]
                c_k = c_ref[0, j, pl.ds(h, 1), pl.ds(0, nk)]
                s = _dot_nt(qm_scr[slot, rows, :], k) + (c_q - c_k) * LOG2_E
                if diagonal:
                    tail = s[:, nk - half:] + mask_scr[...]
                    s = tail if nk == half else jnp.concatenate([s[:, :nk - half], tail], axis=1)
                    m_new = jnp.broadcast_to(jnp.max(s, axis=-1, keepdims=True), (nr, LANES))
                else:
                    m_prev = m_scr[slot, rows, :]
                    m_new = jnp.maximum(m_prev, jnp.max(s, axis=-1, keepdims=True))
                    alphas.append(jnp.exp2(m_prev - m_new))
                probs.append(jnp.exp2((s - jnp.concatenate([m_new] * (nk // LANES), axis=1)).astype(BF16)))
                m_scr[slot, rows, :] = m_new
                weights.append(jnp.concatenate(
                    [jnp.where(in_head[hh], v, 0), jnp.broadcast_to(ones[hh], (nk, LANES))], axis=1))
            pv = _dot(jnp.concatenate(probs, axis=1), jnp.concatenate(weights, axis=0))
            if diagonal:
                acc_scr[pr, rows, :] = pv
            else:
                alpha = jnp.where(first, alphas[0], alphas[1])
                acc_scr[pr, rows, :] = jnp.concatenate([alpha, alpha], axis=1) * acc_scr[pr, rows, :] + pv

    block(qi, 0, half, half, True)
    block(qi, half, half, blk, True)

    def off_diagonal(j, carry):
        block(j, 0, blk, blk, False)
        return carry

    lax.fori_loop(0, qi, off_diagonal, 0)
    for pr in range(pairs):
        acc = acc_scr[pr]
        o_ref[0, :, pr * LANES:(pr + 1) * LANES] = (acc[:, :LANES] / acc[:, LANES:]).astype(o_ref.dtype)


def _fox(qkv3, c4, heads):
    b, seq, _ = qkv3.shape
    blk = ROW_BLOCK
    pairs = FOX_PAIRS_PER_STEP
    width = pairs * LANES
    groups = heads * FOX_HEAD_DIM // width
    return pl.pallas_call(
        functools.partial(_fox_kernel, blk=blk, pairs=pairs),
        grid=(b, groups, seq // blk),
        in_specs=[
            pl.BlockSpec((1, blk, width), lambda bi, g, qi: (bi, qi, g)),
            pl.BlockSpec((1, seq, width), lambda bi, g, qi: (bi, 0, groups + g)),
            pl.BlockSpec((1, seq, width), lambda bi, g, qi: (bi, 0, 2 * groups + g)),
            pl.BlockSpec((1,) + c4.shape[1:], lambda bi, g, qi: (bi, 0, 0, 0)),
        ],
        out_specs=pl.BlockSpec((1, blk, width), lambda bi, g, qi: (bi, qi, g)),
        out_shape=jax.ShapeDtypeStruct((b, seq, groups * width), BF16),
        scratch_shapes=[
            pltpu.VMEM((2 * pairs, blk, LANES), F32),
            pltpu.VMEM((pairs, blk, 2 * LANES), F32),
            pltpu.VMEM((2 * pairs, blk, LANES), BF16),
            pltpu.VMEM((blk // 2, blk // 2), F32),
        ],
        compiler_params=_params(("arbitrary", "arbitrary", "arbitrary"), 48),
        name="fox_attention",
    )(qkv3, qkv3, qkv3, c4)


def _s5_param_kernel(log_dt_ref, lr_ref, li_ref, br_ref, bi_ref, ar_ref, ai_ref, bbr_ref, bbi_ref):
    dt = jnp.exp(log_dt_ref[...])
    lr = lr_ref[...]
    li = li_ref[...]
    mag = jnp.exp(lr * dt)
    a_re = mag * jnp.cos(li * dt)
    a_im = mag * jnp.sin(li * dt)
    den = lr * lr + li * li
    n_re = a_re - 1.0
    coef_re = (n_re * lr + a_im * li) / den
    coef_im = (a_im * lr - n_re * li) / den
    br = br_ref[...]
    bi = bi_ref[...]
    ar_ref[...] = a_re
    ai_ref[...] = a_im
    bbr_ref[...] = coef_re * br - coef_im * bi
    bbi_ref[...] = coef_re * bi + coef_im * br


def _s5_params(log_dt, lam_re, lam_im, b_re, b_im):
    groups, state, width = b_re.shape
    rep = lambda t: jnp.repeat(t, width, axis=1)
    flat = (groups, state * width)
    out = jax.ShapeDtypeStruct(flat, F32)
    a_re, a_im, bb_re, bb_im = pl.pallas_call(
        _s5_param_kernel,
        out_shape=[out, out, out, out],
        name="s5_discretise",
    )(rep(jnp.broadcast_to(log_dt[:, None], (groups, state))), rep(lam_re), rep(lam_im),
      b_re.reshape(flat), b_im.reshape(flat))
    unrep = lambda t: t.reshape(groups, state, width)[:, :, 0].reshape(1, groups * state)
    return unrep(a_re), unrep(a_im), bb_re.reshape(b_re.shape), bb_im.reshape(b_re.shape)


def _s5_kernel(u_ref, bbt_re_ref, bbt_im_ref, ct_re_ref, ct_im_ref, are_ref, aim_ref, d_ref, wglu_ref, bglu_ref,
               o_ref, bmat_scr, cre_scr, cim_scr, st_scr, ub_scr, ut_scr, v_scr, yt_scr,
               *, tl, batch, width, half, state, group):
    ci = pl.program_id(0)
    tile = 2 * LANES

    @pl.when(ci == 0)
    def _():
        st_scr[...] = jnp.zeros_like(st_scr)
        bmat_scr[...] = jnp.zeros_like(bmat_scr)
        cre_scr[...] = jnp.zeros_like(cre_scr)
        cim_scr[...] = jnp.zeros_like(cim_scr)
        rb = LANES // state * group
        r = lax.broadcasted_iota(jnp.int32, (rb, LANES), 0)
        l = lax.broadcasted_iota(jnp.int32, (rb, LANES), 1)
        keep_b = (r // group) == (l // state)
        for m in range(half // LANES):
            rows = slice(m * rb, (m + 1) * rb)
            for part, src in enumerate((bbt_re_ref, bbt_im_ref)):
                cols = slice(part * half + m * LANES, part * half + (m + 1) * LANES)
                bmat_scr[rows, cols] = jnp.where(keep_b, src[rows, :], 0.0).astype(BF16)
        rc = LANES // group * state
        r = lax.broadcasted_iota(jnp.int32, (rc, LANES), 0)
        l = lax.broadcasted_iota(jnp.int32, (rc, LANES), 1)
        keep_c = (r // state) == (l // group)
        for n in range(width // LANES):
            rows = slice(n * rc, (n + 1) * rc)
            cols = slice(n * LANES, (n + 1) * LANES)
            cre_scr[rows, cols] = jnp.where(keep_c, ct_re_ref[rows, :], 0.0).astype(BF16)
            cim_scr[rows, cols] = jnp.where(keep_c, ct_im_ref[rows, :], 0.0).astype(BF16)

    planes = width // LANES
    for b in range(batch):
        for p in range(planes):
            ub_scr[p, b * tl:(b + 1) * tl, :] = u_ref[b, :, p * LANES:(p + 1) * LANES]
    for t in range(tl):
        for p in range(planes):
            ut_scr[t * batch:(t + 1) * batch, p * LANES:(p + 1) * LANES] = (
                ub_scr[p, pl.ds(t, batch, stride=tl), :])

    ut = ut_scr[...]
    utb = ut.astype(BF16)
    u_cols_per_tile = tile * width // half
    for part in range(2):
        for jt in range(half // tile):
            u0 = (jt * u_cols_per_tile) // LANES * LANES
            c0 = part * half + jt * tile
            v_scr[:, c0:c0 + tile] = _dot(utb[:, u0:u0 + LANES], bmat_scr[u0:u0 + LANES, c0:c0 + tile])

    cg_w = 4 * LANES
    for cg in range(half // cg_w):
        re_cols = slice(cg * cg_w, (cg + 1) * cg_w)
        im_cols = slice(half + cg * cg_w, half + (cg + 1) * cg_w)
        a_re = jnp.broadcast_to(are_ref[:, re_cols], (batch, cg_w))
        a_im = jnp.broadcast_to(aim_ref[:, re_cols], (batch, cg_w))
        s_re = st_scr[:, re_cols]
        s_im = st_scr[:, im_cols]
        for t in range(tl):
            rows = slice(t * batch, (t + 1) * batch)
            s_re, s_im = (a_re * s_re - a_im * s_im + v_scr[rows, re_cols],
                          a_re * s_im + a_im * s_re + v_scr[rows, im_cols])
            v_scr[rows, re_cols] = s_re
            v_scr[rows, im_cols] = s_im
        st_scr[:, re_cols] = s_re
        st_scr[:, im_cols] = s_im

    kw = half * tile // width
    for nt in range(width // tile):
        k0 = nt * kw
        y = (_dot(v_scr[:, k0:k0 + kw].astype(BF16), cre_scr[k0:k0 + kw, nt * tile:(nt + 1) * tile])
             - _dot(v_scr[:, half + k0:half + k0 + kw].astype(BF16),
                    cim_scr[k0:k0 + kw, nt * tile:(nt + 1) * tile]))
        yt_scr[:, nt * tile:(nt + 1) * tile] = y
    y = yt_scr[...] + d_ref[...] * ut
    y = 0.5 * y * (1.0 + jnp.tanh(math.sqrt(2.0 / math.pi) * (y + 0.044715 * (y * y * y))))
    y = y * _sigmoid(_dot(y.astype(BF16), wglu_ref[...]) + bglu_ref[...])
    for p in range(planes):
        ub_scr[p] = y[:, p * LANES:(p + 1) * LANES]
    for b in range(batch):
        for p in range(planes):
            o_ref[b, :, p * LANES:(p + 1) * LANES] = (
                ub_scr[p, pl.ds(b, tl, stride=batch), :].astype(o_ref.dtype))


def _s5(u3, bbt_re, bbt_im, ct_re, ct_im, a_re, a_im, d_skip, w_glu, b_glu, group):
    batch, seq, width = u3.shape
    half = a_re.shape[1]
    state = half * group // width
    tl = S5_TIME_BLOCK
    rows = tl * batch
    blk = pl.BlockSpec((batch, tl, width), lambda ci: (0, ci, 0))
    return pl.pallas_call(
        functools.partial(_s5_kernel, tl=tl, batch=batch, width=width, half=half, state=state, group=group),
        grid=(seq // tl,),
        in_specs=[blk, _resident(bbt_re.shape), _resident(bbt_im.shape), _resident(ct_re.shape),
                  _resident(ct_im.shape), _resident(a_re.shape), _resident(a_im.shape), _resident((1, width)),
                  _resident(w_glu.shape), _resident((1, width))],
        out_specs=blk,
        out_shape=jax.ShapeDtypeStruct(u3.shape, BF16),
        scratch_shapes=[
            pltpu.VMEM((width, 2 * half), BF16),
            pltpu.VMEM((half, width), BF16),
            pltpu.VMEM((half, width), BF16),
            pltpu.VMEM((batch, 2 * half), F32),
            pltpu.VMEM((width // LANES, rows, LANES), F32),
            pltpu.VMEM((rows, width), F32),
            pltpu.VMEM((rows, 2 * half), F32),
            pltpu.VMEM((rows, width), F32),
        ],
        compiler_params=_params(("arbitrary",), 40),
        name="s5_scan",
    )(u3, bbt_re, bbt_im, ct_re, ct_im, a_re, a_im, d_skip, w_glu, b_glu)


def _out_mlp_kernel(a_ref, b_ref, x_ref, wa_ref, wb_ref, g_ref, wup_ref, wdn_ref, gf_ref, o_ref,
                    acc_scr, h_scr, *, final_norm):
    x1 = x_ref[...] + _dot(a_ref[...], wa_ref[...]) + _dot(b_ref[...], wb_ref[...])
    h_scr[...] = _rms_norm(x1, g_ref[...]).astype(BF16)
    acc_scr[...] = x1

    def ff_chunk(c, carry):
        c0 = pl.multiple_of(c * MLP_FF_BLOCK, MLP_FF_BLOCK)
        t = jnp.maximum(_dot(h_scr[...], wup_ref[0, :, pl.ds(c0, MLP_FF_BLOCK)]), 0.0)
        acc_scr[...] += _dot((t * t).astype(BF16), wdn_ref[0, pl.ds(c0, MLP_FF_BLOCK), :])
        return carry

    lax.fori_loop(0, wup_ref.shape[2] // MLP_FF_BLOCK, ff_chunk, 0, unroll=2)
    acc = acc_scr[...]
    if final_norm:
        acc = _rms_norm(acc, gf_ref[...])
    o_ref[...] = acc


def _out_mlp(a, b, x2, wa, wb, g, w_up, w_down, layer, g_final, final_norm):
    n, d = x2.shape
    tm = MLP_ROW_BLOCK
    row = lambda i: (i, 0)
    slab = lambda shape: pl.BlockSpec((1,) + shape[1:], lambda i: (layer, 0, 0), pipeline_mode=pl.Buffered(1))
    return pl.pallas_call(
        functools.partial(_out_mlp_kernel, final_norm=final_norm),
        grid=(n // tm,),
        in_specs=[
            pl.BlockSpec((tm, a.shape[1]), row),
            pl.BlockSpec((tm, b.shape[1]), row),
            pl.BlockSpec((tm, d), row),
            _resident(wa.shape), _resident(wb.shape), _resident((1, d)),
            slab(w_up.shape), slab(w_down.shape), _resident((1, d)),
        ],
        out_specs=pl.BlockSpec((tm, d), row),
        out_shape=jax.ShapeDtypeStruct((n, d), F32),
        scratch_shapes=[pltpu.VMEM((tm, d), F32), pltpu.VMEM((tm, d), BF16)],
        compiler_params=_params(("arbitrary",), 58),
        name="out_mlp",
    )(a, b, x2, wa, wb, g, w_up, w_down, g_final)


def _rope_table_kernel(inv_ref, cos_ref, sin_ref):
    rows, lanes = cos_ref.shape
    pos = lax.broadcasted_iota(jnp.int32, (rows, lanes), 0).astype(F32)
    lane = lax.broadcasted_iota(jnp.int32, (rows, lanes), 1)
    ang = pos * inv_ref[...]
    cos_ref[...] = jnp.cos(ang)
    sin = jnp.sin(ang)
    sin_ref[...] = jnp.where(lane < lanes // 2, -sin, sin)


def _rope_tables(seq, head_dim):
    inv = 1.0 / (10000.0 ** jnp.linspace(0.0, 1.0, head_dim // 2, dtype=F32))
    inv2 = jnp.concatenate([inv, inv])[None, :]
    out = jax.ShapeDtypeStruct((seq, head_dim), F32)
    return pl.pallas_call(_rope_table_kernel, out_shape=[out, out], name="rope_tables")(inv2)


def _odd_in_kernel(x_ref, g_ref, wconv_ref, wq_ref, wk_ref, wv_ref, wg_ref, cw_ref, cos_ref, sin_ref,
                   conv_ref, q_ref, k_ref, v_ref, gate_ref, z_scr, *, blocks_per_seq, taps, head_dim):
    i = pl.program_id(0)
    tm = x_ref.shape[0]
    cw = conv_ref.shape[1]
    pad = SUBLANES

    @pl.when(i % blocks_per_seq == 0)
    def _():
        z_scr[0:pad, :] = jnp.zeros((pad, cw), F32)

    h = _rms_norm(x_ref[...], g_ref[...]).astype(BF16)
    hc = _dot(h, wconv_ref[:, 0:cw])
    gate_b = _dot(h, wconv_ref[:, cw:2 * cw])
    gate_c = _dot(h, wconv_ref[:, 2 * cw:3 * cw])
    z_scr[pad:pad + tm, :] = gate_c * hc
    conv = cw_ref[taps - 1:taps, :] * z_scr[pad:pad + tm, :]
    for j in range(taps - 1):
        shift = taps - 1 - j
        conv = conv + cw_ref[j:j + 1, :] * z_scr[pad - shift:pad - shift + tm, :]
    conv_ref[...] = (gate_b * conv).astype(conv_ref.dtype)
    z_scr[0:pad, :] = z_scr[tm:tm + pad, :]

    cos = cos_ref[...]
    sin = sin_ref[...]

    def rotate(w_ref, out_ref, scale):
        x = _dot(h, w_ref[...])
        for hd in range(x.shape[1] // head_dim):
            xh = x[:, hd * head_dim:(hd + 1) * head_dim]
            r = xh * cos + pltpu.roll(xh, head_dim // 2, 1) * sin
            out_ref[:, hd * head_dim:(hd + 1) * head_dim] = (r * scale).astype(out_ref.dtype)

    rotate(wq_ref, q_ref, 1.0)
    rotate(wk_ref, k_ref, head_dim ** -0.5)
    v_ref[...] = _dot(h, wv_ref[...]).astype(v_ref.dtype)
    gate_ref[...] = _dot(h, wg_ref[...]).astype(gate_ref.dtype)


def _odd_in(x2, g, wconv, wq, wk, wv, wg, conv_w, cos, sin, seq):
    n, d = x2.shape
    tm = ROW_BLOCK
    nb = seq // tm
    cw = conv_w.shape[1]
    rw = wq.shape[1]
    head_dim = cos.shape[1]
    row = lambda i: (i, 0)
    tab = pl.BlockSpec((tm, head_dim), lambda i: (i % nb, 0))
    o = lambda w: jax.ShapeDtypeStruct((n, w), BF16)
    return pl.pallas_call(
        functools.partial(_odd_in_kernel, blocks_per_seq=nb, taps=conv_w.shape[0], head_dim=head_dim),
        grid=(n // tm,),
        in_specs=[pl.BlockSpec((tm, d), row), _resident((1, d)), _resident(wconv.shape),
                  _resident(wq.shape), _resident(wk.shape), _resident(wv.shape), _resident(wg.shape),
                  _resident(conv_w.shape), tab, tab],
        out_specs=[pl.BlockSpec((tm, cw), row)] + [pl.BlockSpec((tm, rw), row)] * 4,
        out_shape=[o(cw), o(rw), o(rw), o(rw), o(rw)],
        scratch_shapes=[pltpu.VMEM((tm + 2 * SUBLANES, cw), F32)],
        compiler_params=_params(("arbitrary",), 40),
        name="odd_in",
    )(x2, g, wconv, wq, wk, wv, wg, conv_w, cos, sin)


def _retention_kernel(q_ref, k_ref, v_ref, g_ref, o_ref, state_scr, decay_scr, *, chunk, heads):
    ri = pl.program_id(1)
    head_dim = q_ref.shape[2] // heads
    log_gamma = [math.log(1.0 - 2.0 ** (-5.0 - h)) for h in range(heads)]

    @pl.when(ri == 0)
    def _():
        state_scr[...] = jnp.zeros_like(state_scr)

    @pl.when((pl.program_id(0) == 0) & (ri == 0))
    def _():
        ridx = lax.broadcasted_iota(jnp.int32, (chunk, chunk), 0)
        cidx = lax.broadcasted_iota(jnp.int32, (chunk, chunk), 1)
        rel = (ridx - cidx).astype(F32)
        for h in range(heads):
            decay_scr[h] = jnp.where(rel >= 0, jnp.exp(log_gamma[h] * jnp.maximum(rel, 0.0)), 0.0)

    idx = lax.broadcasted_iota(jnp.int32, (chunk, 1), 0).astype(F32)
    for h in range(heads):
        query_decay = jnp.exp(log_gamma[h] * (idx + 1.0))
        key_decay = jnp.exp(log_gamma[h] * (chunk - 1.0 - idx))
        chunk_decay = math.exp(log_gamma[h] * chunk)
        cols = slice(h * head_dim, (h + 1) * head_dim)
        state = state_scr[h]
        for c in range(q_ref.shape[1] // chunk):
            rows = slice(c * chunk, (c + 1) * chunk)
            q = q_ref[0, rows, cols]
            k = k_ref[0, rows, cols]
            v = v_ref[0, rows, cols]
            s = _dot_nt(q, k) * decay_scr[h]
            ret = _dot(s.astype(BF16), v) + _dot(q, state.astype(BF16)) * query_decay
            kd = (k.astype(F32) * key_decay).astype(BF16)
            state = state * chunk_decay + _dot_tn(kd, v)
            mu = jnp.mean(ret, axis=-1, keepdims=True)
            dev = ret - mu
            var = jnp.mean(dev * dev, axis=-1, keepdims=True)
            g = g_ref[0, rows, cols].astype(F32)
            o_ref[0, rows, cols] = (g * _sigmoid(g) * (dev * lax.rsqrt(var + EPS))).astype(o_ref.dtype)
        state_scr[h] = state


def _retention(q3, k3, v3, g3):
    b, seq, rw = q3.shape
    head_dim = rw // RET_HEADS
    tr = RET_ROW_BLOCK
    blk = pl.BlockSpec((1, tr, rw), lambda bi, ri: (bi, ri, 0))
    return pl.pallas_call(
        functools.partial(_retention_kernel, chunk=RET_CHUNK, heads=RET_HEADS),
        grid=(b, seq // tr),
        in_specs=[blk, blk, blk, blk],
        out_specs=blk,
        out_shape=jax.ShapeDtypeStruct(q3.shape, BF16),
        scratch_shapes=[pltpu.VMEM((RET_HEADS, head_dim, head_dim), F32),
                        pltpu.VMEM((RET_HEADS, RET_CHUNK, RET_CHUNK), F32)],
        compiler_params=_params(("arbitrary", "arbitrary"), 32),
        name="retention",
    )(q3, k3, v3, g3)


def _even_layer(x2, batch, seq, norm_g, w_in, b_forget, log_dt, lam_re, lam_im, b_re, b_im, c_re, c_im,
                d_skip, w_glu, b_glu, w_out, mlp_g, w_up, w_down, layer, g_final, final_norm):
    d = x2.shape[1]
    heads = b_forget.shape[0]
    fw = heads * FOX_HEAD_DIM
    sw = d_skip.shape[0]
    wqkv = w_in[:, :3 * fw].astype(BF16)
    wf = jnp.pad(w_in[:, 3 * fw:3 * fw + heads], ((0, 0), (0, LANES - heads))).astype(BF16)
    wu = w_in[:, 3 * fw + heads:].astype(BF16)
    bf = jnp.pad(b_forget.astype(F32), (0, LANES - heads))[None, :]
    qscale = jnp.concatenate([jnp.full((fw,), LOG2_E * FOX_HEAD_DIM ** -0.5, F32), jnp.ones((2 * fw,), F32)])[None, :]
    qkv, u, c4 = _even_in(x2, norm_g[None, :], wqkv, qscale, wu, wf, bf, seq)
    fox = _fox(qkv.reshape(batch, seq, 3 * fw), c4, heads)

    a_re, a_im, bb_re, bb_im = _s5_params(log_dt, lam_re, lam_im, b_re, b_im)
    groups, state, group = b_re.shape
    tile_b = lambda t: jnp.tile(t.transpose(0, 2, 1).reshape(groups * group, state), (1, LANES // state))
    tile_c = lambda t: jnp.tile(t.transpose(0, 2, 1).reshape(groups * state, group), (1, LANES // group))
    s5 = _s5(u.reshape(batch, seq, sw), tile_b(bb_re), tile_b(bb_im), tile_c(c_re), tile_c(c_im), a_re, a_im,
             d_skip[None, :], w_glu.astype(BF16), b_glu[None, :], group)

    w_out = w_out.astype(BF16)
    return _out_mlp(fox.reshape(batch * seq, fw), s5.reshape(batch * seq, sw), x2, w_out[:fw], w_out[fw:],
                    mlp_g[None, :], w_up, w_down, layer, g_final[None, :], final_norm)


def _odd_layer(x2, batch, seq, norm_g, w_in, conv_w, w_out, mlp_g, w_up, w_down, layer, g_final, final_norm):
    d = x2.shape[1]
    cw = conv_w.shape[1]
    rw = d - cw
    head_dim = rw // RET_HEADS
    perm = jnp.concatenate([jnp.arange(0, head_dim, 2), jnp.arange(1, head_dim, 2)])
    perm = (jnp.arange(RET_HEADS)[:, None] * head_dim + perm[None, :]).reshape(-1)
    w_in = w_in.astype(BF16)
    wconv = w_in[:, :3 * cw]
    wq = w_in[:, 3 * cw:3 * cw + rw][:, perm]
    wk = w_in[:, 3 * cw + rw:3 * cw + 2 * rw][:, perm]
    wv = w_in[:, 3 * cw + 2 * rw:3 * cw + 3 * rw]
    wg = w_in[:, 3 * cw + 3 * rw:]
    cos, sin = _rope_tables(seq, head_dim)
    conv, q, k, v, gate = _odd_in(x2, norm_g[None, :], wconv, wq, wk, wv, wg, conv_w.astype(F32), cos, sin, seq)
    to3 = lambda t: t.reshape(batch, seq, rw)
    ret = _retention(to3(q), to3(k), to3(v), to3(gate))
    w_out = w_out.astype(BF16)
    return _out_mlp(conv, ret.reshape(batch * seq, rw), x2, w_out[:cw], w_out[cw:],
                    mlp_g[None, :], w_up, w_down, layer, g_final[None, :], final_norm)


def kernel(x, even_norm_mix, even_w_in, even_b_forget, even_s5_log_dt, even_s5_lambda_re, even_s5_lambda_im, even_s5_b_re, even_s5_b_im, even_s5_c_re, even_s5_c_im, even_s5_d, even_s5_w_glu, even_s5_b_glu, even_w_out, odd_norm_mix, odd_w_in, odd_conv_w, odd_w_out, mlp_norm, mlp_w_up, mlp_w_down, final_norm):
    batch, seq, d = x.shape
    depth = mlp_norm.shape[0]
    x2 = x.reshape(batch * seq, d)
    w_up = mlp_w_up.astype(BF16)
    w_down = mlp_w_down.astype(BF16)
    for layer in range(depth):
        j = layer // 2
        last = layer == depth - 1
        if layer % 2 == 0:
            x2 = _even_layer(x2, batch, seq, even_norm_mix[j], even_w_in[j], even_b_forget[j],
                             even_s5_log_dt[j], even_s5_lambda_re[j], even_s5_lambda_im[j],
                             even_s5_b_re[j], even_s5_b_im[j], even_s5_c_re[j], even_s5_c_im[j],
                             even_s5_d[j], even_s5_w_glu[j], even_s5_b_glu[j], even_w_out[j],
                             mlp_norm[layer], w_up, w_down, layer, final_norm, last)
        else:
            x2 = _odd_layer(x2, batch, seq, odd_norm_mix[j], odd_w_in[j], odd_conv_w[j], odd_w_out[j],
                            mlp_norm[layer], w_up, w_down, layer, final_norm, last)
    return x2.reshape(batch, seq, d)
```

```python
import functools
import math

import jax
import jax.numpy as jnp
from jax import lax
from jax.experimental import pallas as pl
from jax.experimental.pallas import tpu as pltpu

F32 = jnp.float32
BF16 = jnp.bfloat16
EPS = 1e-6

LANES = 128
SUBLANES = 8
FOX_HEAD_DIM = 64
RET_HEADS = 4
RET_CHUNK = 256
ROW_BLOCK = 512
IN_ROW_BLOCK = 1024
S5_TIME_BLOCK = 64
RET_ROW_BLOCK = 512
MLP_FF_BLOCK = 1024
MLP_ROW_BLOCK = 1024
MIB = 1024 * 1024
LOG2_E = math.log2(math.e)
FOX_PAIRS_PER_STEP = 4


def _params(semantics, vmem_mib):
    return pltpu.CompilerParams(dimension_semantics=semantics, vmem_limit_bytes=vmem_mib * MIB)


def _resident(shape):
    return pl.BlockSpec(shape, lambda *_: (0,) * len(shape), pipeline_mode=pl.Buffered(1))


def _rms_norm(x, g):
    return x * lax.rsqrt(jnp.mean(x * x, axis=-1, keepdims=True) + EPS) * g


def _sigmoid(x):
    return 1.0 / (1.0 + jnp.exp(-x))


def _dot(a, b):
    return jnp.dot(a, b, preferred_element_type=F32)


def _dot_nt(a, b):
    return lax.dot_general(a, b, (((1,), (1,)), ((), ())), preferred_element_type=F32)


def _dot_tn(a, b):
    return lax.dot_general(a, b, (((0,), (0,)), ((), ())), preferred_element_type=F32)


def _even_in_kernel(x_ref, g_ref, wqkv_ref, qscale_ref, wu_ref, wf_ref, bf_ref,
                    qkv_ref, u_ref, c_ref, carry_ref, *, blocks_per_seq):
    i = pl.program_id(0)

    @pl.when(i % blocks_per_seq == 0)
    def _():
        carry_ref[...] = jnp.zeros_like(carry_ref)

    h = _rms_norm(x_ref[...], g_ref[...]).astype(BF16)
    qkv_ref[...] = (_dot(h, wqkv_ref[...]) * qscale_ref[...]).astype(BF16)
    u_ref[...] = _dot(h, wu_ref[...])
    fl = _dot(h, wf_ref[...]) + bf_ref[...]
    logf = jnp.minimum(fl, 0.0) - jnp.log(1.0 + jnp.exp(-jnp.abs(fl)))
    x = logf.T[:SUBLANES, :]
    tm = x.shape[1]
    lane = lax.broadcasted_iota(jnp.int32, x.shape, 1)
    shift = 1
    while shift < tm:
        x = x + jnp.where(lane >= shift, pltpu.roll(x, shift, 1), 0.0)
        shift *= 2
    cs = x + jnp.concatenate([carry_ref[...]] * (tm // LANES), axis=1)
    carry_ref[...] = jnp.broadcast_to(cs[:, tm - 1:], carry_ref.shape)
    cb = c_ref.shape[3]
    for j in range(tm // cb):
        c_ref[0, j] = cs[:, j * cb:(j + 1) * cb]


def _even_in(x2, g, wqkv, qscale, wu, wf, bf, seq):
    n, d = x2.shape
    tm = IN_ROW_BLOCK
    nb = seq // tm
    cb = ROW_BLOCK
    row = lambda i: (i, 0)
    return pl.pallas_call(
        functools.partial(_even_in_kernel, blocks_per_seq=nb),
        grid=(n // tm,),
        in_specs=[
            pl.BlockSpec((tm, d), row),
            _resident((1, d)),
            _resident(wqkv.shape),
            _resident((1, wqkv.shape[1])),
            _resident(wu.shape),
            _resident(wf.shape),
            _resident((1, LANES)),
        ],
        out_specs=[
            pl.BlockSpec((tm, wqkv.shape[1]), row),
            pl.BlockSpec((tm, wu.shape[1]), row),
            pl.BlockSpec((1, tm // cb, SUBLANES, cb), lambda i: (i // nb, i % nb, 0, 0)),
        ],
        out_shape=[
            jax.ShapeDtypeStruct((n, wqkv.shape[1]), BF16),
            jax.ShapeDtypeStruct((n, wu.shape[1]), F32),
            jax.ShapeDtypeStruct((n // seq, seq // cb, SUBLANES, cb), F32),
        ],
        scratch_shapes=[pltpu.VMEM((SUBLANES, LANES), F32)],
        compiler_params=_params(("arbitrary",), 40),
        name="even_in",
    )(x2, g, wqkv, qscale, wu, wf, bf)


def _fox_kernel(q_ref, k_ref, v_ref, c_ref, o_ref, m_scr, acc_scr, qm_scr, mask_scr, *, blk, pairs):
    grp = pl.program_id(1)
    qi = pl.program_id(2)
    lane = lax.broadcasted_iota(jnp.int32, (1, LANES), 1)
    in_head = [(lane >= FOX_HEAD_DIM * hh) & (lane < FOX_HEAD_DIM * (hh + 1)) for hh in range(2)]
    for pr in range(pairs):
        for hh in range(2):
            qm_scr[2 * pr + hh] = jnp.where(in_head[hh], q_ref[0, :, pr * LANES:(pr + 1) * LANES], 0)

    first = lane < FOX_HEAD_DIM
    ones = [jnp.where(in_head[hh], 1.0, 0.0).astype(BF16) for hh in range(2)]
    half = blk // 2

    @pl.when((pl.program_id(0) == 0) & (grp == 0) & (qi == 0))
    def _():
        row = lax.broadcasted_iota(jnp.int32, (half, half), 0)
        col = lax.broadcasted_iota(jnp.int32, (half, half), 1)
        mask_scr[...] = jnp.where(row >= col, 0.0, -jnp.inf)

    def block(j, r0, nr, nk, diagonal):
        k0 = pl.multiple_of(j * blk, blk)
        rows = slice(r0, r0 + nr)
        for pr in range(pairs):
            k = k_ref[0, pl.ds(k0, nk), pr * LANES:(pr + 1) * LANES]
            v = v_ref[0, pl.ds(k0, nk), pr * LANES:(pr + 1) * LANES]
            probs, alphas, weights = [], [], []
            for hh in range(2):
                slot = 2 * pr + hh
                h = 2 * (grp * pairs + pr) + hh
                c_q = c_ref[0, qi, pl.ds(h, 1), :][:, 0:1]
                c_k = c_ref[0, j, pl.ds(h, 1), pl.ds(0, nk)]
                s = _dot_nt(qm_scr[slot, rows, :], k) + (c_q - c_k) * LOG2_E
                if diagonal:
                    tail = s[:, nk - half:] + mask_scr[...]
                    s = tail if nk == half else jnp.concatenate([s[:, :nk - half], tail], axis=1)
                    m_new = jnp.broadcast_to(jnp.max(s, axis=-1, keepdims=True), (nr, LANES))
                else:
                    m_prev = m_scr[slot, rows, :]
                    m_new = jnp.maximum(m_prev, jnp.max(s, axis=-1, keepdims=True))
                    alphas.append(jnp.exp2(m_prev - m_new))
                probs.append(jnp.exp2((s - jnp.concatenate([m_new] * (nk // LANES), axis=1)).astype(BF16)))
                m_scr[slot, rows, :] = m_new
                weights.append(jnp.concatenate(
                    [jnp.where(in_head[hh], v, 0), jnp.broadcast_to(ones[hh], (nk, LANES))], axis=1))
            pv = _dot(jnp.concatenate(probs, axis=1), jnp.concatenate(weights, axis=0))
            if diagonal:
                acc_scr[pr, rows, :] = pv
            else:
                alpha = jnp.where(first, alphas[0], alphas[1])
                acc_scr[pr, rows, :] = jnp.concatenate([alpha, alpha], axis=1) * acc_scr[pr, rows, :] + pv

    block(qi, 0, half, half, True)
    block(qi, half, half, blk, True)

    def off_diagonal_pair(jj, carry):
        block(2 * jj, 0, blk, blk, False)
        block(2 * jj + 1, 0, blk, blk, False)
        return carry

    lax.fori_loop(0, qi // 2, off_diagonal_pair, 0)

    @pl.when(qi % 2 == 1)
    def _():
        block(qi - 1, 0, blk, blk, False)

    for pr in range(pairs):
        acc = acc_scr[pr]
        o_ref[0, :, pr * LANES:(pr + 1) * LANES] = (acc[:, :LANES] / acc[:, LANES:]).astype(o_ref.dtype)


def _fox(qkv3, c4, heads):
    b, seq, _ = qkv3.shape
    blk = ROW_BLOCK
    pairs = FOX_PAIRS_PER_STEP
    width = pairs * LANES
    groups = heads * FOX_HEAD_DIM // width
    return pl.pallas_call(
        functools.partial(_fox_kernel, blk=blk, pairs=pairs),
        grid=(b, groups, seq // blk),
        in_specs=[
            pl.BlockSpec((1, blk, width), lambda bi, g, qi: (bi, qi, g)),
            pl.BlockSpec((1, seq, width), lambda bi, g, qi: (bi, 0, groups + g)),
            pl.BlockSpec((1, seq, width), lambda bi, g, qi: (bi, 0, 2 * groups + g)),
            pl.BlockSpec((1,) + c4.shape[1:], lambda bi, g, qi: (bi, 0, 0, 0)),
        ],
        out_specs=pl.BlockSpec((1, blk, width), lambda bi, g, qi: (bi, qi, g)),
        out_shape=jax.ShapeDtypeStruct((b, seq, groups * width), BF16),
        scratch_shapes=[
            pltpu.VMEM((2 * pairs, blk, LANES), F32),
            pltpu.VMEM((pairs, blk, 2 * LANES), F32),
            pltpu.VMEM((2 * pairs, blk, LANES), BF16),
            pltpu.VMEM((blk // 2, blk // 2), F32),
        ],
        compiler_params=_params(("arbitrary", "arbitrary", "arbitrary"), 48),
        name="fox_attention",
    )(qkv3, qkv3, qkv3, c4)


def _s5_param_kernel(log_dt_ref, lr_ref, li_ref, br_ref, bi_ref, ar_ref, ai_ref, bbr_ref, bbi_ref):
    dt = jnp.exp(log_dt_ref[...])
    lr = lr_ref[...]
    li = li_ref[...]
    mag = jnp.exp(lr * dt)
    a_re = mag * jnp.cos(li * dt)
    a_im = mag * jnp.sin(li * dt)
    den = lr * lr + li * li
    n_re = a_re - 1.0
    coef_re = (n_re * lr + a_im * li) / den
    coef_im = (a_im * lr - n_re * li) / den
    br = br_ref[...]
    bi = bi_ref[...]
    ar_ref[...] = a_re
    ai_ref[...] = a_im
    bbr_ref[...] = coef_re * br - coef_im * bi
    bbi_ref[...] = coef_re * bi + coef_im * br


def _s5_params(log_dt, lam_re, lam_im, b_re, b_im):
    groups, state, width = b_re.shape
    rep = lambda t: jnp.repeat(t, width, axis=1)
    flat = (groups, state * width)
    out = jax.ShapeDtypeStruct(flat, F32)
    a_re, a_im, bb_re, bb_im = pl.pallas_call(
        _s5_param_kernel,
        out_shape=[out, out, out, out],
        name="s5_discretise",
    )(rep(jnp.broadcast_to(log_dt[:, None], (groups, state))), rep(lam_re), rep(lam_im),
      b_re.reshape(flat), b_im.reshape(flat))
    unrep = lambda t: t.reshape(groups, state, width)[:, :, 0].reshape(1, groups * state)
    return unrep(a_re), unrep(a_im), bb_re.reshape(b_re.shape), bb_im.reshape(b_re.shape)


def _s5_kernel(u_ref, bbt_re_ref, bbt_im_ref, ct_re_ref, ct_im_ref, are_ref, aim_ref, d_ref, wglu_ref, bglu_ref,
               o_ref, bmat_scr, cre_scr, cim_scr, st_scr, ub_scr, ut_scr, v_scr, yt_scr,
               *, tl, batch, width, half, state, group):
    ci = pl.program_id(0)
    tile = 2 * LANES

    @pl.when(ci == 0)
    def _():
        st_scr[...] = jnp.zeros_like(st_scr)
        bmat_scr[...] = jnp.zeros_like(bmat_scr)
        cre_scr[...] = jnp.zeros_like(cre_scr)
        cim_scr[...] = jnp.zeros_like(cim_scr)
        rb = LANES // state * group
        r = lax.broadcasted_iota(jnp.int32, (rb, LANES), 0)
        l = lax.broadcasted_iota(jnp.int32, (rb, LANES), 1)
        keep_b = (r // group) == (l // state)
        for m in range(half // LANES):
            rows = slice(m * rb, (m + 1) * rb)
            for part, src in enumerate((bbt_re_ref, bbt_im_ref)):
                cols = slice(part * half + m * LANES, part * half + (m + 1) * LANES)
                bmat_scr[rows, cols] = jnp.where(keep_b, src[rows, :], 0.0).astype(BF16)
        rc = LANES // group * state
        r = lax.broadcasted_iota(jnp.int32, (rc, LANES), 0)
        l = lax.broadcasted_iota(jnp.int32, (rc, LANES), 1)
        keep_c = (r // state) == (l // group)
        for n in range(width // LANES):
            rows = slice(n * rc, (n + 1) * rc)
            cols = slice(n * LANES, (n + 1) * LANES)
            cre_scr[rows, cols] = jnp.where(keep_c, ct_re_ref[rows, :], 0.0).astype(BF16)
            cim_scr[rows, cols] = jnp.where(keep_c, ct_im_ref[rows, :], 0.0).astype(BF16)

    planes = width // LANES
    for b in range(batch):
        for p in range(planes):
            ub_scr[p, b * tl:(b + 1) * tl, :] = u_ref[b, :, p * LANES:(p + 1) * LANES]
    for t in range(tl):
        for p in range(planes):
            ut_scr[t * batch:(t + 1) * batch, p * LANES:(p + 1) * LANES] = (
                ub_scr[p, pl.ds(t, batch, stride=tl), :])

    ut = ut_scr[...]
    utb = ut.astype(BF16)
    u_cols_per_tile = tile * width // half
    for part in range(2):
        for jt in range(half // tile):
            u0 = (jt * u_cols_per_tile) // LANES * LANES
            c0 = part * half + jt * tile
            v_scr[:, c0:c0 + tile] = _dot(utb[:, u0:u0 + LANES], bmat_scr[u0:u0 + LANES, c0:c0 + tile])

    cg_w = 4 * LANES
    for cg in range(half // cg_w):
        re_cols = slice(cg * cg_w, (cg + 1) * cg_w)
        im_cols = slice(half + cg * cg_w, half + (cg + 1) * cg_w)
        a_re = jnp.broadcast_to(are_ref[:, re_cols], (batch, cg_w))
        a_im = jnp.broadcast_to(aim_ref[:, re_cols], (batch, cg_w))
        s_re = st_scr[:, re_cols]
        s_im = st_scr[:, im_cols]
        for t in range(tl):
            rows = slice(t * batch, (t + 1) * batch)
            s_re, s_im = (a_re * s_re - a_im * s_im + v_scr[rows, re_cols],
                          a_re * s_im + a_im * s_re + v_scr[rows, im_cols])
            v_scr[rows, re_cols] = s_re
            v_scr[rows, im_cols] = s_im
        st_scr[:, re_cols] = s_re
        st_scr[:, im_cols] = s_im

    kw = half * tile // width
    for nt in range(width // tile):
        k0 = nt * kw
        y = (_dot(v_scr[:, k0:k0 + kw].astype(BF16), cre_scr[k0:k0 + kw, nt * tile:(nt + 1) * tile])
             - _dot(v_scr[:, half + k0:half + k0 + kw].astype(BF16),
                    cim_scr[k0:k0 + kw, nt * tile:(nt + 1) * tile]))
        yt_scr[:, nt * tile:(nt + 1) * tile] = y
    y = yt_scr[...] + d_ref[...] * ut
    y = 0.5 * y * (1.0 + jnp.tanh(math.sqrt(2.0 / math.pi) * (y + 0.044715 * (y * y * y))))
    y = y * _sigmoid(_dot(y.astype(BF16), wglu_ref[...]) + bglu_ref[...])
    for p in range(planes):
        ub_scr[p] = y[:, p * LANES:(p + 1) * LANES]
    for b in range(batch):
        for p in range(planes):
            o_ref[b, :, p * LANES:(p + 1) * LANES] = (
                ub_scr[p, pl.ds(b, tl, stride=batch), :].astype(o_ref.dtype))


def _s5(u3, bbt_re, bbt_im, ct_re, ct_im, a_re, a_im, d_skip, w_glu, b_glu, group):
    batch, seq, width = u3.shape
    half = a_re.shape[1]
    state = half * group // width
    tl = S5_TIME_BLOCK
    rows = tl * batch
    blk = pl.BlockSpec((batch, tl, width), lambda ci: (0, ci, 0))
    return pl.pallas_call(
        functools.partial(_s5_kernel, tl=tl, batch=batch, width=width, half=half, state=state, group=group),
        grid=(seq // tl,),
        in_specs=[blk, _resident(bbt_re.shape), _resident(bbt_im.shape), _resident(ct_re.shape),
                  _resident(ct_im.shape), _resident(a_re.shape), _resident(a_im.shape), _resident((1, width)),
                  _resident(w_glu.shape), _resident((1, width))],
        out_specs=blk,
        out_shape=jax.ShapeDtypeStruct(u3.shape, BF16),
        scratch_shapes=[
            pltpu.VMEM((width, 2 * half), BF16),
            pltpu.VMEM((half, width), BF16),
            pltpu.VMEM((half, width), BF16),
            pltpu.VMEM((batch, 2 * half), F32),
            pltpu.VMEM((width // LANES, rows, LANES), F32),
            pltpu.VMEM((rows, width), F32),
            pltpu.VMEM((rows, 2 * half), F32),
            pltpu.VMEM((rows, width), F32),
        ],
        compiler_params=_params(("arbitrary",), 40),
        name="s5_scan",
    )(u3, bbt_re, bbt_im, ct_re, ct_im, a_re, a_im, d_skip, w_glu, b_glu)


def _out_mlp_kernel(a_ref, b_ref, x_ref, wa_ref, wb_ref, g_ref, wup_ref, wdn_ref, gf_ref, o_ref,
                    acc_scr, h_scr, *, final_norm):
    x1 = x_ref[...] + _dot(a_ref[...], wa_ref[...]) + _dot(b_ref[...], wb_ref[...])
    h_scr[...] = _rms_norm(x1, g_ref[...]).astype(BF16)
    acc_scr[...] = x1

    def ff_chunk(c, carry):
        c0 = pl.multiple_of(c * MLP_FF_BLOCK, MLP_FF_BLOCK)
        t = jnp.maximum(_dot(h_scr[...], wup_ref[0, :, pl.ds(c0, MLP_FF_BLOCK)]), 0.0)
        acc_scr[...] += _dot((t * t).astype(BF16), wdn_ref[0, pl.ds(c0, MLP_FF_BLOCK), :])
        return carry

    lax.fori_loop(0, wup_ref.shape[2] // MLP_FF_BLOCK, ff_chunk, 0, unroll=2)
    acc = acc_scr[...]
    if final_norm:
        acc = _rms_norm(acc, gf_ref[...])
    o_ref[...] = acc


def _out_mlp(a, b, x2, wa, wb, g, w_up, w_down, layer, g_final, final_norm):
    n, d = x2.shape
    tm = MLP_ROW_BLOCK
    row = lambda i: (i, 0)
    slab = lambda shape: pl.BlockSpec((1,) + shape[1:], lambda i: (layer, 0, 0), pipeline_mode=pl.Buffered(1))
    return pl.pallas_call(
        functools.partial(_out_mlp_kernel, final_norm=final_norm),
        grid=(n // tm,),
        in_specs=[
            pl.BlockSpec((tm, a.shape[1]), row),
            pl.BlockSpec((tm, b.shape[1]), row),
            pl.BlockSpec((tm, d), row),
            _resident(wa.shape), _resident(wb.shape), _resident((1, d)),
            slab(w_up.shape), slab(w_down.shape), _resident((1, d)),
        ],
        out_specs=pl.BlockSpec((tm, d), row),
        out_shape=jax.ShapeDtypeStruct((n, d), F32),
        scratch_shapes=[pltpu.VMEM((tm, d), F32), pltpu.VMEM((tm, d), BF16)],
        compiler_params=_params(("arbitrary",), 58),
        name="out_mlp",
    )(a, b, x2, wa, wb, g, w_up, w_down, g_final)


def _rope_table_kernel(inv_ref, cos_ref, sin_ref):
    rows, lanes = cos_ref.shape
    pos = lax.broadcasted_iota(jnp.int32, (rows, lanes), 0).astype(F32)
    lane = lax.broadcasted_iota(jnp.int32, (rows, lanes), 1)
    ang = pos * inv_ref[...]
    cos_ref[...] = jnp.cos(ang)
    sin = jnp.sin(ang)
    sin_ref[...] = jnp.where(lane < lanes // 2, -sin, sin)


def _rope_tables(seq, head_dim):
    inv = 1.0 / (10000.0 ** jnp.linspace(0.0, 1.0, head_dim // 2, dtype=F32))
    inv2 = jnp.concatenate([inv, inv])[None, :]
    out = jax.ShapeDtypeStruct((seq, head_dim), F32)
    return pl.pallas_call(_rope_table_kernel, out_shape=[out, out], name="rope_tables")(inv2)


def _odd_in_kernel(x_ref, g_ref, wconv_ref, wq_ref, wk_ref, wv_ref, wg_ref, cw_ref, cos_ref, sin_ref,
                   conv_ref, q_ref, k_ref, v_ref, gate_ref, z_scr, *, blocks_per_seq, taps, head_dim):
    i = pl.program_id(0)
    tm = x_ref.shape[0]
    cw = conv_ref.shape[1]
    pad = SUBLANES

    @pl.when(i % blocks_per_seq == 0)
    def _():
        z_scr[0:pad, :] = jnp.zeros((pad, cw), F32)

    h = _rms_norm(x_ref[...], g_ref[...]).astype(BF16)
    hc = _dot(h, wconv_ref[:, 0:cw])
    gate_b = _dot(h, wconv_ref[:, cw:2 * cw])
    gate_c = _dot(h, wconv_ref[:, 2 * cw:3 * cw])
    z_scr[pad:pad + tm, :] = gate_c * hc
    conv = cw_ref[taps - 1:taps, :] * z_scr[pad:pad + tm, :]
    for j in range(taps - 1):
        shift = taps - 1 - j
        conv = conv + cw_ref[j:j + 1, :] * z_scr[pad - shift:pad - shift + tm, :]
    conv_ref[...] = (gate_b * conv).astype(conv_ref.dtype)
    z_scr[0:pad, :] = z_scr[tm:tm + pad, :]

    cos = cos_ref[...]
    sin = sin_ref[...]

    def rotate(w_ref, out_ref, scale):
        x = _dot(h, w_ref[...])
        for hd in range(x.shape[1] // head_dim):
            xh = x[:, hd * head_dim:(hd + 1) * head_dim]
            r = xh * cos + pltpu.roll(xh, head_dim // 2, 1) * sin
            out_ref[:, hd * head_dim:(hd + 1) * head_dim] = (r * scale).astype(out_ref.dtype)

    rotate(wq_ref, q_ref, 1.0)
    rotate(wk_ref, k_ref, head_dim ** -0.5)
    v_ref[...] = _dot(h, wv_ref[...]).astype(v_ref.dtype)
    gate_ref[...] = _dot(h, wg_ref[...]).astype(gate_ref.dtype)


def _odd_in(x2, g, wconv, wq, wk, wv, wg, conv_w, cos, sin, seq):
    n, d = x2.shape
    tm = IN_ROW_BLOCK
    nb = seq // tm
    cw = conv_w.shape[1]
    rw = wq.shape[1]
    head_dim = cos.shape[1]
    row = lambda i: (i, 0)
    tab = pl.BlockSpec((tm, head_dim), lambda i: (i % nb, 0))
    o = lambda w: jax.ShapeDtypeStruct((n, w), BF16)
    return pl.pallas_call(
        functools.partial(_odd_in_kernel, blocks_per_seq=nb, taps=conv_w.shape[0], head_dim=head_dim),
        grid=(n // tm,),
        in_specs=[pl.BlockSpec((tm, d), row), _resident((1, d)), _resident(wconv.shape),
                  _resident(wq.shape), _resident(wk.shape), _resident(wv.shape), _resident(wg.shape),
                  _resident(conv_w.shape), tab, tab],
        out_specs=[pl.BlockSpec((tm, cw), row)] + [pl.BlockSpec((tm, rw), row)] * 4,
        out_shape=[o(cw), o(rw), o(rw), o(rw), o(rw)],
        scratch_shapes=[pltpu.VMEM((tm + 2 * SUBLANES, cw), F32)],
        compiler_params=_params(("arbitrary",), 40),
        name="odd_in",
    )(x2, g, wconv, wq, wk, wv, wg, conv_w, cos, sin)


def _retention_kernel(q_ref, k_ref, v_ref, g_ref, o_ref, state_scr, decay_scr, *, chunk, heads):
    ri = pl.program_id(1)
    head_dim = q_ref.shape[2] // heads
    log_gamma = [math.log(1.0 - 2.0 ** (-5.0 - h)) for h in range(heads)]

    @pl.when(ri == 0)
    def _():
        state_scr[...] = jnp.zeros_like(state_scr)

    @pl.when((pl.program_id(0) == 0) & (ri == 0))
    def _():
        ridx = lax.broadcasted_iota(jnp.int32, (chunk, chunk), 0)
        cidx = lax.broadcasted_iota(jnp.int32, (chunk, chunk), 1)
        rel = (ridx - cidx).astype(F32)
        for h in range(heads):
            decay_scr[h] = jnp.where(rel >= 0, jnp.exp(log_gamma[h] * jnp.maximum(rel, 0.0)), 0.0)

    idx = lax.broadcasted_iota(jnp.int32, (chunk, 1), 0).astype(F32)
    for h in range(heads):
        query_decay = jnp.exp(log_gamma[h] * (idx + 1.0))
        key_decay = jnp.exp(log_gamma[h] * (chunk - 1.0 - idx))
        chunk_decay = math.exp(log_gamma[h] * chunk)
        cols = slice(h * head_dim, (h + 1) * head_dim)
        state = state_scr[h]
        for c in range(q_ref.shape[1] // chunk):
            rows = slice(c * chunk, (c + 1) * chunk)
            q = q_ref[0, rows, cols]
            k = k_ref[0, rows, cols]
            v = v_ref[0, rows, cols]
            s = _dot_nt(q, k) * decay_scr[h]
            ret = _dot(s.astype(BF16), v) + _dot(q, state.astype(BF16)) * query_decay
            kd = (k.astype(F32) * key_decay).astype(BF16)
            state = state * chunk_decay + _dot_tn(kd, v)
            mu = jnp.mean(ret, axis=-1, keepdims=True)
            dev = ret - mu
            var = jnp.mean(dev * dev, axis=-1, keepdims=True)
            g = g_ref[0, rows, cols].astype(F32)
            o_ref[0, rows, cols] = (g * _sigmoid(g) * (dev * lax.rsqrt(var + EPS))).astype(o_ref.dtype)
        state_scr[h] = state


def _retention(q3, k3, v3, g3):
    b, seq, rw = q3.shape
    head_dim = rw // RET_HEADS
    tr = RET_ROW_BLOCK
    blk = pl.BlockSpec((1, tr, rw), lambda bi, ri: (bi, ri, 0))
    return pl.pallas_call(
        functools.partial(_retention_kernel, chunk=RET_CHUNK, heads=RET_HEADS),
        grid=(b, seq // tr),
        in_specs=[blk, blk, blk, blk],
        out_specs=blk,
        out_shape=jax.ShapeDtypeStruct(q3.shape, BF16),
        scratch_shapes=[pltpu.VMEM((RET_HEADS, head_dim, head_dim), F32),
                        pltpu.VMEM((RET_HEADS, RET_CHUNK, RET_CHUNK), F32)],
        compiler_params=_params(("arbitrary", "arbitrary"), 32),
        name="retention",
    )(q3, k3, v3, g3)


def _even_layer(x2, batch, seq, norm_g, w_in, b_forget, log_dt, lam_re, lam_im, b_re, b_im, c_re, c_im,
                d_skip, w_glu, b_glu, w_out, mlp_g, w_up, w_down, layer, g_final, final_norm):
    d = x2.shape[1]
    heads = b_forget.shape[0]
    fw = heads * FOX_HEAD_DIM
    sw = d_skip.shape[0]
    wqkv = w_in[:, :3 * fw].astype(BF16)
    wf = jnp.pad(w_in[:, 3 * fw:3 * fw + heads], ((0, 0), (0, LANES - heads))).astype(BF16)
    wu = w_in[:, 3 * fw + heads:].astype(BF16)
    bf = jnp.pad(b_forget.astype(F32), (0, LANES - heads))[None, :]
    qscale = jnp.concatenate([jnp.full((fw,), LOG2_E * FOX_HEAD_DIM ** -0.5, F32), jnp.ones((2 * fw,), F32)])[None, :]
    qkv, u, c4 = _even_in(x2, norm_g[None, :], wqkv, qscale, wu, wf, bf, seq)
    fox = _fox(qkv.reshape(batch, seq, 3 * fw), c4, heads)

    a_re, a_im, bb_re, bb_im = _s5_params(log_dt, lam_re, lam_im, b_re, b_im)
    groups, state, group = b_re.shape
    tile_b = lambda t: jnp.tile(t.transpose(0, 2, 1).reshape(groups * group, state), (1, LANES // state))
    tile_c = lambda t: jnp.tile(t.transpose(0, 2, 1).reshape(groups * state, group), (1, LANES // group))
    s5 = _s5(u.reshape(batch, seq, sw), tile_b(bb_re), tile_b(bb_im), tile_c(c_re), tile_c(c_im), a_re, a_im,
             d_skip[None, :], w_glu.astype(BF16), b_glu[None, :], group)

    w_out = w_out.astype(BF16)
    return _out_mlp(fox.reshape(batch * seq, fw), s5.reshape(batch * seq, sw), x2, w_out[:fw], w_out[fw:],
                    mlp_g[None, :], w_up, w_down, layer, g_final[None, :], final_norm)


def _odd_layer(x2, batch, seq, norm_g, w_in, conv_w, w_out, mlp_g, w_up, w_down, layer, g_final, final_norm):
    d = x2.shape[1]
    cw = conv_w.shape[1]
    rw = d - cw
    head_dim = rw // RET_HEADS
    perm = jnp.concatenate([jnp.arange(0, head_dim, 2), jnp.arange(1, head_dim, 2)])
    perm = (jnp.arange(RET_HEADS)[:, None] * head_dim + perm[None, :]).reshape(-1)
    w_in = w_in.astype(BF16)
    wconv = w_in[:, :3 * cw]
    wq = w_in[:, 3 * cw:3 * cw + rw][:, perm]
    wk = w_in[:, 3 * cw + rw:3 * cw + 2 * rw][:, perm]
    wv = w_in[:, 3 * cw + 2 * rw:3 * cw + 3 * rw]
    wg = w_in[:, 3 * cw + 3 * rw:]
    cos, sin = _rope_tables(seq, head_dim)
    conv, q, k, v, gate = _odd_in(x2, norm_g[None, :], wconv, wq, wk, wv, wg, conv_w.astype(F32), cos, sin, seq)
    to3 = lambda t: t.reshape(batch, seq, rw)
    ret = _retention(to3(q), to3(k), to3(v), to3(gate))
    w_out = w_out.astype(BF16)
    return _out_mlp(conv, ret.reshape(batch * seq, rw), x2, w_out[:cw], w_out[cw:],
                    mlp_g[None, :], w_up, w_down, layer, g_final[None, :], final_norm)


def kernel(x, even_norm_mix, even_w_in, even_b_forget, even_s5_log_dt, even_s5_lambda_re, even_s5_lambda_im, even_s5_b_re, even_s5_b_im, even_s5_c_re, even_s5_c_im, even_s5_d, even_s5_w_glu, even_s5_b_glu, even_w_out, odd_norm_mix, odd_w_in, odd_conv_w, odd_w_out, mlp_norm, mlp_w_up, mlp_w_down, final_norm):
    batch, seq, d = x.shape
    depth = mlp_norm.shape[0]
    x2 = x.reshape(batch * seq, d)
    w_up = mlp_w_up.astype(BF16)
    w_down = mlp_w_down.astype(BF16)
    for layer in range(depth):
        j = layer // 2
        last = layer == depth - 1
        if layer % 2 == 0:
            x2 = _even_layer(x2, batch, seq, even_norm_mix[j], even_w_in[j], even_b_forget[j],
                             even_s5_log_dt[j], even_s5_lambda_re[j], even_s5_lambda_im[j],
                             even_s5_b_re[j], even_s5_b_im[j], even_s5_c_re[j], even_s5_c_im[j],
                             even_s5_d[j], even_s5_w_glu[j], even_s5_b_glu[j], even_w_out[j],
                             mlp_norm[layer], w_up, w_down, layer, final_norm, last)
        else:
            x2 = _odd_layer(x2, batch, seq, odd_norm_mix[j], odd_w_in[j], odd_conv_w[j], odd_w_out[j],
                            mlp_norm[layer], w_up, w_down, layer, final_norm, last)
    return x2.reshape(batch, seq, d)
```

```python
import functools
import math

import jax
import jax.numpy as jnp
from jax import lax
from jax.experimental import pallas as pl
from jax.experimental.pallas import tpu as pltpu

F32 = jnp.float32
BF16 = jnp.bfloat16
EPS = 1e-6

LANES = 128
SUBLANES = 8
FOX_HEAD_DIM = 64
RET_HEADS = 4
RET_CHUNK = 256
ROW_BLOCK = 512
IN_ROW_BLOCK = 1024
S5_TIME_BLOCK = 64
RET_ROW_BLOCK = 1024
MLP_FF_BLOCK = 1024
MLP_ROW_BLOCK = 1024
MIB = 1024 * 1024
LOG2_E = math.log2(math.e)
FOX_PAIRS_PER_STEP = 4


def _params(semantics, vmem_mib):
    return pltpu.CompilerParams(dimension_semantics=semantics, vmem_limit_bytes=vmem_mib * MIB)


def _resident(shape):
    return pl.BlockSpec(shape, lambda *_: (0,) * len(shape), pipeline_mode=pl.Buffered(1))


def _rms_norm(x, g):
    return x * lax.rsqrt(jnp.mean(x * x, axis=-1, keepdims=True) + EPS) * g


def _sigmoid(x):
    return 1.0 / (1.0 + jnp.exp(-x))


def _dot(a, b):
    return jnp.dot(a, b, preferred_element_type=F32)


def _dot_nt(a, b):
    return lax.dot_general(a, b, (((1,), (1,)), ((), ())), preferred_element_type=F32)


def _dot_tn(a, b):
    return lax.dot_general(a, b, (((0,), (0,)), ((), ())), preferred_element_type=F32)


def _even_in_kernel(x_ref, g_ref, wqkv_ref, qscale_ref, wu_ref, wf_ref, bf_ref,
                    qkv_ref, u_ref, c_ref, carry_ref, *, blocks_per_seq):
    i = pl.program_id(0)

    @pl.when(i % blocks_per_seq == 0)
    def _():
        carry_ref[...] = jnp.zeros_like(carry_ref)

    h = _rms_norm(x_ref[...], g_ref[...]).astype(BF16)
    qkv_ref[...] = (_dot(h, wqkv_ref[...]) * qscale_ref[...]).astype(BF16)
    u_ref[...] = _dot(h, wu_ref[...])
    fl = _dot(h, wf_ref[...]) + bf_ref[...]
    logf = jnp.minimum(fl, 0.0) - jnp.log(1.0 + jnp.exp(-jnp.abs(fl)))
    x = logf.T[:SUBLANES, :]
    tm = x.shape[1]
    lane = lax.broadcasted_iota(jnp.int32, x.shape, 1)
    shift = 1
    while shift < tm:
        x = x + jnp.where(lane >= shift, pltpu.roll(x, shift, 1), 0.0)
        shift *= 2
    cs = x + jnp.concatenate([carry_ref[...]] * (tm // LANES), axis=1)
    carry_ref[...] = jnp.broadcast_to(cs[:, tm - 1:], carry_ref.shape)
    cb = c_ref.shape[3]
    for j in range(tm // cb):
        c_ref[0, j] = cs[:, j * cb:(j + 1) * cb]


def _even_in(x2, g, wqkv, qscale, wu, wf, bf, seq):
    n, d = x2.shape
    tm = IN_ROW_BLOCK
    nb = seq // tm
    cb = ROW_BLOCK
    row = lambda i: (i, 0)
    return pl.pallas_call(
        functools.partial(_even_in_kernel, blocks_per_seq=nb),
        grid=(n // tm,),
        in_specs=[
            pl.BlockSpec((tm, d), row),
            _resident((1, d)),
            _resident(wqkv.shape),
            _resident((1, wqkv.shape[1])),
            _resident(wu.shape),
            _resident(wf.shape),
            _resident((1, LANES)),
        ],
        out_specs=[
            pl.BlockSpec((tm, wqkv.shape[1]), row),
            pl.BlockSpec((tm, wu.shape[1]), row),
            pl.BlockSpec((1, tm // cb, SUBLANES, cb), lambda i: (i // nb, i % nb, 0, 0)),
        ],
        out_shape=[
            jax.ShapeDtypeStruct((n, wqkv.shape[1]), BF16),
            jax.ShapeDtypeStruct((n, wu.shape[1]), F32),
            jax.ShapeDtypeStruct((n // seq, seq // cb, SUBLANES, cb), F32),
        ],
        scratch_shapes=[pltpu.VMEM((SUBLANES, LANES), F32)],
        compiler_params=_params(("arbitrary",), 40),
        name="even_in",
    )(x2, g, wqkv, qscale, wu, wf, bf)


def _fox_kernel(q_ref, k_ref, v_ref, c_ref, o_ref, m_scr, acc_scr, qm_scr, mask_scr, *, blk, pairs):
    grp = pl.program_id(1)
    qi = pl.program_id(2)
    lane = lax.broadcasted_iota(jnp.int32, (1, LANES), 1)
    in_head = [(lane >= FOX_HEAD_DIM * hh) & (lane < FOX_HEAD_DIM * (hh + 1)) for hh in range(2)]
    for pr in range(pairs):
        for hh in range(2):
            qm_scr[2 * pr + hh] = jnp.where(in_head[hh], q_ref[0, :, pr * LANES:(pr + 1) * LANES], 0)

    first = lane < FOX_HEAD_DIM
    ones = [jnp.where(in_head[hh], 1.0, 0.0).astype(BF16) for hh in range(2)]
    half = blk // 2

    @pl.when((pl.program_id(0) == 0) & (grp == 0) & (qi == 0))
    def _():
        row = lax.broadcasted_iota(jnp.int32, (half, half), 0)
        col = lax.broadcasted_iota(jnp.int32, (half, half), 1)
        mask_scr[...] = jnp.where(row >= col, 0.0, -jnp.inf)

    def block(j, r0, nr, nk, diagonal):
        k0 = pl.multiple_of(j * blk, blk)
        rows = slice(r0, r0 + nr)
        for pr in range(pairs):
            k = k_ref[0, pl.ds(k0, nk), pr * LANES:(pr + 1) * LANES]
            v = v_ref[0, pl.ds(k0, nk), pr * LANES:(pr + 1) * LANES]
            probs, alphas, weights = [], [], []
            for hh in range(2):
                slot = 2 * pr + hh
                h = 2 * (grp * pairs + pr) + hh
                c_q = c_ref[0, qi, pl.ds(h, 1), :][:, 0:1]
                c_k = c_ref[0, j, pl.ds(h, 1), pl.ds(0, nk)]
                s = _dot_nt(qm_scr[slot, rows, :], k) + (c_q - c_k) * LOG2_E
                if diagonal:
                    tail = s[:, nk - half:] + mask_scr[...]
                    s = tail if nk == half else jnp.concatenate([s[:, :nk - half], tail], axis=1)
                    m_new = jnp.broadcast_to(jnp.max(s, axis=-1, keepdims=True), (nr, LANES))
                else:
                    m_prev = m_scr[slot, rows, :]
                    m_new = jnp.maximum(m_prev, jnp.max(s, axis=-1, keepdims=True))
                    alphas.append(jnp.exp2(m_prev - m_new))
                probs.append(jnp.exp2((s - jnp.concatenate([m_new] * (nk // LANES), axis=1)).astype(BF16)))
                m_scr[slot, rows, :] = m_new
                weights.append(jnp.concatenate(
                    [jnp.where(in_head[hh], v, 0), jnp.broadcast_to(ones[hh], (nk, LANES))], axis=1))
            pv = _dot(jnp.concatenate(probs, axis=1), jnp.concatenate(weights, axis=0))
            if diagonal:
                acc_scr[pr, rows, :] = pv
            else:
                alpha = jnp.where(first, alphas[0], alphas[1])
                acc_scr[pr, rows, :] = jnp.concatenate([alpha, alpha], axis=1) * acc_scr[pr, rows, :] + pv

    block(qi, 0, half, half, True)
    block(qi, half, half, blk, True)

    def off_diagonal_pair(jj, carry):
        block(2 * jj, 0, blk, blk, False)
        block(2 * jj + 1, 0, blk, blk, False)
        return carry

    lax.fori_loop(0, qi // 2, off_diagonal_pair, 0)

    @pl.when(qi % 2 == 1)
    def _():
        block(qi - 1, 0, blk, blk, False)

    for pr in range(pairs):
        acc = acc_scr[pr]
        o_ref[0, :, pr * LANES:(pr + 1) * LANES] = (acc[:, :LANES] / acc[:, LANES:]).astype(o_ref.dtype)


def _fox(qkv3, c4, heads):
    b, seq, _ = qkv3.shape
    blk = ROW_BLOCK
    pairs = FOX_PAIRS_PER_STEP
    width = pairs * LANES
    groups = heads * FOX_HEAD_DIM // width
    return pl.pallas_call(
        functools.partial(_fox_kernel, blk=blk, pairs=pairs),
        grid=(b, groups, seq // blk),
        in_specs=[
            pl.BlockSpec((1, blk, width), lambda bi, g, qi: (bi, qi, g)),
            pl.BlockSpec((1, seq, width), lambda bi, g, qi: (bi, 0, groups + g)),
            pl.BlockSpec((1, seq, width), lambda bi, g, qi: (bi, 0, 2 * groups + g)),
            pl.BlockSpec((1,) + c4.shape[1:], lambda bi, g, qi: (bi, 0, 0, 0)),
        ],
        out_specs=pl.BlockSpec((1, blk, width), lambda bi, g, qi: (bi, qi, g)),
        out_shape=jax.ShapeDtypeStruct((b, seq, groups * width), BF16),
        scratch_shapes=[
            pltpu.VMEM((2 * pairs, blk, LANES), F32),
            pltpu.VMEM((pairs, blk, 2 * LANES), F32),
            pltpu.VMEM((2 * pairs, blk, LANES), BF16),
            pltpu.VMEM((blk // 2, blk // 2), F32),
        ],
        compiler_params=_params(("arbitrary", "arbitrary", "arbitrary"), 48),
        name="fox_attention",
    )(qkv3, qkv3, qkv3, c4)


def _s5_param_kernel(log_dt_ref, lr_ref, li_ref, br_ref, bi_ref, ar_ref, ai_ref, bbr_ref, bbi_ref):
    dt = jnp.exp(log_dt_ref[...])
    lr = lr_ref[...]
    li = li_ref[...]
    mag = jnp.exp(lr * dt)
    a_re = mag * jnp.cos(li * dt)
    a_im = mag * jnp.sin(li * dt)
    den = lr * lr + li * li
    n_re = a_re - 1.0
    coef_re = (n_re * lr + a_im * li) / den
    coef_im = (a_im * lr - n_re * li) / den
    br = br_ref[...]
    bi = bi_ref[...]
    ar_ref[...] = a_re
    ai_ref[...] = a_im
    bbr_ref[...] = coef_re * br - coef_im * bi
    bbi_ref[...] = coef_re * bi + coef_im * br


def _s5_params(log_dt, lam_re, lam_im, b_re, b_im):
    groups, state, width = b_re.shape
    rep = lambda t: jnp.repeat(t, width, axis=1)
    flat = (groups, state * width)
    out = jax.ShapeDtypeStruct(flat, F32)
    a_re, a_im, bb_re, bb_im = pl.pallas_call(
        _s5_param_kernel,
        out_shape=[out, out, out, out],
        name="s5_discretise",
    )(rep(jnp.broadcast_to(log_dt[:, None], (groups, state))), rep(lam_re), rep(lam_im),
      b_re.reshape(flat), b_im.reshape(flat))
    unrep = lambda t: t.reshape(groups, state, width)[:, :, 0].reshape(1, groups * state)
    return unrep(a_re), unrep(a_im), bb_re.reshape(b_re.shape), bb_im.reshape(b_re.shape)


def _s5_kernel(u_ref, bbt_re_ref, bbt_im_ref, ct_re_ref, ct_im_ref, are_ref, aim_ref, d_ref, wglu_ref, bglu_ref,
               o_ref, bmat_scr, cre_scr, cim_scr, st_scr, ub_scr, ut_scr, v_scr, yt_scr,
               *, tl, batch, width, half, state, group):
    ci = pl.program_id(0)
    tile = 2 * LANES

    @pl.when(ci == 0)
    def _():
        st_scr[...] = jnp.zeros_like(st_scr)
        bmat_scr[...] = jnp.zeros_like(bmat_scr)
        cre_scr[...] = jnp.zeros_like(cre_scr)
        cim_scr[...] = jnp.zeros_like(cim_scr)
        rb = LANES // state * group
        r = lax.broadcasted_iota(jnp.int32, (rb, LANES), 0)
        l = lax.broadcasted_iota(jnp.int32, (rb, LANES), 1)
        keep_b = (r // group) == (l // state)
        for m in range(half // LANES):
            rows = slice(m * rb, (m + 1) * rb)
            for part, src in enumerate((bbt_re_ref, bbt_im_ref)):
                cols = slice(part * half + m * LANES, part * half + (m + 1) * LANES)
                bmat_scr[rows, cols] = jnp.where(keep_b, src[rows, :], 0.0).astype(BF16)
        rc = LANES // group * state
        r = lax.broadcasted_iota(jnp.int32, (rc, LANES), 0)
        l = lax.broadcasted_iota(jnp.int32, (rc, LANES), 1)
        keep_c = (r // state) == (l // group)
        for n in range(width // LANES):
            rows = slice(n * rc, (n + 1) * rc)
            cols = slice(n * LANES, (n + 1) * LANES)
            cre_scr[rows, cols] = jnp.where(keep_c, ct_re_ref[rows, :], 0.0).astype(BF16)
            cim_scr[rows, cols] = jnp.where(keep_c, ct_im_ref[rows, :], 0.0).astype(BF16)

    planes = width // LANES
    for b in range(batch):
        for p in range(planes):
            ub_scr[p, b * tl:(b + 1) * tl, :] = u_ref[b, :, p * LANES:(p + 1) * LANES]
    for t in range(tl):
        for p in range(planes):
            ut_scr[t * batch:(t + 1) * batch, p * LANES:(p + 1) * LANES] = (
                ub_scr[p, pl.ds(t, batch, stride=tl), :])

    ut = ut_scr[...]
    utb = ut.astype(BF16)
    u_cols_per_tile = tile * width // half
    for part in range(2):
        for jt in range(half // tile):
            u0 = (jt * u_cols_per_tile) // LANES * LANES
            c0 = part * half + jt * tile
            v_scr[:, c0:c0 + tile] = _dot(utb[:, u0:u0 + LANES], bmat_scr[u0:u0 + LANES, c0:c0 + tile])

    cg_w = 4 * LANES
    for cg in range(half // cg_w):
        re_cols = slice(cg * cg_w, (cg + 1) * cg_w)
        im_cols = slice(half + cg * cg_w, half + (cg + 1) * cg_w)
        a_re = jnp.broadcast_to(are_ref[:, re_cols], (batch, cg_w))
        a_im = jnp.broadcast_to(aim_ref[:, re_cols], (batch, cg_w))
        s_re = st_scr[:, re_cols]
        s_im = st_scr[:, im_cols]
        for t in range(tl):
            rows = slice(t * batch, (t + 1) * batch)
            s_re, s_im = (a_re * s_re - a_im * s_im + v_scr[rows, re_cols],
                          a_re * s_im + a_im * s_re + v_scr[rows, im_cols])
            v_scr[rows, re_cols] = s_re
            v_scr[rows, im_cols] = s_im
        st_scr[:, re_cols] = s_re
        st_scr[:, im_cols] = s_im

    kw = half * tile // width
    for nt in range(width // tile):
        k0 = nt * kw
        y = (_dot(v_scr[:, k0:k0 + kw].astype(BF16), cre_scr[k0:k0 + kw, nt * tile:(nt + 1) * tile])
             - _dot(v_scr[:, half + k0:half + k0 + kw].astype(BF16),
                    cim_scr[k0:k0 + kw, nt * tile:(nt + 1) * tile]))
        yt_scr[:, nt * tile:(nt + 1) * tile] = y
    y = yt_scr[...] + d_ref[...] * ut
    y = 0.5 * y * (1.0 + jnp.tanh(math.sqrt(2.0 / math.pi) * (y + 0.044715 * (y * y * y))))
    y = y * _sigmoid(_dot(y.astype(BF16), wglu_ref[...]) + bglu_ref[...])
    for p in range(planes):
        ub_scr[p] = y[:, p * LANES:(p + 1) * LANES]
    for b in range(batch):
        for p in range(planes):
            o_ref[b, :, p * LANES:(p + 1) * LANES] = (
                ub_scr[p, pl.ds(b, tl, stride=batch), :].astype(o_ref.dtype))


def _s5(u3, bbt_re, bbt_im, ct_re, ct_im, a_re, a_im, d_skip, w_glu, b_glu, group):
    batch, seq, width = u3.shape
    half = a_re.shape[1]
    state = half * group // width
    tl = S5_TIME_BLOCK
    rows = tl * batch
    blk = pl.BlockSpec((batch, tl, width), lambda ci: (0, ci, 0))
    return pl.pallas_call(
        functools.partial(_s5_kernel, tl=tl, batch=batch, width=width, half=half, state=state, group=group),
        grid=(seq // tl,),
        in_specs=[blk, _resident(bbt_re.shape), _resident(bbt_im.shape), _resident(ct_re.shape),
                  _resident(ct_im.shape), _resident(a_re.shape), _resident(a_im.shape), _resident((1, width)),
                  _resident(w_glu.shape), _resident((1, width))],
        out_specs=blk,
        out_shape=jax.ShapeDtypeStruct(u3.shape, BF16),
        scratch_shapes=[
            pltpu.VMEM((width, 2 * half), BF16),
            pltpu.VMEM((half, width), BF16),
            pltpu.VMEM((half, width), BF16),
            pltpu.VMEM((batch, 2 * half), F32),
            pltpu.VMEM((width // LANES, rows, LANES), F32),
            pltpu.VMEM((rows, width), F32),
            pltpu.VMEM((rows, 2 * half), F32),
            pltpu.VMEM((rows, width), F32),
        ],
        compiler_params=_params(("arbitrary",), 40),
        name="s5_scan",
    )(u3, bbt_re, bbt_im, ct_re, ct_im, a_re, a_im, d_skip, w_glu, b_glu)


def _out_mlp_kernel(a_ref, b_ref, x_ref, wa_ref, wb_ref, g_ref, wup_ref, wdn_ref, gf_ref, o_ref,
                    h_scr, *, final_norm):
    x1 = x_ref[...] + _dot(a_ref[...], wa_ref[...]) + _dot(b_ref[...], wb_ref[...])
    h_scr[...] = _rms_norm(x1, g_ref[...]).astype(BF16)
    o_ref[...] = x1

    def ff_chunk(c, carry):
        c0 = pl.multiple_of(c * MLP_FF_BLOCK, MLP_FF_BLOCK)
        t = jnp.maximum(_dot(h_scr[...], wup_ref[0, :, pl.ds(c0, MLP_FF_BLOCK)]), 0.0)
        o_ref[...] += _dot((t * t).astype(BF16), wdn_ref[0, pl.ds(c0, MLP_FF_BLOCK), :])
        return carry

    lax.fori_loop(0, wup_ref.shape[2] // MLP_FF_BLOCK, ff_chunk, 0, unroll=2)
    if final_norm:
        o_ref[...] = _rms_norm(o_ref[...], gf_ref[...])


def _out_mlp(a, b, x2, wa, wb, g, w_up, w_down, layer, g_final, final_norm):
    n, d = x2.shape
    tm = MLP_ROW_BLOCK
    row = lambda i: (i, 0)
    slab = lambda shape: pl.BlockSpec((1,) + shape[1:], lambda i: (layer, 0, 0), pipeline_mode=pl.Buffered(1))
    return pl.pallas_call(
        functools.partial(_out_mlp_kernel, final_norm=final_norm),
        grid=(n // tm,),
        in_specs=[
            pl.BlockSpec((tm, a.shape[1]), row),
            pl.BlockSpec((tm, b.shape[1]), row),
            pl.BlockSpec((tm, d), row),
            _resident(wa.shape), _resident(wb.shape), _resident((1, d)),
            slab(w_up.shape), slab(w_down.shape), _resident((1, d)),
        ],
        out_specs=pl.BlockSpec((tm, d), row),
        out_shape=jax.ShapeDtypeStruct((n, d), F32),
        scratch_shapes=[pltpu.VMEM((tm, d), BF16)],
        compiler_params=_params(("arbitrary",), 58),
        name="out_mlp",
    )(a, b, x2, wa, wb, g, w_up, w_down, g_final)


def _rope_table_kernel(inv_ref, cos_ref, sin_ref):
    rows, lanes = cos_ref.shape
    pos = lax.broadcasted_iota(jnp.int32, (rows, lanes), 0).astype(F32)
    lane = lax.broadcasted_iota(jnp.int32, (rows, lanes), 1)
    ang = pos * inv_ref[...]
    cos_ref[...] = jnp.cos(ang)
    sin = jnp.sin(ang)
    sin_ref[...] = jnp.where(lane < lanes // 2, -sin, sin)


def _rope_tables(seq, head_dim):
    inv = 1.0 / (10000.0 ** jnp.linspace(0.0, 1.0, head_dim // 2, dtype=F32))
    inv2 = jnp.concatenate([inv, inv])[None, :]
    out = jax.ShapeDtypeStruct((seq, head_dim), F32)
    return pl.pallas_call(_rope_table_kernel, out_shape=[out, out], name="rope_tables")(inv2)


def _odd_in_kernel(x_ref, g_ref, wconv_ref, wq_ref, wk_ref, wv_ref, wg_ref, cw_ref, cos_ref, sin_ref,
                   conv_ref, q_ref, k_ref, v_ref, gate_ref, z_scr, *, blocks_per_seq, taps, head_dim):
    i = pl.program_id(0)
    tm = x_ref.shape[0]
    cw = conv_ref.shape[1]
    pad = SUBLANES

    @pl.when(i % blocks_per_seq == 0)
    def _():
        z_scr[0:pad, :] = jnp.zeros((pad, cw), F32)

    h = _rms_norm(x_ref[...], g_ref[...]).astype(BF16)
    hc = _dot(h, wconv_ref[:, 0:cw])
    gate_b = _dot(h, wconv_ref[:, cw:2 * cw])
    gate_c = _dot(h, wconv_ref[:, 2 * cw:3 * cw])
    z_scr[pad:pad + tm, :] = gate_c * hc
    conv = cw_ref[taps - 1:taps, :] * z_scr[pad:pad + tm, :]
    for j in range(taps - 1):
        shift = taps - 1 - j
        conv = conv + cw_ref[j:j + 1, :] * z_scr[pad - shift:pad - shift + tm, :]
    conv_ref[...] = (gate_b * conv).astype(conv_ref.dtype)
    z_scr[0:pad, :] = z_scr[tm:tm + pad, :]

    cos = cos_ref[...]
    sin = sin_ref[...]

    def rotate(w_ref, out_ref, scale):
        x = _dot(h, w_ref[...])
        for hd in range(x.shape[1] // head_dim):
            xh = x[:, hd * head_dim:(hd + 1) * head_dim]
            r = xh * cos + pltpu.roll(xh, head_dim // 2, 1) * sin
            out_ref[:, hd * head_dim:(hd + 1) * head_dim] = (r * scale).astype(out_ref.dtype)

    rotate(wq_ref, q_ref, 1.0)
    rotate(wk_ref, k_ref, head_dim ** -0.5)
    v_ref[...] = _dot(h, wv_ref[...]).astype(v_ref.dtype)
    gate_ref[...] = _dot(h, wg_ref[...]).astype(gate_ref.dtype)


def _odd_in(x2, g, wconv, wq, wk, wv, wg, conv_w, cos, sin, seq):
    n, d = x2.shape
    tm = IN_ROW_BLOCK
    nb = seq // tm
    cw = conv_w.shape[1]
    rw = wq.shape[1]
    head_dim = cos.shape[1]
    row = lambda i: (i, 0)
    tab = pl.BlockSpec((tm, head_dim), lambda i: (i % nb, 0))
    o = lambda w: jax.ShapeDtypeStruct((n, w), BF16)
    return pl.pallas_call(
        functools.partial(_odd_in_kernel, blocks_per_seq=nb, taps=conv_w.shape[0], head_dim=head_dim),
        grid=(n // tm,),
        in_specs=[pl.BlockSpec((tm, d), row), _resident((1, d)), _resident(wconv.shape),
                  _resident(wq.shape), _resident(wk.shape), _resident(wv.shape), _resident(wg.shape),
                  _resident(conv_w.shape), tab, tab],
        out_specs=[pl.BlockSpec((tm, cw), row)] + [pl.BlockSpec((tm, rw), row)] * 4,
        out_shape=[o(cw), o(rw), o(rw), o(rw), o(rw)],
        scratch_shapes=[pltpu.VMEM((tm + 2 * SUBLANES, cw), F32)],
        compiler_params=_params(("arbitrary",), 40),
        name="odd_in",
    )(x2, g, wconv, wq, wk, wv, wg, conv_w, cos, sin)


def _retention_kernel(q_ref, k_ref, v_ref, g_ref, o_ref, state_scr, decay_scr, *, chunk, heads):
    ri = pl.program_id(1)
    head_dim = q_ref.shape[2] // heads
    log_gamma = [math.log(1.0 - 2.0 ** (-5.0 - h)) for h in range(heads)]

    @pl.when(ri == 0)
    def _():
        state_scr[...] = jnp.zeros_like(state_scr)

    @pl.when((pl.program_id(0) == 0) & (ri == 0))
    def _():
        ridx = lax.broadcasted_iota(jnp.int32, (chunk, chunk), 0)
        cidx = lax.broadcasted_iota(jnp.int32, (chunk, chunk), 1)
        rel = (ridx - cidx).astype(F32)
        for h in range(heads):
            decay_scr[h] = jnp.where(rel >= 0, jnp.exp(log_gamma[h] * jnp.maximum(rel, 0.0)), 0.0)

    idx = lax.broadcasted_iota(jnp.int32, (chunk, 1), 0).astype(F32)
    for h in range(heads):
        query_decay = jnp.exp(log_gamma[h] * (idx + 1.0))
        key_decay = jnp.exp(log_gamma[h] * (chunk - 1.0 - idx))
        chunk_decay = math.exp(log_gamma[h] * chunk)
        cols = slice(h * head_dim, (h + 1) * head_dim)
        state = state_scr[h]
        for c in range(q_ref.shape[1] // chunk):
            rows = slice(c * chunk, (c + 1) * chunk)
            q = q_ref[0, rows, cols]
            k = k_ref[0, rows, cols]
            v = v_ref[0, rows, cols]
            s = _dot_nt(q, k) * decay_scr[h]
            ret = _dot(s.astype(BF16), v) + _dot(q, state.astype(BF16)) * query_decay
            kd = (k.astype(F32) * key_decay).astype(BF16)
            state = state * chunk_decay + _dot_tn(kd, v)
            mu = jnp.mean(ret, axis=-1, keepdims=True)
            dev = ret - mu
            var = jnp.mean(dev * dev, axis=-1, keepdims=True)
            g = g_ref[0, rows, cols].astype(F32)
            o_ref[0, rows, cols] = (g * _sigmoid(g) * (dev * lax.rsqrt(var + EPS))).astype(o_ref.dtype)
        state_scr[h] = state


def _retention(q3, k3, v3, g3):
    b, seq, rw = q3.shape
    head_dim = rw // RET_HEADS
    tr = RET_ROW_BLOCK
    blk = pl.BlockSpec((1, tr, rw), lambda bi, ri: (bi, ri, 0))
    return pl.pallas_call(
        functools.partial(_retention_kernel, chunk=RET_CHUNK, heads=RET_HEADS),
        grid=(b, seq // tr),
        in_specs=[blk, blk, blk, blk],
        out_specs=blk,
        out_shape=jax.ShapeDtypeStruct(q3.shape, BF16),
        scratch_shapes=[pltpu.VMEM((RET_HEADS, head_dim, head_dim), F32),
                        pltpu.VMEM((RET_HEADS, RET_CHUNK, RET_CHUNK), F32)],
        compiler_params=_params(("arbitrary", "arbitrary"), 32),
        name="retention",
    )(q3, k3, v3, g3)


def _even_layer(x2, batch, seq, norm_g, w_in, b_forget, log_dt, lam_re, lam_im, b_re, b_im, c_re, c_im,
                d_skip, w_glu, b_glu, w_out, mlp_g, w_up, w_down, layer, g_final, final_norm):
    d = x2.shape[1]
    heads = b_forget.shape[0]
    fw = heads * FOX_HEAD_DIM
    sw = d_skip.shape[0]
    wqkv = w_in[:, :3 * fw].astype(BF16)
    wf = jnp.pad(w_in[:, 3 * fw:3 * fw + heads], ((0, 0), (0, LANES - heads))).astype(BF16)
    wu = w_in[:, 3 * fw + heads:].astype(BF16)
    bf = jnp.pad(b_forget.astype(F32), (0, LANES - heads))[None, :]
    qscale = jnp.concatenate([jnp.full((fw,), LOG2_E * FOX_HEAD_DIM ** -0.5, F32), jnp.ones((2 * fw,), F32)])[None, :]
    qkv, u, c4 = _even_in(x2, norm_g[None, :], wqkv, qscale, wu, wf, bf, seq)
    fox = _fox(qkv.reshape(batch, seq, 3 * fw), c4, heads)

    a_re, a_im, bb_re, bb_im = _s5_params(log_dt, lam_re, lam_im, b_re, b_im)
    groups, state, group = b_re.shape
    tile_b = lambda t: jnp.tile(t.transpose(0, 2, 1).reshape(groups * group, state), (1, LANES // state))
    tile_c = lambda t: jnp.tile(t.transpose(0, 2, 1).reshape(groups * state, group), (1, LANES // group))
    s5 = _s5(u.reshape(batch, seq, sw), tile_b(bb_re), tile_b(bb_im), tile_c(c_re), tile_c(c_im), a_re, a_im,
             d_skip[None, :], w_glu.astype(BF16), b_glu[None, :], group)

    w_out = w_out.astype(BF16)
    return _out_mlp(fox.reshape(batch * seq, fw), s5.reshape(batch * seq, sw), x2, w_out[:fw], w_out[fw:],
                    mlp_g[None, :], w_up, w_down, layer, g_final[None, :], final_norm)


def _odd_layer(x2, batch, seq, norm_g, w_in, conv_w, w_out, mlp_g, w_up, w_down, layer, g_final, final_norm):
    d = x2.shape[1]
    cw = conv_w.shape[1]
    rw = d - cw
    head_dim = rw // RET_HEADS
    perm = jnp.concatenate([jnp.arange(0, head_dim, 2), jnp.arange(1, head_dim, 2)])
    perm = (jnp.arange(RET_HEADS)[:, None] * head_dim + perm[None, :]).reshape(-1)
    w_in = w_in.astype(BF16)
    wconv = w_in[:, :3 * cw]
    wq = w_in[:, 3 * cw:3 * cw + rw][:, perm]
    wk = w_in[:, 3 * cw + rw:3 * cw + 2 * rw][:, perm]
    wv = w_in[:, 3 * cw + 2 * rw:3 * cw + 3 * rw]
    wg = w_in[:, 3 * cw + 3 * rw:]
    cos, sin = _rope_tables(seq, head_dim)
    conv, q, k, v, gate = _odd_in(x2, norm_g[None, :], wconv, wq, wk, wv, wg, conv_w.astype(F32), cos, sin, seq)
    to3 = lambda t: t.reshape(batch, seq, rw)
    ret = _retention(to3(q), to3(k), to3(v), to3(gate))
    w_out = w_out.astype(BF16)
    return _out_mlp(conv, ret.reshape(batch * seq, rw), x2, w_out[:cw], w_out[cw:],
                    mlp_g[None, :], w_up, w_down, layer, g_final[None, :], final_norm)


def kernel(x, even_norm_mix, even_w_in, even_b_forget, even_s5_log_dt, even_s5_lambda_re, even_s5_lambda_im, even_s5_b_re, even_s5_b_im, even_s5_c_re, even_s5_c_im, even_s5_d, even_s5_w_glu, even_s5_b_glu, even_w_out, odd_norm_mix, odd_w_in, odd_conv_w, odd_w_out, mlp_norm, mlp_w_up, mlp_w_down, final_norm):
    batch, seq, d = x.shape
    depth = mlp_norm.shape[0]
    x2 = x.reshape(batch * seq, d)
    w_up = mlp_w_up.astype(BF16)
    w_down = mlp_w_down.astype(BF16)
    for layer in range(depth):
        j = layer // 2
        last = layer == depth - 1
        if layer % 2 == 0:
            x2 = _even_layer(x2, batch, seq, even_norm_mix[j], even_w_in[j], even_b_forget[j],
                             even_s5_log_dt[j], even_s5_lambda_re[j], even_s5_lambda_im[j],
                             even_s5_b_re[j], even_s5_b_im[j], even_s5_c_re[j], even_s5_c_im[j],
                             even_s5_d[j], even_s5_w_glu[j], even_s5_b_glu[j], even_w_out[j],
                             mlp_norm[layer], w_up, w_down, layer, final_norm, last)
        else:
            x2 = _odd_layer(x2, batch, seq, odd_norm_mix[j], odd_w_in[j], odd_conv_w[j], odd_w_out[j],
                            mlp_norm[layer], w_up, w_down, layer, final_norm, last)
    return x2.reshape(batch, seq, d)
```

```python
import functools
import math

import jax
import jax.numpy as jnp
from jax import lax
from jax.experimental import pallas as pl
from jax.experimental.pallas import tpu as pltpu

F32 = jnp.float32
BF16 = jnp.bfloat16
EPS = 1e-6

LANES = 128
SUBLANES = 8
FOX_HEAD_DIM = 64
RET_HEADS = 4
RET_CHUNK = 256
ROW_BLOCK = 512
IN_ROW_BLOCK = 1024
S5_TIME_BLOCK = 64
RET_ROW_BLOCK = 1024
MLP_FF_BLOCK = 1024
MLP_ROW_BLOCK = 1024
MLP_STAGE_ROWS = 512
MIB = 1024 * 1024
LOG2_E = math.log2(math.e)
FOX_PAIRS_PER_STEP = 4


def _params(semantics, vmem_mib):
    return pltpu.CompilerParams(dimension_semantics=semantics, vmem_limit_bytes=vmem_mib * MIB)


def _resident(shape):
    return pl.BlockSpec(shape, lambda *_: (0,) * len(shape), pipeline_mode=pl.Buffered(1))


def _rms_norm(x, g):
    return x * lax.rsqrt(jnp.mean(x * x, axis=-1, keepdims=True) + EPS) * g


def _sigmoid(x):
    return 1.0 / (1.0 + jnp.exp(-x))


def _dot(a, b):
    return jnp.dot(a, b, preferred_element_type=F32)


def _dot_nt(a, b):
    return lax.dot_general(a, b, (((1,), (1,)), ((), ())), preferred_element_type=F32)


def _dot_tn(a, b):
    return lax.dot_general(a, b, (((0,), (0,)), ((), ())), preferred_element_type=F32)


def _even_in_kernel(x_ref, g_ref, wqkv_ref, qscale_ref, wu_ref, wf_ref, bf_ref,
                    qkv_ref, u_ref, c_ref, carry_ref, *, blocks_per_seq):
    i = pl.program_id(0)

    @pl.when(i % blocks_per_seq == 0)
    def _():
        carry_ref[...] = jnp.zeros_like(carry_ref)

    h = _rms_norm(x_ref[...], g_ref[...]).astype(BF16)
    qkv_ref[...] = (_dot(h, wqkv_ref[...]) * qscale_ref[...]).astype(BF16)
    u_ref[...] = _dot(h, wu_ref[...])
    fl = _dot(h, wf_ref[...]) + bf_ref[...]
    logf = jnp.minimum(fl, 0.0) - jnp.log(1.0 + jnp.exp(-jnp.abs(fl)))
    x = logf.T[:SUBLANES, :]
    tm = x.shape[1]
    lane = lax.broadcasted_iota(jnp.int32, x.shape, 1)
    shift = 1
    while shift < tm:
        x = x + jnp.where(lane >= shift, pltpu.roll(x, shift, 1), 0.0)
        shift *= 2
    cs = x + jnp.concatenate([carry_ref[...]] * (tm // LANES), axis=1)
    carry_ref[...] = jnp.broadcast_to(cs[:, tm - 1:], carry_ref.shape)
    cb = c_ref.shape[3]
    for j in range(tm // cb):
        c_ref[0, j] = cs[:, j * cb:(j + 1) * cb]


def _even_in(x2, g, wqkv, qscale, wu, wf, bf, seq):
    n, d = x2.shape
    tm = IN_ROW_BLOCK
    nb = seq // tm
    cb = ROW_BLOCK
    row = lambda i: (i, 0)
    return pl.pallas_call(
        functools.partial(_even_in_kernel, blocks_per_seq=nb),
        grid=(n // tm,),
        in_specs=[
            pl.BlockSpec((tm, d), row),
            _resident((1, d)),
            _resident(wqkv.shape),
            _resident((1, wqkv.shape[1])),
            _resident(wu.shape),
            _resident(wf.shape),
            _resident((1, LANES)),
        ],
        out_specs=[
            pl.BlockSpec((tm, wqkv.shape[1]), row),
            pl.BlockSpec((tm, wu.shape[1]), row),
            pl.BlockSpec((1, tm // cb, SUBLANES, cb), lambda i: (i // nb, i % nb, 0, 0)),
        ],
        out_shape=[
            jax.ShapeDtypeStruct((n, wqkv.shape[1]), BF16),
            jax.ShapeDtypeStruct((n, wu.shape[1]), F32),
            jax.ShapeDtypeStruct((n // seq, seq // cb, SUBLANES, cb), F32),
        ],
        scratch_shapes=[pltpu.VMEM((SUBLANES, LANES), F32)],
        compiler_params=_params(("arbitrary",), 40),
        name="even_in",
    )(x2, g, wqkv, qscale, wu, wf, bf)


def _fox_kernel(q_ref, k_ref, v_ref, c_ref, o_ref, m_scr, acc_scr, qm_scr, mask_scr, *, blk, pairs):
    grp = pl.program_id(1)
    qi = pl.program_id(2)
    lane = lax.broadcasted_iota(jnp.int32, (1, LANES), 1)
    in_head = [(lane >= FOX_HEAD_DIM * hh) & (lane < FOX_HEAD_DIM * (hh + 1)) for hh in range(2)]
    for pr in range(pairs):
        for hh in range(2):
            qm_scr[2 * pr + hh] = jnp.where(in_head[hh], q_ref[0, :, pr * LANES:(pr + 1) * LANES], 0)

    first = lane < FOX_HEAD_DIM
    ones = [jnp.where(in_head[hh], 1.0, 0.0).astype(BF16) for hh in range(2)]
    half = blk // 2

    @pl.when((pl.program_id(0) == 0) & (grp == 0) & (qi == 0))
    def _():
        row = lax.broadcasted_iota(jnp.int32, (half, half), 0)
        col = lax.broadcasted_iota(jnp.int32, (half, half), 1)
        mask_scr[...] = jnp.where(row >= col, 0.0, -jnp.inf)

    def block(j, r0, nr, nk, diagonal):
        k0 = pl.multiple_of(j * blk, blk)
        rows = slice(r0, r0 + nr)
        for pr in range(pairs):
            k = k_ref[0, pl.ds(k0, nk), pr * LANES:(pr + 1) * LANES]
            v = v_ref[0, pl.ds(k0, nk), pr * LANES:(pr + 1) * LANES]
            probs, alphas, weights = [], [], []
            for hh in range(2):
                slot = 2 * pr + hh
                h = 2 * (grp * pairs + pr) + hh
                c_q = c_ref[0, qi, pl.ds(h, 1), :][:, 0:1]
                c_k = c_ref[0, j, pl.ds(h, 1), pl.ds(0, nk)]
                s = _dot_nt(qm_scr[slot, rows, :], k) + (c_q - c_k) * LOG2_E
                if diagonal:
                    tail = s[:, nk - half:] + mask_scr[...]
                    s = tail if nk == half else jnp.concatenate([s[:, :nk - half], tail], axis=1)
                    m_new = jnp.broadcast_to(jnp.max(s, axis=-1, keepdims=True), (nr, LANES))
                else:
                    m_prev = m_scr[slot, rows, :]
                    m_new = jnp.maximum(m_prev, jnp.max(s, axis=-1, keepdims=True))
                    alphas.append(jnp.exp2(m_prev - m_new))
                probs.append(jnp.exp2((s - jnp.concatenate([m_new] * (nk // LANES), axis=1)).astype(BF16)))
                m_scr[slot, rows, :] = m_new
                weights.append(jnp.concatenate(
                    [jnp.where(in_head[hh], v, 0), jnp.broadcast_to(ones[hh], (nk, LANES))], axis=1))
            pv = _dot(jnp.concatenate(probs, axis=1), jnp.concatenate(weights, axis=0))
            if diagonal:
                acc_scr[pr, rows, :] = pv
            else:
                alpha = jnp.where(first, alphas[0], alphas[1])
                acc_scr[pr, rows, :] = jnp.concatenate([alpha, alpha], axis=1) * acc_scr[pr, rows, :] + pv

    block(qi, 0, half, half, True)
    block(qi, half, half, blk, True)

    def off_diagonal_pair(jj, carry):
        block(2 * jj, 0, blk, blk, False)
        block(2 * jj + 1, 0, blk, blk, False)
        return carry

    lax.fori_loop(0, qi // 2, off_diagonal_pair, 0)

    @pl.when(qi % 2 == 1)
    def _():
        block(qi - 1, 0, blk, blk, False)

    for pr in range(pairs):
        acc = acc_scr[pr]
        o_ref[0, :, pr * LANES:(pr + 1) * LANES] = (acc[:, :LANES] / acc[:, LANES:]).astype(o_ref.dtype)


def _fox(qkv3, c4, heads):
    b, seq, _ = qkv3.shape
    blk = ROW_BLOCK
    pairs = FOX_PAIRS_PER_STEP
    width = pairs * LANES
    groups = heads * FOX_HEAD_DIM // width
    return pl.pallas_call(
        functools.partial(_fox_kernel, blk=blk, pairs=pairs),
        grid=(b, groups, seq // blk),
        in_specs=[
            pl.BlockSpec((1, blk, width), lambda bi, g, qi: (bi, qi, g)),
            pl.BlockSpec((1, seq, width), lambda bi, g, qi: (bi, 0, groups + g)),
            pl.BlockSpec((1, seq, width), lambda bi, g, qi: (bi, 0, 2 * groups + g)),
            pl.BlockSpec((1,) + c4.shape[1:], lambda bi, g, qi: (bi, 0, 0, 0)),
        ],
        out_specs=pl.BlockSpec((1, blk, width), lambda bi, g, qi: (bi, qi, g)),
        out_shape=jax.ShapeDtypeStruct((b, seq, groups * width), BF16),
        scratch_shapes=[
            pltpu.VMEM((2 * pairs, blk, LANES), F32),
            pltpu.VMEM((pairs, blk, 2 * LANES), F32),
            pltpu.VMEM((2 * pairs, blk, LANES), BF16),
            pltpu.VMEM((blk // 2, blk // 2), F32),
        ],
        compiler_params=_params(("arbitrary", "arbitrary", "arbitrary"), 48),
        name="fox_attention",
    )(qkv3, qkv3, qkv3, c4)


def _s5_param_kernel(log_dt_ref, lr_ref, li_ref, br_ref, bi_ref, ar_ref, ai_ref, bbr_ref, bbi_ref):
    dt = jnp.exp(log_dt_ref[...])
    lr = lr_ref[...]
    li = li_ref[...]
    mag = jnp.exp(lr * dt)
    a_re = mag * jnp.cos(li * dt)
    a_im = mag * jnp.sin(li * dt)
    den = lr * lr + li * li
    n_re = a_re - 1.0
    coef_re = (n_re * lr + a_im * li) / den
    coef_im = (a_im * lr - n_re * li) / den
    br = br_ref[...]
    bi = bi_ref[...]
    ar_ref[...] = a_re
    ai_ref[...] = a_im
    bbr_ref[...] = coef_re * br - coef_im * bi
    bbi_ref[...] = coef_re * bi + coef_im * br


def _s5_params(log_dt, lam_re, lam_im, b_re, b_im):
    groups, state, width = b_re.shape
    rep = lambda t: jnp.repeat(t, width, axis=1)
    flat = (groups, state * width)
    out = jax.ShapeDtypeStruct(flat, F32)
    a_re, a_im, bb_re, bb_im = pl.pallas_call(
        _s5_param_kernel,
        out_shape=[out, out, out, out],
        name="s5_discretise",
    )(rep(jnp.broadcast_to(log_dt[:, None], (groups, state))), rep(lam_re), rep(lam_im),
      b_re.reshape(flat), b_im.reshape(flat))
    unrep = lambda t: t.reshape(groups, state, width)[:, :, 0].reshape(1, groups * state)
    return unrep(a_re), unrep(a_im), bb_re.reshape(b_re.shape), bb_im.reshape(b_re.shape)


def _s5_kernel(u_ref, bbt_re_ref, bbt_im_ref, ct_re_ref, ct_im_ref, are_ref, aim_ref, d_ref, wglu_ref, bglu_ref,
               o_ref, bmat_scr, cre_scr, cim_scr, st_scr, ub_scr, ut_scr, v_scr, yt_scr,
               *, tl, batch, width, half, state, group):
    ci = pl.program_id(0)
    tile = 2 * LANES

    @pl.when(ci == 0)
    def _():
        st_scr[...] = jnp.zeros_like(st_scr)
        bmat_scr[...] = jnp.zeros_like(bmat_scr)
        cre_scr[...] = jnp.zeros_like(cre_scr)
        cim_scr[...] = jnp.zeros_like(cim_scr)
        rb = LANES // state * group
        r = lax.broadcasted_iota(jnp.int32, (rb, LANES), 0)
        l = lax.broadcasted_iota(jnp.int32, (rb, LANES), 1)
        keep_b = (r // group) == (l // state)
        for m in range(half // LANES):
            rows = slice(m * rb, (m + 1) * rb)
            for part, src in enumerate((bbt_re_ref, bbt_im_ref)):
                cols = slice(part * half + m * LANES, part * half + (m + 1) * LANES)
                bmat_scr[rows, cols] = jnp.where(keep_b, src[rows, :], 0.0).astype(BF16)
        rc = LANES // group * state
        r = lax.broadcasted_iota(jnp.int32, (rc, LANES), 0)
        l = lax.broadcasted_iota(jnp.int32, (rc, LANES), 1)
        keep_c = (r // state) == (l // group)
        for n in range(width // LANES):
            rows = slice(n * rc, (n + 1) * rc)
            cols = slice(n * LANES, (n + 1) * LANES)
            cre_scr[rows, cols] = jnp.where(keep_c, ct_re_ref[rows, :], 0.0).astype(BF16)
            cim_scr[rows, cols] = jnp.where(keep_c, ct_im_ref[rows, :], 0.0).astype(BF16)

    planes = width // LANES
    for b in range(batch):
        for p in range(planes):
            ub_scr[p, b * tl:(b + 1) * tl, :] = u_ref[b, :, p * LANES:(p + 1) * LANES]
    for t in range(tl):
        for p in range(planes):
            ut_scr[t * batch:(t + 1) * batch, p * LANES:(p + 1) * LANES] = (
                ub_scr[p, pl.ds(t, batch, stride=tl), :])

    ut = ut_scr[...]
    utb = ut.astype(BF16)
    u_cols_per_tile = tile * width // half
    for part in range(2):
        for jt in range(half // tile):
            u0 = (jt * u_cols_per_tile) // LANES * LANES
            c0 = part * half + jt * tile
            v_scr[:, c0:c0 + tile] = _dot(utb[:, u0:u0 + LANES], bmat_scr[u0:u0 + LANES, c0:c0 + tile])

    cg_w = 4 * LANES
    for cg in range(half // cg_w):
        re_cols = slice(cg * cg_w, (cg + 1) * cg_w)
        im_cols = slice(half + cg * cg_w, half + (cg + 1) * cg_w)
        a_re = jnp.broadcast_to(are_ref[:, re_cols], (batch, cg_w))
        a_im = jnp.broadcast_to(aim_ref[:, re_cols], (batch, cg_w))
        s_re = st_scr[:, re_cols]
        s_im = st_scr[:, im_cols]
        for t in range(tl):
            rows = slice(t * batch, (t + 1) * batch)
            s_re, s_im = (a_re * s_re - a_im * s_im + v_scr[rows, re_cols],
                          a_re * s_im + a_im * s_re + v_scr[rows, im_cols])
            v_scr[rows, re_cols] = s_re
            v_scr[rows, im_cols] = s_im
        st_scr[:, re_cols] = s_re
        st_scr[:, im_cols] = s_im

    kw = half * tile // width
    for nt in range(width // tile):
        k0 = nt * kw
        y = (_dot(v_scr[:, k0:k0 + kw].astype(BF16), cre_scr[k0:k0 + kw, nt * tile:(nt + 1) * tile])
             - _dot(v_scr[:, half + k0:half + k0 + kw].astype(BF16),
                    cim_scr[k0:k0 + kw, nt * tile:(nt + 1) * tile]))
        yt_scr[:, nt * tile:(nt + 1) * tile] = y
    y = yt_scr[...] + d_ref[...] * ut
    y = 0.5 * y * (1.0 + jnp.tanh(math.sqrt(2.0 / math.pi) * (y + 0.044715 * (y * y * y))))
    y = y * _sigmoid(_dot(y.astype(BF16), wglu_ref[...]) + bglu_ref[...])
    for p in range(planes):
        ub_scr[p] = y[:, p * LANES:(p + 1) * LANES]
    for b in range(batch):
        for p in range(planes):
            o_ref[b, :, p * LANES:(p + 1) * LANES] = (
                ub_scr[p, pl.ds(b, tl, stride=batch), :].astype(o_ref.dtype))


def _s5(u3, bbt_re, bbt_im, ct_re, ct_im, a_re, a_im, d_skip, w_glu, b_glu, group):
    batch, seq, width = u3.shape
    half = a_re.shape[1]
    state = half * group // width
    tl = S5_TIME_BLOCK
    rows = tl * batch
    blk = pl.BlockSpec((batch, tl, width), lambda ci: (0, ci, 0))
    return pl.pallas_call(
        functools.partial(_s5_kernel, tl=tl, batch=batch, width=width, half=half, state=state, group=group),
        grid=(seq // tl,),
        in_specs=[blk, _resident(bbt_re.shape), _resident(bbt_im.shape), _resident(ct_re.shape),
                  _resident(ct_im.shape), _resident(a_re.shape), _resident(a_im.shape), _resident((1, width)),
                  _resident(w_glu.shape), _resident((1, width))],
        out_specs=blk,
        out_shape=jax.ShapeDtypeStruct(u3.shape, BF16),
        scratch_shapes=[
            pltpu.VMEM((width, 2 * half), BF16),
            pltpu.VMEM((half, width), BF16),
            pltpu.VMEM((half, width), BF16),
            pltpu.VMEM((batch, 2 * half), F32),
            pltpu.VMEM((width // LANES, rows, LANES), F32),
            pltpu.VMEM((rows, width), F32),
            pltpu.VMEM((rows, 2 * half), F32),
            pltpu.VMEM((rows, width), F32),
        ],
        compiler_params=_params(("arbitrary",), 40),
        name="s5_scan",
    )(u3, bbt_re, bbt_im, ct_re, ct_im, a_re, a_im, d_skip, w_glu, b_glu)


def _out_mlp_kernel(a_ref, b_ref, x_ref, wa_ref, wb_ref, g_ref, wup_hbm, wdn_hbm, gf_ref, o_ref,
                    h_scr, wup_scr, wdn_scr, stage_scr, sem, *, layer, final_norm):
    d, ff = wup_scr.shape
    rows = stage_scr.shape[1]

    @pl.when(pl.program_id(0) == 0)
    def _():
        tiles = ([(wup_hbm, wup_scr, r, c) for r in range(d // rows) for c in range(ff // d)]
                 + [(wdn_hbm, wdn_scr, r, 0) for r in range(ff // rows)])

        def tile_copy(n):
            src, _, r, c = tiles[n]
            return pltpu.make_async_copy(src.at[layer, pl.ds(r * rows, rows), pl.ds(c * d, d)],
                                         stage_scr.at[n % 2], sem.at[n % 2])

        tile_copy(0).start()
        for n, (_, dst, r, c) in enumerate(tiles):
            if n + 1 < len(tiles):
                tile_copy(n + 1).start()
            tile_copy(n).wait()
            dst[r * rows:(r + 1) * rows, c * d:(c + 1) * d] = stage_scr[n % 2].astype(BF16)

    x1 = x_ref[...] + _dot(a_ref[...], wa_ref[...]) + _dot(b_ref[...], wb_ref[...])
    h_scr[...] = _rms_norm(x1, g_ref[...]).astype(BF16)
    o_ref[...] = x1

    def ff_chunk(c, carry):
        c0 = pl.multiple_of(c * MLP_FF_BLOCK, MLP_FF_BLOCK)
        t = jnp.maximum(_dot(h_scr[...], wup_scr[:, pl.ds(c0, MLP_FF_BLOCK)]), 0.0)
        o_ref[...] += _dot((t * t).astype(BF16), wdn_scr[pl.ds(c0, MLP_FF_BLOCK), :])
        return carry

    lax.fori_loop(0, ff // MLP_FF_BLOCK, ff_chunk, 0, unroll=2)
    if final_norm:
        o_ref[...] = _rms_norm(o_ref[...], gf_ref[...])


def _out_mlp(a, b, x2, wa, wb, g, w_up, w_down, layer, g_final, final_norm):
    n, d = x2.shape
    ff = w_up.shape[2]
    tm = MLP_ROW_BLOCK
    row = lambda i: (i, 0)
    return pl.pallas_call(
        functools.partial(_out_mlp_kernel, layer=layer, final_norm=final_norm),
        grid=(n // tm,),
        in_specs=[
            pl.BlockSpec((tm, a.shape[1]), row),
            pl.BlockSpec((tm, b.shape[1]), row),
            pl.BlockSpec((tm, d), row),
            _resident(wa.shape), _resident(wb.shape), _resident((1, d)),
            pl.BlockSpec(memory_space=pl.ANY), pl.BlockSpec(memory_space=pl.ANY), _resident((1, d)),
        ],
        out_specs=pl.BlockSpec((tm, d), row),
        out_shape=jax.ShapeDtypeStruct((n, d), F32),
        scratch_shapes=[pltpu.VMEM((tm, d), BF16), pltpu.VMEM((d, ff), BF16), pltpu.VMEM((ff, d), BF16),
                        pltpu.VMEM((2, MLP_STAGE_ROWS, d), F32), pltpu.SemaphoreType.DMA((2,))],
        compiler_params=_params(("arbitrary",), 58),
        name="out_mlp",
    )(a, b, x2, wa, wb, g, w_up, w_down, g_final)


def _rope_table_kernel(inv_ref, cos_ref, sin_ref):
    rows, lanes = cos_ref.shape
    pos = lax.broadcasted_iota(jnp.int32, (rows, lanes), 0).astype(F32)
    lane = lax.broadcasted_iota(jnp.int32, (rows, lanes), 1)
    ang = pos * inv_ref[...]
    cos_ref[...] = jnp.cos(ang)
    sin = jnp.sin(ang)
    sin_ref[...] = jnp.where(lane < lanes // 2, -sin, sin)


def _rope_tables(seq, head_dim):
    inv = 1.0 / (10000.0 ** jnp.linspace(0.0, 1.0, head_dim // 2, dtype=F32))
    inv2 = jnp.concatenate([inv, inv])[None, :]
    out = jax.ShapeDtypeStruct((seq, head_dim), F32)
    return pl.pallas_call(_rope_table_kernel, out_shape=[out, out], name="rope_tables")(inv2)


def _odd_in_kernel(x_ref, g_ref, wconv_ref, wq_ref, wk_ref, wv_ref, wg_ref, cw_ref, cos_ref, sin_ref,
                   conv_ref, q_ref, k_ref, v_ref, gate_ref, z_scr, *, blocks_per_seq, taps, head_dim):
    i = pl.program_id(0)
    tm = x_ref.shape[0]
    cw = conv_ref.shape[1]
    pad = SUBLANES

    @pl.when(i % blocks_per_seq == 0)
    def _():
        z_scr[0:pad, :] = jnp.zeros((pad, cw), F32)

    h = _rms_norm(x_ref[...], g_ref[...]).astype(BF16)
    hc = _dot(h, wconv_ref[:, 0:cw])
    gate_b = _dot(h, wconv_ref[:, cw:2 * cw])
    gate_c = _dot(h, wconv_ref[:, 2 * cw:3 * cw])
    z_scr[pad:pad + tm, :] = gate_c * hc
    conv = cw_ref[taps - 1:taps, :] * z_scr[pad:pad + tm, :]
    for j in range(taps - 1):
        shift = taps - 1 - j
        conv = conv + cw_ref[j:j + 1, :] * z_scr[pad - shift:pad - shift + tm, :]
    conv_ref[...] = (gate_b * conv).astype(conv_ref.dtype)
    z_scr[0:pad, :] = z_scr[tm:tm + pad, :]

    cos = cos_ref[...]
    sin = sin_ref[...]

    def rotate(w_ref, out_ref, scale):
        x = _dot(h, w_ref[...])
        for hd in range(x.shape[1] // head_dim):
            xh = x[:, hd * head_dim:(hd + 1) * head_dim]
            r = xh * cos + pltpu.roll(xh, head_dim // 2, 1) * sin
            out_ref[:, hd * head_dim:(hd + 1) * head_dim] = (r * scale).astype(out_ref.dtype)

    rotate(wq_ref, q_ref, 1.0)
    rotate(wk_ref, k_ref, head_dim ** -0.5)
    v_ref[...] = _dot(h, wv_ref[...]).astype(v_ref.dtype)
    gate_ref[...] = _dot(h, wg_ref[...]).astype(gate_ref.dtype)


def _odd_in(x2, g, wconv, wq, wk, wv, wg, conv_w, cos, sin, seq):
    n, d = x2.shape
    tm = IN_ROW_BLOCK
    nb = seq // tm
    cw = conv_w.shape[1]
    rw = wq.shape[1]
    head_dim = cos.shape[1]
    row = lambda i: (i, 0)
    tab = pl.BlockSpec((tm, head_dim), lambda i: (i % nb, 0))
    o = lambda w: jax.ShapeDtypeStruct((n, w), BF16)
    return pl.pallas_call(
        functools.partial(_odd_in_kernel, blocks_per_seq=nb, taps=conv_w.shape[0], head_dim=head_dim),
        grid=(n // tm,),
        in_specs=[pl.BlockSpec((tm, d), row), _resident((1, d)), _resident(wconv.shape),
                  _resident(wq.shape), _resident(wk.shape), _resident(wv.shape), _resident(wg.shape),
                  _resident(conv_w.shape), tab, tab],
        out_specs=[pl.BlockSpec((tm, cw), row)] + [pl.BlockSpec((tm, rw), row)] * 4,
        out_shape=[o(cw), o(rw), o(rw), o(rw), o(rw)],
        scratch_shapes=[pltpu.VMEM((tm + 2 * SUBLANES, cw), F32)],
        compiler_params=_params(("arbitrary",), 40),
        name="odd_in",
    )(x2, g, wconv, wq, wk, wv, wg, conv_w, cos, sin)


def _retention_kernel(q_ref, k_ref, v_ref, g_ref, o_ref, state_scr, decay_scr, *, chunk, heads):
    ri = pl.program_id(1)
    head_dim = q_ref.shape[2] // heads
    log_gamma = [math.log(1.0 - 2.0 ** (-5.0 - h)) for h in range(heads)]

    @pl.when(ri == 0)
    def _():
        state_scr[...] = jnp.zeros_like(state_scr)

    @pl.when((pl.program_id(0) == 0) & (ri == 0))
    def _():
        ridx = lax.broadcasted_iota(jnp.int32, (chunk, chunk), 0)
        cidx = lax.broadcasted_iota(jnp.int32, (chunk, chunk), 1)
        rel = (ridx - cidx).astype(F32)
        for h in range(heads):
            decay_scr[h] = jnp.where(rel >= 0, jnp.exp(log_gamma[h] * jnp.maximum(rel, 0.0)), 0.0)

    idx = lax.broadcasted_iota(jnp.int32, (chunk, 1), 0).astype(F32)
    for h in range(heads):
        query_decay = jnp.exp(log_gamma[h] * (idx + 1.0))
        key_decay = jnp.exp(log_gamma[h] * (chunk - 1.0 - idx))
        chunk_decay = math.exp(log_gamma[h] * chunk)
        cols = slice(h * head_dim, (h + 1) * head_dim)
        state = state_scr[h]
        for c in range(q_ref.shape[1] // chunk):
            rows = slice(c * chunk, (c + 1) * chunk)
            q = q_ref[0, rows, cols]
            k = k_ref[0, rows, cols]
            v = v_ref[0, rows, cols]
            s = _dot_nt(q, k) * decay_scr[h]
            ret = _dot(s.astype(BF16), v) + _dot(q, state.astype(BF16)) * query_decay
            kd = (k.astype(F32) * key_decay).astype(BF16)
            state = state * chunk_decay + _dot_tn(kd, v)
            mu = jnp.mean(ret, axis=-1, keepdims=True)
            dev = ret - mu
            var = jnp.mean(dev * dev, axis=-1, keepdims=True)
            g = g_ref[0, rows, cols].astype(F32)
            o_ref[0, rows, cols] = (g * _sigmoid(g) * (dev * lax.rsqrt(var + EPS))).astype(o_ref.dtype)
        state_scr[h] = state


def _retention(q3, k3, v3, g3):
    b, seq, rw = q3.shape
    head_dim = rw // RET_HEADS
    tr = RET_ROW_BLOCK
    blk = pl.BlockSpec((1, tr, rw), lambda bi, ri: (bi, ri, 0))
    return pl.pallas_call(
        functools.partial(_retention_kernel, chunk=RET_CHUNK, heads=RET_HEADS),
        grid=(b, seq // tr),
        in_specs=[blk, blk, blk, blk],
        out_specs=blk,
        out_shape=jax.ShapeDtypeStruct(q3.shape, BF16),
        scratch_shapes=[pltpu.VMEM((RET_HEADS, head_dim, head_dim), F32),
                        pltpu.VMEM((RET_HEADS, RET_CHUNK, RET_CHUNK), F32)],
        compiler_params=_params(("arbitrary", "arbitrary"), 32),
        name="retention",
    )(q3, k3, v3, g3)


def _even_layer(x2, batch, seq, norm_g, w_in, b_forget, log_dt, lam_re, lam_im, b_re, b_im, c_re, c_im,
                d_skip, w_glu, b_glu, w_out, mlp_g, w_up, w_down, layer, g_final, final_norm):
    d = x2.shape[1]
    heads = b_forget.shape[0]
    fw = heads * FOX_HEAD_DIM
    sw = d_skip.shape[0]
    wqkv = w_in[:, :3 * fw].astype(BF16)
    wf = jnp.pad(w_in[:, 3 * fw:3 * fw + heads], ((0, 0), (0, LANES - heads))).astype(BF16)
    wu = w_in[:, 3 * fw + heads:].astype(BF16)
    bf = jnp.pad(b_forget.astype(F32), (0, LANES - heads))[None, :]
    qscale = jnp.concatenate([jnp.full((fw,), LOG2_E * FOX_HEAD_DIM ** -0.5, F32), jnp.ones((2 * fw,), F32)])[None, :]
    qkv, u, c4 = _even_in(x2, norm_g[None, :], wqkv, qscale, wu, wf, bf, seq)
    fox = _fox(qkv.reshape(batch, seq, 3 * fw), c4, heads)

    a_re, a_im, bb_re, bb_im = _s5_params(log_dt, lam_re, lam_im, b_re, b_im)
    groups, state, group = b_re.shape
    tile_b = lambda t: jnp.tile(t.transpose(0, 2, 1).reshape(groups * group, state), (1, LANES // state))
    tile_c = lambda t: jnp.tile(t.transpose(0, 2, 1).reshape(groups * state, group), (1, LANES // group))
    s5 = _s5(u.reshape(batch, seq, sw), tile_b(bb_re), tile_b(bb_im), tile_c(c_re), tile_c(c_im), a_re, a_im,
             d_skip[None, :], w_glu.astype(BF16), b_glu[None, :], group)

    w_out = w_out.astype(BF16)
    return _out_mlp(fox.reshape(batch * seq, fw), s5.reshape(batch * seq, sw), x2, w_out[:fw], w_out[fw:],
                    mlp_g[None, :], w_up, w_down, layer, g_final[None, :], final_norm)


def _odd_layer(x2, batch, seq, norm_g, w_in, conv_w, w_out, mlp_g, w_up, w_down, layer, g_final, final_norm):
    d = x2.shape[1]
    cw = conv_w.shape[1]
    rw = d - cw
    head_dim = rw // RET_HEADS
    perm = jnp.concatenate([jnp.arange(0, head_dim, 2), jnp.arange(1, head_dim, 2)])
    perm = (jnp.arange(RET_HEADS)[:, None] * head_dim + perm[None, :]).reshape(-1)
    w_in = w_in.astype(BF16)
    wconv = w_in[:, :3 * cw]
    wq = w_in[:, 3 * cw:3 * cw + rw][:, perm]
    wk = w_in[:, 3 * cw + rw:3 * cw + 2 * rw][:, perm]
    wv = w_in[:, 3 * cw + 2 * rw:3 * cw + 3 * rw]
    wg = w_in[:, 3 * cw + 3 * rw:]
    cos, sin = _rope_tables(seq, head_dim)
    conv, q, k, v, gate = _odd_in(x2, norm_g[None, :], wconv, wq, wk, wv, wg, conv_w.astype(F32), cos, sin, seq)
    to3 = lambda t: t.reshape(batch, seq, rw)
    ret = _retention(to3(q), to3(k), to3(v), to3(gate))
    w_out = w_out.astype(BF16)
    return _out_mlp(conv, ret.reshape(batch * seq, rw), x2, w_out[:cw], w_out[cw:],
                    mlp_g[None, :], w_up, w_down, layer, g_final[None, :], final_norm)


def kernel(x, even_norm_mix, even_w_in, even_b_forget, even_s5_log_dt, even_s5_lambda_re, even_s5_lambda_im, even_s5_b_re, even_s5_b_im, even_s5_c_re, even_s5_c_im, even_s5_d, even_s5_w_glu, even_s5_b_glu, even_w_out, odd_norm_mix, odd_w_in, odd_conv_w, odd_w_out, mlp_norm, mlp_w_up, mlp_w_down, final_norm):
    batch, seq, d = x.shape
    depth = mlp_norm.shape[0]
    x2 = x.reshape(batch * seq, d)
    for layer in range(depth):
        j = layer // 2
        last = layer == depth - 1
        if layer % 2 == 0:
            x2 = _even_layer(x2, batch, seq, even_norm_mix[j], even_w_in[j], even_b_forget[j],
                             even_s5_log_dt[j], even_s5_lambda_re[j], even_s5_lambda_im[j],
                             even_s5_b_re[j], even_s5_b_im[j], even_s5_c_re[j], even_s5_c_im[j],
                             even_s5_d[j], even_s5_w_glu[j], even_s5_b_glu[j], even_w_out[j],
                             mlp_norm[layer], mlp_w_up, mlp_w_down, layer, final_norm, last)
        else:
            x2 = _odd_layer(x2, batch, seq, odd_norm_mix[j], odd_w_in[j], odd_conv_w[j], odd_w_out[j],
                            mlp_norm[layer], mlp_w_up, mlp_w_down, layer, final_norm, last)
    return x2.reshape(batch, seq, d)
```

```python
import functools
import math

import jax
import jax.numpy as jnp
from jax import lax
from jax.experimental import pallas as pl
from jax.experimental.pallas import tpu as pltpu

F32 = jnp.float32
BF16 = jnp.bfloat16
EPS = 1e-6

LANES = 128
SUBLANES = 8
FOX_HEAD_DIM = 64
RET_HEADS = 4
RET_CHUNK = 256
ROW_BLOCK = 512
IN_ROW_BLOCK = 1024
S5_TIME_BLOCK = 64
RET_ROW_BLOCK = 1024
MLP_FF_BLOCK = 1024
MLP_ROW_BLOCK = 1024
MLP_STAGE_ROWS = 512
NORM_PARTS = 4
MIB = 1024 * 1024
LOG2_E = math.log2(math.e)
FOX_PAIRS_PER_STEP = 4


def _params(semantics, vmem_mib):
    return pltpu.CompilerParams(dimension_semantics=semantics, vmem_limit_bytes=vmem_mib * MIB)


def _resident(shape):
    return pl.BlockSpec(shape, lambda *_: (0,) * len(shape), pipeline_mode=pl.Buffered(1))


def _rms_norm(x, g):
    return x * lax.rsqrt(jnp.mean(x * x, axis=-1, keepdims=True) + EPS) * g


def _sigmoid(x):
    return 1.0 / (1.0 + jnp.exp(-x))


def _dot(a, b):
    return jnp.dot(a, b, preferred_element_type=F32)


def _dot_nt(a, b):
    return lax.dot_general(a, b, (((1,), (1,)), ((), ())), preferred_element_type=F32)


def _dot_tn(a, b):
    return lax.dot_general(a, b, (((0,), (0,)), ((), ())), preferred_element_type=F32)


def _even_in_kernel(x_ref, g_ref, wqkv_ref, qscale_ref, wu_ref, wf_ref, bf_ref,
                    qkv_ref, u_ref, c_ref, carry_ref, h_scr, *, blocks_per_seq):
    i = pl.program_id(0)

    @pl.when(i % blocks_per_seq == 0)
    def _():
        carry_ref[...] = jnp.zeros_like(carry_ref)

    qrows = x_ref.shape[0] // NORM_PARTS

    def normalise(p):
        rows = slice(p * qrows, (p + 1) * qrows)
        h_scr[rows, :] = _rms_norm(x_ref[rows, :], g_ref[...]).astype(BF16)

    def project(p):
        rows = slice(p * qrows, (p + 1) * qrows)
        h = h_scr[rows, :]
        qkv_ref[rows, :] = (_dot(h, wqkv_ref[...]) * qscale_ref[...]).astype(BF16)
        u_ref[rows, :] = _dot(h, wu_ref[...])
        return _dot(h, wf_ref[...]) + bf_ref[...]

    normalise(0)
    fls = []
    for p in range(NORM_PARTS):
        if p + 1 < NORM_PARTS:
            normalise(p + 1)
        fls.append(project(p))
    fl = jnp.concatenate(fls, axis=0)
    logf = jnp.minimum(fl, 0.0) - jnp.log(1.0 + jnp.exp(-jnp.abs(fl)))
    x = logf.T[:SUBLANES, :]
    tm = x.shape[1]
    lane = lax.broadcasted_iota(jnp.int32, x.shape, 1)
    shift = 1
    while shift < tm:
        x = x + jnp.where(lane >= shift, pltpu.roll(x, shift, 1), 0.0)
        shift *= 2
    cs = x + jnp.concatenate([carry_ref[...]] * (tm // LANES), axis=1)
    carry_ref[...] = jnp.broadcast_to(cs[:, tm - 1:], carry_ref.shape)
    cb = c_ref.shape[3]
    for j in range(tm // cb):
        c_ref[0, j] = cs[:, j * cb:(j + 1) * cb]


def _even_in(x2, g, wqkv, qscale, wu, wf, bf, seq):
    n, d = x2.shape
    tm = IN_ROW_BLOCK
    nb = seq // tm
    cb = ROW_BLOCK
    row = lambda i: (i, 0)
    return pl.pallas_call(
        functools.partial(_even_in_kernel, blocks_per_seq=nb),
        grid=(n // tm,),
        in_specs=[
            pl.BlockSpec((tm, d), row),
            _resident((1, d)),
            _resident(wqkv.shape),
            _resident((1, wqkv.shape[1])),
            _resident(wu.shape),
            _resident(wf.shape),
            _resident((1, LANES)),
        ],
        out_specs=[
            pl.BlockSpec((tm, wqkv.shape[1]), row),
            pl.BlockSpec((tm, wu.shape[1]), row),
            pl.BlockSpec((1, tm // cb, SUBLANES, cb), lambda i: (i // nb, i % nb, 0, 0)),
        ],
        out_shape=[
            jax.ShapeDtypeStruct((n, wqkv.shape[1]), BF16),
            jax.ShapeDtypeStruct((n, wu.shape[1]), F32),
            jax.ShapeDtypeStruct((n // seq, seq // cb, SUBLANES, cb), F32),
        ],
        scratch_shapes=[pltpu.VMEM((SUBLANES, LANES), F32), pltpu.VMEM((tm, d), BF16)],
        compiler_params=_params(("arbitrary",), 40),
        name="even_in",
    )(x2, g, wqkv, qscale, wu, wf, bf)


def _fox_kernel(q_ref, k_ref, v_ref, c_ref, o_ref, m_scr, acc_scr, qm_scr, mask_scr, *, blk, pairs):
    grp = pl.program_id(1)
    qi = pl.program_id(2)
    lane = lax.broadcasted_iota(jnp.int32, (1, LANES), 1)
    in_head = [(lane >= FOX_HEAD_DIM * hh) & (lane < FOX_HEAD_DIM * (hh + 1)) for hh in range(2)]
    for pr in range(pairs):
        for hh in range(2):
            qm_scr[2 * pr + hh] = jnp.where(in_head[hh], q_ref[0, :, pr * LANES:(pr + 1) * LANES], 0)

    first = lane < FOX_HEAD_DIM
    ones = [jnp.where(in_head[hh], 1.0, 0.0).astype(BF16) for hh in range(2)]
    half = blk // 2

    @pl.when((pl.program_id(0) == 0) & (grp == 0) & (qi == 0))
    def _():
        row = lax.broadcasted_iota(jnp.int32, (half, half), 0)
        col = lax.broadcasted_iota(jnp.int32, (half, half), 1)
        mask_scr[...] = jnp.where(row >= col, 0.0, -jnp.inf)

    def block(j, r0, nr, nk, diagonal):
        k0 = pl.multiple_of(j * blk, blk)
        rows = slice(r0, r0 + nr)
        for pr in range(pairs):
            k = k_ref[0, pl.ds(k0, nk), pr * LANES:(pr + 1) * LANES]
            v = v_ref[0, pl.ds(k0, nk), pr * LANES:(pr + 1) * LANES]
            probs, alphas, weights = [], [], []
            for hh in range(2):
                slot = 2 * pr + hh
                h = 2 * (grp * pairs + pr) + hh
                c_q = c_ref[0, qi, pl.ds(h, 1), :][:, 0:1]
                c_k = c_ref[0, j, pl.ds(h, 1), pl.ds(0, nk)]
                s = _dot_nt(qm_scr[slot, rows, :], k) + (c_q - c_k) * LOG2_E
                if diagonal:
                    tail = s[:, nk - half:] + mask_scr[...]
                    s = tail if nk == half else jnp.concatenate([s[:, :nk - half], tail], axis=1)
                    m_new = jnp.broadcast_to(jnp.max(s, axis=-1, keepdims=True), (nr, LANES))
                else:
                    m_prev = m_scr[slot, rows, :]
                    m_new = jnp.maximum(m_prev, jnp.max(s, axis=-1, keepdims=True))
                    alphas.append(jnp.exp2(m_prev - m_new))
                probs.append(jnp.exp2((s - jnp.concatenate([m_new] * (nk // LANES), axis=1)).astype(BF16)))
                m_scr[slot, rows, :] = m_new
                weights.append(jnp.concatenate(
                    [jnp.where(in_head[hh], v, 0), jnp.broadcast_to(ones[hh], (nk, LANES))], axis=1))
            pv = _dot(jnp.concatenate(probs, axis=1), jnp.concatenate(weights, axis=0))
            if diagonal:
                acc_scr[pr, rows, :] = pv
            else:
                alpha = jnp.where(first, alphas[0], alphas[1])
                acc_scr[pr, rows, :] = jnp.concatenate([alpha, alpha], axis=1) * acc_scr[pr, rows, :] + pv

    block(qi, 0, half, half, True)
    block(qi, half, half, blk, True)

    def off_diagonal_pair(jj, carry):
        block(2 * jj, 0, blk, blk, False)
        block(2 * jj + 1, 0, blk, blk, False)
        return carry

    lax.fori_loop(0, qi // 2, off_diagonal_pair, 0)

    @pl.when(qi % 2 == 1)
    def _():
        block(qi - 1, 0, blk, blk, False)

    for pr in range(pairs):
        acc = acc_scr[pr]
        o_ref[0, :, pr * LANES:(pr + 1) * LANES] = (acc[:, :LANES] / acc[:, LANES:]).astype(o_ref.dtype)


def _fox(qkv3, c4, heads):
    b, seq, _ = qkv3.shape
    blk = ROW_BLOCK
    pairs = FOX_PAIRS_PER_STEP
    width = pairs * LANES
    groups = heads * FOX_HEAD_DIM // width
    return pl.pallas_call(
        functools.partial(_fox_kernel, blk=blk, pairs=pairs),
        grid=(b, groups, seq // blk),
        in_specs=[
            pl.BlockSpec((1, blk, width), lambda bi, g, qi: (bi, qi, g)),
            pl.BlockSpec((1, seq, width), lambda bi, g, qi: (bi, 0, groups + g)),
            pl.BlockSpec((1, seq, width), lambda bi, g, qi: (bi, 0, 2 * groups + g)),
            pl.BlockSpec((1,) + c4.shape[1:], lambda bi, g, qi: (bi, 0, 0, 0)),
        ],
        out_specs=pl.BlockSpec((1, blk, width), lambda bi, g, qi: (bi, qi, g)),
        out_shape=jax.ShapeDtypeStruct((b, seq, groups * width), BF16),
        scratch_shapes=[
            pltpu.VMEM((2 * pairs, blk, LANES), F32),
            pltpu.VMEM((pairs, blk, 2 * LANES), F32),
            pltpu.VMEM((2 * pairs, blk, LANES), BF16),
            pltpu.VMEM((blk // 2, blk // 2), F32),
        ],
        compiler_params=_params(("arbitrary", "arbitrary", "arbitrary"), 48),
        name="fox_attention",
    )(qkv3, qkv3, qkv3, c4)


def _s5_param_kernel(log_dt_ref, lr_ref, li_ref, br_ref, bi_ref, ar_ref, ai_ref, bbr_ref, bbi_ref):
    dt = jnp.exp(log_dt_ref[...])
    lr = lr_ref[...]
    li = li_ref[...]
    mag = jnp.exp(lr * dt)
    a_re = mag * jnp.cos(li * dt)
    a_im = mag * jnp.sin(li * dt)
    den = lr * lr + li * li
    n_re = a_re - 1.0
    coef_re = (n_re * lr + a_im * li) / den
    coef_im = (a_im * lr - n_re * li) / den
    br = br_ref[...]
    bi = bi_ref[...]
    ar_ref[...] = a_re
    ai_ref[...] = a_im
    bbr_ref[...] = coef_re * br - coef_im * bi
    bbi_ref[...] = coef_re * bi + coef_im * br


def _s5_params(log_dt, lam_re, lam_im, b_re, b_im):
    groups, state, width = b_re.shape
    rep = lambda t: jnp.repeat(t, width, axis=1)
    flat = (groups, state * width)
    out = jax.ShapeDtypeStruct(flat, F32)
    a_re, a_im, bb_re, bb_im = pl.pallas_call(
        _s5_param_kernel,
        out_shape=[out, out, out, out],
        name="s5_discretise",
    )(rep(jnp.broadcast_to(log_dt[:, None], (groups, state))), rep(lam_re), rep(lam_im),
      b_re.reshape(flat), b_im.reshape(flat))
    unrep = lambda t: t.reshape(groups, state, width)[:, :, 0].reshape(1, groups * state)
    return unrep(a_re), unrep(a_im), bb_re.reshape(b_re.shape), bb_im.reshape(b_re.shape)


def _s5_kernel(u_ref, bbt_re_ref, bbt_im_ref, ct_re_ref, ct_im_ref, are_ref, aim_ref, d_ref, wglu_ref, bglu_ref,
               o_ref, bmat_scr, cre_scr, cim_scr, st_scr, ub_scr, ut_scr, v_scr, yt_scr,
               *, tl, batch, width, half, state, group):
    ci = pl.program_id(0)
    tile = 2 * LANES

    @pl.when(ci == 0)
    def _():
        st_scr[...] = jnp.zeros_like(st_scr)
        bmat_scr[...] = jnp.zeros_like(bmat_scr)
        cre_scr[...] = jnp.zeros_like(cre_scr)
        cim_scr[...] = jnp.zeros_like(cim_scr)
        rb = LANES // state * group
        r = lax.broadcasted_iota(jnp.int32, (rb, LANES), 0)
        l = lax.broadcasted_iota(jnp.int32, (rb, LANES), 1)
        keep_b = (r // group) == (l // state)
        for m in range(half // LANES):
            rows = slice(m * rb, (m + 1) * rb)
            for part, src in enumerate((bbt_re_ref, bbt_im_ref)):
                cols = slice(part * half + m * LANES, part * half + (m + 1) * LANES)
                bmat_scr[rows, cols] = jnp.where(keep_b, src[rows, :], 0.0).astype(BF16)
        rc = LANES // group * state
        r = lax.broadcasted_iota(jnp.int32, (rc, LANES), 0)
        l = lax.broadcasted_iota(jnp.int32, (rc, LANES), 1)
        keep_c = (r // state) == (l // group)
        for n in range(width // LANES):
            rows = slice(n * rc, (n + 1) * rc)
            cols = slice(n * LANES, (n + 1) * LANES)
            cre_scr[rows, cols] = jnp.where(keep_c, ct_re_ref[rows, :], 0.0).astype(BF16)
            cim_scr[rows, cols] = jnp.where(keep_c, ct_im_ref[rows, :], 0.0).astype(BF16)

    planes = width // LANES
    for b in range(batch):
        for p in range(planes):
            ub_scr[p, b * tl:(b + 1) * tl, :] = u_ref[b, :, p * LANES:(p + 1) * LANES]
    for t in range(tl):
        for p in range(planes):
            ut_scr[t * batch:(t + 1) * batch, p * LANES:(p + 1) * LANES] = (
                ub_scr[p, pl.ds(t, batch, stride=tl), :])

    ut = ut_scr[...]
    utb = ut.astype(BF16)
    u_cols_per_tile = tile * width // half
    for part in range(2):
        for jt in range(half // tile):
            u0 = (jt * u_cols_per_tile) // LANES * LANES
            c0 = part * half + jt * tile
            v_scr[:, c0:c0 + tile] = _dot(utb[:, u0:u0 + LANES], bmat_scr[u0:u0 + LANES, c0:c0 + tile])

    cg_w = 4 * LANES
    for cg in range(half // cg_w):
        re_cols = slice(cg * cg_w, (cg + 1) * cg_w)
        im_cols = slice(half + cg * cg_w, half + (cg + 1) * cg_w)
        a_re = jnp.broadcast_to(are_ref[:, re_cols], (batch, cg_w))
        a_im = jnp.broadcast_to(aim_ref[:, re_cols], (batch, cg_w))
        s_re = st_scr[:, re_cols]
        s_im = st_scr[:, im_cols]
        for t in range(tl):
            rows = slice(t * batch, (t + 1) * batch)
            s_re, s_im = (a_re * s_re - a_im * s_im + v_scr[rows, re_cols],
                          a_re * s_im + a_im * s_re + v_scr[rows, im_cols])
            v_scr[rows, re_cols] = s_re
            v_scr[rows, im_cols] = s_im
        st_scr[:, re_cols] = s_re
        st_scr[:, im_cols] = s_im

    kw = half * tile // width
    for nt in range(width // tile):
        k0 = nt * kw
        y = (_dot(v_scr[:, k0:k0 + kw].astype(BF16), cre_scr[k0:k0 + kw, nt * tile:(nt + 1) * tile])
             - _dot(v_scr[:, half + k0:half + k0 + kw].astype(BF16),
                    cim_scr[k0:k0 + kw, nt * tile:(nt + 1) * tile]))
        yt_scr[:, nt * tile:(nt + 1) * tile] = y
    y = yt_scr[...] + d_ref[...] * ut
    y = 0.5 * y * (1.0 + jnp.tanh(math.sqrt(2.0 / math.pi) * (y + 0.044715 * (y * y * y))))
    y = y * _sigmoid(_dot(y.astype(BF16), wglu_ref[...]) + bglu_ref[...])
    for p in range(planes):
        ub_scr[p] = y[:, p * LANES:(p + 1) * LANES]
    for b in range(batch):
        for p in range(planes):
            o_ref[b, :, p * LANES:(p + 1) * LANES] = (
                ub_scr[p, pl.ds(b, tl, stride=batch), :].astype(o_ref.dtype))


def _s5(u3, bbt_re, bbt_im, ct_re, ct_im, a_re, a_im, d_skip, w_glu, b_glu, group):
    batch, seq, width = u3.shape
    half = a_re.shape[1]
    state = half * group // width
    tl = S5_TIME_BLOCK
    rows = tl * batch
    blk = pl.BlockSpec((batch, tl, width), lambda ci: (0, ci, 0))
    return pl.pallas_call(
        functools.partial(_s5_kernel, tl=tl, batch=batch, width=width, half=half, state=state, group=group),
        grid=(seq // tl,),
        in_specs=[blk, _resident(bbt_re.shape), _resident(bbt_im.shape), _resident(ct_re.shape),
                  _resident(ct_im.shape), _resident(a_re.shape), _resident(a_im.shape), _resident((1, width)),
                  _resident(w_glu.shape), _resident((1, width))],
        out_specs=blk,
        out_shape=jax.ShapeDtypeStruct(u3.shape, BF16),
        scratch_shapes=[
            pltpu.VMEM((width, 2 * half), BF16),
            pltpu.VMEM((half, width), BF16),
            pltpu.VMEM((half, width), BF16),
            pltpu.VMEM((batch, 2 * half), F32),
            pltpu.VMEM((width // LANES, rows, LANES), F32),
            pltpu.VMEM((rows, width), F32),
            pltpu.VMEM((rows, 2 * half), F32),
            pltpu.VMEM((rows, width), F32),
        ],
        compiler_params=_params(("arbitrary",), 40),
        name="s5_scan",
    )(u3, bbt_re, bbt_im, ct_re, ct_im, a_re, a_im, d_skip, w_glu, b_glu)


def _out_mlp_kernel(a_ref, b_ref, x_ref, wa_ref, wb_ref, g_ref, wup_hbm, wdn_hbm, gf_ref, o_ref,
                    h_scr, wup_scr, wdn_scr, stage_scr, sem, *, layer, final_norm):
    d, ff = wup_scr.shape
    rows = stage_scr.shape[1]

    @pl.when(pl.program_id(0) == 0)
    def _():
        tiles = ([(wup_hbm, wup_scr, r, c) for r in range(d // rows) for c in range(ff // d)]
                 + [(wdn_hbm, wdn_scr, r, 0) for r in range(ff // rows)])

        def tile_copy(n):
            src, _, r, c = tiles[n]
            return pltpu.make_async_copy(src.at[layer, pl.ds(r * rows, rows), pl.ds(c * d, d)],
                                         stage_scr.at[n % 2], sem.at[n % 2])

        tile_copy(0).start()
        for n, (_, dst, r, c) in enumerate(tiles):
            if n + 1 < len(tiles):
                tile_copy(n + 1).start()
            tile_copy(n).wait()
            dst[r * rows:(r + 1) * rows, c * d:(c + 1) * d] = stage_scr[n % 2].astype(BF16)

    parts = NORM_PARTS
    qrows = x_ref.shape[0] // parts

    def project(p):
        rows = slice(p * qrows, (p + 1) * qrows)
        o_ref[rows, :] = x_ref[rows, :] + _dot(a_ref[rows, :], wa_ref[...]) + _dot(b_ref[rows, :], wb_ref[...])

    def normalise(p):
        rows = slice(p * qrows, (p + 1) * qrows)
        h_scr[rows, :] = _rms_norm(o_ref[rows, :], g_ref[...]).astype(BF16)

    project(0)
    for p in range(parts):
        if p + 1 < parts:
            project(p + 1)
        normalise(p)

    def ff_chunk(c, carry):
        c0 = pl.multiple_of(c * MLP_FF_BLOCK, MLP_FF_BLOCK)
        t = jnp.maximum(_dot(h_scr[...], wup_scr[:, pl.ds(c0, MLP_FF_BLOCK)]), 0.0)
        o_ref[...] += _dot((t * t).astype(BF16), wdn_scr[pl.ds(c0, MLP_FF_BLOCK), :])
        return carry

    lax.fori_loop(0, ff // MLP_FF_BLOCK, ff_chunk, 0, unroll=2)
    if final_norm:
        o_ref[...] = _rms_norm(o_ref[...], gf_ref[...])


def _out_mlp(a, b, x2, wa, wb, g, w_up, w_down, layer, g_final, final_norm):
    n, d = x2.shape
    ff = w_up.shape[2]
    tm = MLP_ROW_BLOCK
    row = lambda i: (i, 0)
    return pl.pallas_call(
        functools.partial(_out_mlp_kernel, layer=layer, final_norm=final_norm),
        grid=(n // tm,),
        in_specs=[
            pl.BlockSpec((tm, a.shape[1]), row),
            pl.BlockSpec((tm, b.shape[1]), row),
            pl.BlockSpec((tm, d), row),
            _resident(wa.shape), _resident(wb.shape), _resident((1, d)),
            pl.BlockSpec(memory_space=pl.ANY), pl.BlockSpec(memory_space=pl.ANY), _resident((1, d)),
        ],
        out_specs=pl.BlockSpec((tm, d), row),
        out_shape=jax.ShapeDtypeStruct((n, d), F32),
        scratch_shapes=[pltpu.VMEM((tm, d), BF16), pltpu.VMEM((d, ff), BF16), pltpu.VMEM((ff, d), BF16),
                        pltpu.VMEM((2, MLP_STAGE_ROWS, d), F32), pltpu.SemaphoreType.DMA((2,))],
        compiler_params=_params(("arbitrary",), 58),
        name="out_mlp",
    )(a, b, x2, wa, wb, g, w_up, w_down, g_final)


def _rope_table_kernel(inv_ref, cos_ref, sin_ref):
    rows, lanes = cos_ref.shape
    pos = lax.broadcasted_iota(jnp.int32, (rows, lanes), 0).astype(F32)
    lane = lax.broadcasted_iota(jnp.int32, (rows, lanes), 1)
    ang = pos * inv_ref[...]
    cos_ref[...] = jnp.cos(ang)
    sin = jnp.sin(ang)
    sin_ref[...] = jnp.where(lane < lanes // 2, -sin, sin)


def _rope_tables(seq, head_dim):
    inv = 1.0 / (10000.0 ** jnp.linspace(0.0, 1.0, head_dim // 2, dtype=F32))
    inv2 = jnp.concatenate([inv, inv])[None, :]
    out = jax.ShapeDtypeStruct((seq, head_dim), F32)
    return pl.pallas_call(_rope_table_kernel, out_shape=[out, out], name="rope_tables")(inv2)


def _odd_in_kernel(x_ref, g_ref, wconv_ref, wq_ref, wk_ref, wv_ref, wg_ref, cw_ref, cos_ref, sin_ref,
                   conv_ref, q_ref, k_ref, v_ref, gate_ref, z_scr, h_scr, *, blocks_per_seq, taps, head_dim):
    i = pl.program_id(0)
    tm = x_ref.shape[0]
    cw = conv_ref.shape[1]
    pad = SUBLANES

    @pl.when(i % blocks_per_seq == 0)
    def _():
        z_scr[0:pad, :] = jnp.zeros((pad, cw), F32)

    qrows = tm // NORM_PARTS

    def normalise(p):
        rows = slice(p * qrows, (p + 1) * qrows)
        h_scr[rows, :] = _rms_norm(x_ref[rows, :], g_ref[...]).astype(BF16)

    def project(p):
        r0 = p * qrows
        rows = slice(r0, r0 + qrows)
        zrows = slice(pad + r0, pad + r0 + qrows)
        h = h_scr[rows, :]
        hc = _dot(h, wconv_ref[:, 0:cw])
        gate_b = _dot(h, wconv_ref[:, cw:2 * cw])
        gate_c = _dot(h, wconv_ref[:, 2 * cw:3 * cw])
        z_scr[zrows, :] = gate_c * hc
        conv = cw_ref[taps - 1:taps, :] * z_scr[zrows, :]
        for j in range(taps - 1):
            shift = taps - 1 - j
            conv = conv + cw_ref[j:j + 1, :] * z_scr[pad + r0 - shift:pad + r0 - shift + qrows, :]
        conv_ref[rows, :] = (gate_b * conv).astype(conv_ref.dtype)

        cos = cos_ref[rows, :]
        sin = sin_ref[rows, :]

        def rotate(w_ref, out_ref, scale):
            x = _dot(h, w_ref[...])
            for hd in range(x.shape[1] // head_dim):
                xh = x[:, hd * head_dim:(hd + 1) * head_dim]
                rot = xh * cos + pltpu.roll(xh, head_dim // 2, 1) * sin
                out_ref[rows, hd * head_dim:(hd + 1) * head_dim] = (rot * scale).astype(out_ref.dtype)

        rotate(wq_ref, q_ref, 1.0)
        rotate(wk_ref, k_ref, head_dim ** -0.5)
        v_ref[rows, :] = _dot(h, wv_ref[...]).astype(v_ref.dtype)
        gate_ref[rows, :] = _dot(h, wg_ref[...]).astype(gate_ref.dtype)

    normalise(0)
    for p in range(NORM_PARTS):
        if p + 1 < NORM_PARTS:
            normalise(p + 1)
        project(p)
    z_scr[0:pad, :] = z_scr[tm:tm + pad, :]


def _odd_in(x2, g, wconv, wq, wk, wv, wg, conv_w, cos, sin, seq):
    n, d = x2.shape
    tm = IN_ROW_BLOCK
    nb = seq // tm
    cw = conv_w.shape[1]
    rw = wq.shape[1]
    head_dim = cos.shape[1]
    row = lambda i: (i, 0)
    tab = pl.BlockSpec((tm, head_dim), lambda i: (i % nb, 0))
    o = lambda w: jax.ShapeDtypeStruct((n, w), BF16)
    return pl.pallas_call(
        functools.partial(_odd_in_kernel, blocks_per_seq=nb, taps=conv_w.shape[0], head_dim=head_dim),
        grid=(n // tm,),
        in_specs=[pl.BlockSpec((tm, d), row), _resident((1, d)), _resident(wconv.shape),
                  _resident(wq.shape), _resident(wk.shape), _resident(wv.shape), _resident(wg.shape),
                  _resident(conv_w.shape), tab, tab],
        out_specs=[pl.BlockSpec((tm, cw), row)] + [pl.BlockSpec((tm, rw), row)] * 4,
        out_shape=[o(cw), o(rw), o(rw), o(rw), o(rw)],
        scratch_shapes=[pltpu.VMEM((tm + 2 * SUBLANES, cw), F32), pltpu.VMEM((tm, d), BF16)],
        compiler_params=_params(("arbitrary",), 40),
        name="odd_in",
    )(x2, g, wconv, wq, wk, wv, wg, conv_w, cos, sin)


def _retention_kernel(q_ref, k_ref, v_ref, g_ref, o_ref, state_scr, decay_scr, *, chunk, heads):
    ri = pl.program_id(1)
    head_dim = q_ref.shape[2] // heads
    log_gamma = [math.log(1.0 - 2.0 ** (-5.0 - h)) for h in range(heads)]

    @pl.when(ri == 0)
    def _():
        state_scr[...] = jnp.zeros_like(state_scr)

    @pl.when((pl.program_id(0) == 0) & (ri == 0))
    def _():
        ridx = lax.broadcasted_iota(jnp.int32, (chunk, chunk), 0)
        cidx = lax.broadcasted_iota(jnp.int32, (chunk, chunk), 1)
        rel = (ridx - cidx).astype(F32)
        for h in range(heads):
            decay_scr[h] = jnp.where(rel >= 0, jnp.exp(log_gamma[h] * jnp.maximum(rel, 0.0)), 0.0)

    idx = lax.broadcasted_iota(jnp.int32, (chunk, 1), 0).astype(F32)
    for h in range(heads):
        query_decay = jnp.exp(log_gamma[h] * (idx + 1.0))
        key_decay = jnp.exp(log_gamma[h] * (chunk - 1.0 - idx))
        chunk_decay = math.exp(log_gamma[h] * chunk)
        cols = slice(h * head_dim, (h + 1) * head_dim)
        state = state_scr[h]
        for c in range(q_ref.shape[1] // chunk):
            rows = slice(c * chunk, (c + 1) * chunk)
            q = q_ref[0, rows, cols]
            k = k_ref[0, rows, cols]
            v = v_ref[0, rows, cols]
            s = _dot_nt(q, k) * decay_scr[h]
            ret = _dot(s.astype(BF16), v) + _dot(q, state.astype(BF16)) * query_decay
            kd = (k.astype(F32) * key_decay).astype(BF16)
            state = state * chunk_decay + _dot_tn(kd, v)
            mu = jnp.mean(ret, axis=-1, keepdims=True)
            dev = ret - mu
            var = jnp.mean(dev * dev, axis=-1, keepdims=True)
            g = g_ref[0, rows, cols].astype(F32)
            o_ref[0, rows, cols] = (g * _sigmoid(g) * (dev * lax.rsqrt(var + EPS))).astype(o_ref.dtype)
        state_scr[h] = state


def _retention(q3, k3, v3, g3):
    b, seq, rw = q3.shape
    head_dim = rw // RET_HEADS
    tr = RET_ROW_BLOCK
    blk = pl.BlockSpec((1, tr, rw), lambda bi, ri: (bi, ri, 0))
    return pl.pallas_call(
        functools.partial(_retention_kernel, chunk=RET_CHUNK, heads=RET_HEADS),
        grid=(b, seq // tr),
        in_specs=[blk, blk, blk, blk],
        out_specs=blk,
        out_shape=jax.ShapeDtypeStruct(q3.shape, BF16),
        scratch_shapes=[pltpu.VMEM((RET_HEADS, head_dim, head_dim), F32),
                        pltpu.VMEM((RET_HEADS, RET_CHUNK, RET_CHUNK), F32)],
        compiler_params=_params(("arbitrary", "arbitrary"), 32),
        name="retention",
    )(q3, k3, v3, g3)


def _even_layer(x2, batch, seq, norm_g, w_in, b_forget, log_dt, lam_re, lam_im, b_re, b_im, c_re, c_im,
                d_skip, w_glu, b_glu, w_out, mlp_g, w_up, w_down, layer, g_final, final_norm):
    d = x2.shape[1]
    heads = b_forget.shape[0]
    fw = heads * FOX_HEAD_DIM
    sw = d_skip.shape[0]
    wqkv = w_in[:, :3 * fw].astype(BF16)
    wf = jnp.pad(w_in[:, 3 * fw:3 * fw + heads], ((0, 0), (0, LANES - heads))).astype(BF16)
    wu = w_in[:, 3 * fw + heads:].astype(BF16)
    bf = jnp.pad(b_forget.astype(F32), (0, LANES - heads))[None, :]
    qscale = jnp.concatenate([jnp.full((fw,), LOG2_E * FOX_HEAD_DIM ** -0.5, F32), jnp.ones((2 * fw,), F32)])[None, :]
    qkv, u, c4 = _even_in(x2, norm_g[None, :], wqkv, qscale, wu, wf, bf, seq)
    fox = _fox(qkv.reshape(batch, seq, 3 * fw), c4, heads)

    a_re, a_im, bb_re, bb_im = _s5_params(log_dt, lam_re, lam_im, b_re, b_im)
    groups, state, group = b_re.shape
    tile_b = lambda t: jnp.tile(t.transpose(0, 2, 1).reshape(groups * group, state), (1, LANES // state))
    tile_c = lambda t: jnp.tile(t.transpose(0, 2, 1).reshape(groups * state, group), (1, LANES // group))
    s5 = _s5(u.reshape(batch, seq, sw), tile_b(bb_re), tile_b(bb_im), tile_c(c_re), tile_c(c_im), a_re, a_im,
             d_skip[None, :], w_glu.astype(BF16), b_glu[None, :], group)

    w_out = w_out.astype(BF16)
    return _out_mlp(fox.reshape(batch * seq, fw), s5.reshape(batch * seq, sw), x2, w_out[:fw], w_out[fw:],
                    mlp_g[None, :], w_up, w_down, layer, g_final[None, :], final_norm)


def _odd_layer(x2, batch, seq, norm_g, w_in, conv_w, w_out, mlp_g, w_up, w_down, layer, g_final, final_norm):
    d = x2.shape[1]
    cw = conv_w.shape[1]
    rw = d - cw
    head_dim = rw // RET_HEADS
    perm = jnp.concatenate([jnp.arange(0, head_dim, 2), jnp.arange(1, head_dim, 2)])
    perm = (jnp.arange(RET_HEADS)[:, None] * head_dim + perm[None, :]).reshape(-1)
    w_in = w_in.astype(BF16)
    wconv = w_in[:, :3 * cw]
    wq = w_in[:, 3 * cw:3 * cw + rw][:, perm]
    wk = w_in[:, 3 * cw + rw:3 * cw + 2 * rw][:, perm]
    wv = w_in[:, 3 * cw + 2 * rw:3 * cw + 3 * rw]
    wg = w_in[:, 3 * cw + 3 * rw:]
    cos, sin = _rope_tables(seq, head_dim)
    conv, q, k, v, gate = _odd_in(x2, norm_g[None, :], wconv, wq, wk, wv, wg, conv_w.astype(F32), cos, sin, seq)
    to3 = lambda t: t.reshape(batch, seq, rw)
    ret = _retention(to3(q), to3(k), to3(v), to3(gate))
    w_out = w_out.astype(BF16)
    return _out_mlp(conv, ret.reshape(batch * seq, rw), x2, w_out[:cw], w_out[cw:],
                    mlp_g[None, :], w_up, w_down, layer, g_final[None, :], final_norm)


def kernel(x, even_norm_mix, even_w_in, even_b_forget, even_s5_log_dt, even_s5_lambda_re, even_s5_lambda_im, even_s5_b_re, even_s5_b_im, even_s5_c_re, even_s5_c_im, even_s5_d, even_s5_w_glu, even_s5_b_glu, even_w_out, odd_norm_mix, odd_w_in, odd_conv_w, odd_w_out, mlp_norm, mlp_w_up, mlp_w_down, final_norm):
    batch, seq, d = x.shape
    depth = mlp_norm.shape[0]
    x2 = x.reshape(batch * seq, d)
    for layer in range(depth):
        j = layer // 2
        last = layer == depth - 1
        if layer % 2 == 0:
            x2 = _even_layer(x2, batch, seq, even_norm_mix[j], even_w_in[j], even_b_forget[j],
                             even_s5_log_dt[j], even_s5_lambda_re[j], even_s5_lambda_im[j],
                             even_s5_b_re[j], even_s5_b_im[j], even_s5_c_re[j], even_s5_c_im[j],
                             even_s5_d[j], even_s5_w_glu[j], even_s5_b_glu[j], even_w_out[j],
                             mlp_norm[layer], mlp_w_up, mlp_w_down, layer, final_norm, last)
        else:
            x2 = _odd_layer(x2, batch, seq, odd_norm_mix[j], odd_w_in[j], odd_conv_w[j], odd_w_out[j],
                            mlp_norm[layer], mlp_w_up, mlp_w_down, layer, final_norm, last)
    return x2.reshape(batch, seq, d)
```

```python
import functools
import math

import jax
import jax.numpy as jnp
from jax import lax
from jax.experimental import pallas as pl
from jax.experimental.pallas import tpu as pltpu

F32 = jnp.float32
BF16 = jnp.bfloat16
EPS = 1e-6

LANES = 128
SUBLANES = 8
FOX_HEAD_DIM = 64
RET_HEADS = 4
RET_CHUNK = 256
ROW_BLOCK = 512
IN_ROW_BLOCK = 1024
S5_TIME_BLOCK = 64
RET_ROW_BLOCK = 1024
MLP_FF_BLOCK = 1024
MLP_ROW_BLOCK = 1024
MLP_STAGE_ROWS = 512
NORM_PARTS = 4
MIB = 1024 * 1024
LOG2_E = math.log2(math.e)
FOX_PAIRS_PER_STEP = 4


def _params(semantics, vmem_mib):
    return pltpu.CompilerParams(dimension_semantics=semantics, vmem_limit_bytes=vmem_mib * MIB)


def _resident(shape):
    return pl.BlockSpec(shape, lambda *_: (0,) * len(shape), pipeline_mode=pl.Buffered(1))


def _rms_norm(x, g):
    return x * lax.rsqrt(jnp.mean(x * x, axis=-1, keepdims=True) + EPS) * g


def _sigmoid(x):
    return 1.0 / (1.0 + jnp.exp(-x))


def _dot(a, b):
    return jnp.dot(a, b, preferred_element_type=F32)


def _dot_nt(a, b):
    return lax.dot_general(a, b, (((1,), (1,)), ((), ())), preferred_element_type=F32)


def _dot_tn(a, b):
    return lax.dot_general(a, b, (((0,), (0,)), ((), ())), preferred_element_type=F32)


def _even_in_kernel(x_ref, g_ref, wqkv_ref, qscale_ref, wu_ref, wf_ref, bf_ref,
                    qkv_ref, u_ref, c_ref, carry_ref, h_scr, *, blocks_per_seq):
    i = pl.program_id(0)

    @pl.when(i % blocks_per_seq == 0)
    def _():
        carry_ref[...] = jnp.zeros_like(carry_ref)

    qrows = x_ref.shape[0] // NORM_PARTS

    def normalise(p):
        rows = slice(p * qrows, (p + 1) * qrows)
        h_scr[rows, :] = _rms_norm(x_ref[rows, :], g_ref[...]).astype(BF16)

    def project(p):
        rows = slice(p * qrows, (p + 1) * qrows)
        h = h_scr[rows, :]
        qkv_ref[rows, :] = (_dot(h, wqkv_ref[...]) * qscale_ref[...]).astype(BF16)
        u_ref[rows, :] = _dot(h, wu_ref[...])
        return _dot(h, wf_ref[...]) + bf_ref[...]

    normalise(0)
    fls = []
    for p in range(NORM_PARTS):
        if p + 1 < NORM_PARTS:
            normalise(p + 1)
        fls.append(project(p))
    fl = jnp.concatenate(fls, axis=0)
    logf = jnp.minimum(fl, 0.0) - jnp.log(1.0 + jnp.exp(-jnp.abs(fl)))
    x = logf.T[:SUBLANES, :]
    tm = x.shape[1]
    lane = lax.broadcasted_iota(jnp.int32, x.shape, 1)
    shift = 1
    while shift < tm:
        x = x + jnp.where(lane >= shift, pltpu.roll(x, shift, 1), 0.0)
        shift *= 2
    cs = x + jnp.concatenate([carry_ref[...]] * (tm // LANES), axis=1)
    carry_ref[...] = jnp.broadcast_to(cs[:, tm - 1:], carry_ref.shape)
    cb = c_ref.shape[3]
    for j in range(tm // cb):
        c_ref[0, j] = cs[:, j * cb:(j + 1) * cb]


def _even_in(x2, g, wqkv, qscale, wu, wf, bf, seq):
    n, d = x2.shape
    tm = IN_ROW_BLOCK
    nb = seq // tm
    cb = ROW_BLOCK
    row = lambda i: (i, 0)
    return pl.pallas_call(
        functools.partial(_even_in_kernel, blocks_per_seq=nb),
        grid=(n // tm,),
        in_specs=[
            pl.BlockSpec((tm, d), row),
            _resident((1, d)),
            _resident(wqkv.shape),
            _resident((1, wqkv.shape[1])),
            _resident(wu.shape),
            _resident(wf.shape),
            _resident((1, LANES)),
        ],
        out_specs=[
            pl.BlockSpec((tm, wqkv.shape[1]), row),
            pl.BlockSpec((tm, wu.shape[1]), row),
            pl.BlockSpec((1, tm // cb, SUBLANES, cb), lambda i: (i // nb, i % nb, 0, 0)),
        ],
        out_shape=[
            jax.ShapeDtypeStruct((n, wqkv.shape[1]), BF16),
            jax.ShapeDtypeStruct((n, wu.shape[1]), F32),
            jax.ShapeDtypeStruct((n // seq, seq // cb, SUBLANES, cb), F32),
        ],
        scratch_shapes=[pltpu.VMEM((SUBLANES, LANES), F32), pltpu.VMEM((tm, d), BF16)],
        compiler_params=_params(("arbitrary",), 40),
        name="even_in",
    )(x2, g, wqkv, qscale, wu, wf, bf)


def _fox_kernel(q_ref, k_ref, v_ref, c_ref, o_ref, m_scr, acc_scr, qm_scr, mask_scr, *, blk, pairs):
    grp = pl.program_id(1)
    qi = pl.program_id(2)
    lane = lax.broadcasted_iota(jnp.int32, (1, LANES), 1)
    in_head = [(lane >= FOX_HEAD_DIM * hh) & (lane < FOX_HEAD_DIM * (hh + 1)) for hh in range(2)]
    for pr in range(pairs):
        for hh in range(2):
            qm_scr[2 * pr + hh] = jnp.where(in_head[hh], q_ref[0, :, pr * LANES:(pr + 1) * LANES], 0)

    first = lane < FOX_HEAD_DIM
    ones = [jnp.where(in_head[hh], 1.0, 0.0).astype(BF16) for hh in range(2)]
    half = blk // 2

    @pl.when((pl.program_id(0) == 0) & (grp == 0) & (qi == 0))
    def _():
        row = lax.broadcasted_iota(jnp.int32, (half, half), 0)
        col = lax.broadcasted_iota(jnp.int32, (half, half), 1)
        mask_scr[...] = jnp.where(row >= col, 0.0, -jnp.inf)

    def block(j, r0, nr, nk, diagonal):
        k0 = pl.multiple_of(j * blk, blk)
        rows = slice(r0, r0 + nr)
        for pr in range(pairs):
            k = k_ref[0, pl.ds(k0, nk), pr * LANES:(pr + 1) * LANES]
            v = v_ref[0, pl.ds(k0, nk), pr * LANES:(pr + 1) * LANES]
            probs, alphas, weights = [], [], []
            for hh in range(2):
                slot = 2 * pr + hh
                h = 2 * (grp * pairs + pr) + hh
                c_q = c_ref[0, qi, pl.ds(h, 1), :][:, 0:1]
                c_k = c_ref[0, j, pl.ds(h, 1), pl.ds(0, nk)]
                s = _dot_nt(qm_scr[slot, rows, :], k) + (c_q - c_k) * LOG2_E
                if diagonal:
                    tail = s[:, nk - half:] + mask_scr[...]
                    s = tail if nk == half else jnp.concatenate([s[:, :nk - half], tail], axis=1)
                    m_new = jnp.broadcast_to(jnp.max(s, axis=-1, keepdims=True), (nr, LANES))
                else:
                    m_prev = m_scr[slot, rows, :]
                    m_new = jnp.maximum(m_prev, jnp.max(s, axis=-1, keepdims=True))
                    alphas.append(jnp.exp2(m_prev - m_new))
                probs.append(jnp.exp2((s - jnp.concatenate([m_new] * (nk // LANES), axis=1)).astype(BF16)))
                m_scr[slot, rows, :] = m_new
                weights.append(jnp.concatenate(
                    [jnp.where(in_head[hh], v, 0), jnp.broadcast_to(ones[hh], (nk, LANES))], axis=1))
            pv = _dot(jnp.concatenate(probs, axis=1), jnp.concatenate(weights, axis=0))
            if diagonal:
                acc_scr[pr, rows, :] = pv
            else:
                alpha = jnp.where(first, alphas[0], alphas[1])
                acc_scr[pr, rows, :] = jnp.concatenate([alpha, alpha], axis=1) * acc_scr[pr, rows, :] + pv

    block(qi, 0, half, half, True)
    block(qi, half, half, blk, True)

    def off_diagonal_pair(jj, carry):
        block(2 * jj, 0, blk, blk, False)
        block(2 * jj + 1, 0, blk, blk, False)
        return carry

    lax.fori_loop(0, qi // 2, off_diagonal_pair, 0)

    @pl.when(qi % 2 == 1)
    def _():
        block(qi - 1, 0, blk, blk, False)

    for pr in range(pairs):
        acc = acc_scr[pr]
        o_ref[0, :, pr * LANES:(pr + 1) * LANES] = (acc[:, :LANES] / acc[:, LANES:]).astype(o_ref.dtype)


def _fox(qkv3, c4, heads):
    b, seq, _ = qkv3.shape
    blk = ROW_BLOCK
    pairs = FOX_PAIRS_PER_STEP
    width = pairs * LANES
    groups = heads * FOX_HEAD_DIM // width
    return pl.pallas_call(
        functools.partial(_fox_kernel, blk=blk, pairs=pairs),
        grid=(b, groups, seq // blk),
        in_specs=[
            pl.BlockSpec((1, blk, width), lambda bi, g, qi: (bi, qi, g)),
            pl.BlockSpec((1, seq, width), lambda bi, g, qi: (bi, 0, groups + g)),
            pl.BlockSpec((1, seq, width), lambda bi, g, qi: (bi, 0, 2 * groups + g)),
            pl.BlockSpec((1,) + c4.shape[1:], lambda bi, g, qi: (bi, 0, 0, 0)),
        ],
        out_specs=pl.BlockSpec((1, blk, width), lambda bi, g, qi: (bi, qi, g)),
        out_shape=jax.ShapeDtypeStruct((b, seq, groups * width), BF16),
        scratch_shapes=[
            pltpu.VMEM((2 * pairs, blk, LANES), F32),
            pltpu.VMEM((pairs, blk, 2 * LANES), F32),
            pltpu.VMEM((2 * pairs, blk, LANES), BF16),
            pltpu.VMEM((blk // 2, blk // 2), F32),
        ],
        compiler_params=_params(("arbitrary", "arbitrary", "arbitrary"), 48),
        name="fox_attention",
    )(qkv3, qkv3, qkv3, c4)


def _s5_param_kernel(log_dt_ref, lr_ref, li_ref, br_ref, bi_ref, ar_ref, ai_ref, bbr_ref, bbi_ref):
    dt = jnp.exp(log_dt_ref[...])
    lr = lr_ref[...]
    li = li_ref[...]
    mag = jnp.exp(lr * dt)
    a_re = mag * jnp.cos(li * dt)
    a_im = mag * jnp.sin(li * dt)
    den = lr * lr + li * li
    n_re = a_re - 1.0
    coef_re = (n_re * lr + a_im * li) / den
    coef_im = (a_im * lr - n_re * li) / den
    br = br_ref[...]
    bi = bi_ref[...]
    ar_ref[...] = a_re
    ai_ref[...] = a_im
    bbr_ref[...] = coef_re * br - coef_im * bi
    bbi_ref[...] = coef_re * bi + coef_im * br


def _s5_params(log_dt, lam_re, lam_im, b_re, b_im):
    groups, state, width = b_re.shape
    rep = lambda t: jnp.repeat(t, width, axis=1)
    flat = (groups, state * width)
    out = jax.ShapeDtypeStruct(flat, F32)
    a_re, a_im, bb_re, bb_im = pl.pallas_call(
        _s5_param_kernel,
        out_shape=[out, out, out, out],
        name="s5_discretise",
    )(rep(jnp.broadcast_to(log_dt[:, None], (groups, state))), rep(lam_re), rep(lam_im),
      b_re.reshape(flat), b_im.reshape(flat))
    unrep = lambda t: t.reshape(groups, state, width)[:, :, 0].reshape(1, groups * state)
    return unrep(a_re), unrep(a_im), bb_re.reshape(b_re.shape), bb_im.reshape(b_re.shape)


def _s5_kernel(u_ref, bbt_re_ref, bbt_im_ref, ct_re_ref, ct_im_ref, are_ref, aim_ref, d_ref, wglu_ref, bglu_ref,
               o_ref, bmat_scr, cre_scr, cim_scr, st_scr, ub_scr, ut_scr, v_scr, yt_scr,
               *, tl, batch, width, half, state, group):
    ci = pl.program_id(0)
    tile = 2 * LANES

    @pl.when(ci == 0)
    def _():
        st_scr[...] = jnp.zeros_like(st_scr)
        bmat_scr[...] = jnp.zeros_like(bmat_scr)
        cre_scr[...] = jnp.zeros_like(cre_scr)
        cim_scr[...] = jnp.zeros_like(cim_scr)
        rb = LANES // state * group
        r = lax.broadcasted_iota(jnp.int32, (rb, LANES), 0)
        l = lax.broadcasted_iota(jnp.int32, (rb, LANES), 1)
        keep_b = (r // group) == (l // state)
        for m in range(half // LANES):
            rows = slice(m * rb, (m + 1) * rb)
            for part, src in enumerate((bbt_re_ref, bbt_im_ref)):
                cols = slice(part * half + m * LANES, part * half + (m + 1) * LANES)
                bmat_scr[rows, cols] = jnp.where(keep_b, src[rows, :], 0.0).astype(BF16)
        rc = LANES // group * state
        r = lax.broadcasted_iota(jnp.int32, (rc, LANES), 0)
        l = lax.broadcasted_iota(jnp.int32, (rc, LANES), 1)
        keep_c = (r // state) == (l // group)
        for n in range(width // LANES):
            rows = slice(n * rc, (n + 1) * rc)
            cols = slice(n * LANES, (n + 1) * LANES)
            cre_scr[rows, cols] = jnp.where(keep_c, ct_re_ref[rows, :], 0.0).astype(BF16)
            cim_scr[rows, cols] = jnp.where(keep_c, ct_im_ref[rows, :], 0.0).astype(BF16)

    planes = width // LANES
    for b in range(batch):
        for p in range(planes):
            ub_scr[p, b * tl:(b + 1) * tl, :] = u_ref[b, :, p * LANES:(p + 1) * LANES]
    for t in range(tl):
        for p in range(planes):
            ut_scr[t * batch:(t + 1) * batch, p * LANES:(p + 1) * LANES] = (
                ub_scr[p, pl.ds(t, batch, stride=tl), :])

    ut = ut_scr[...]
    utb = ut.astype(BF16)
    u_cols_per_tile = tile * width // half
    cg_w = tile
    n_cg = half // cg_w
    oc = LANES
    cg_per_out = oc * half // width // cg_w

    def input_matmul(cg):
        for part in range(2):
            u0 = (cg * u_cols_per_tile) // LANES * LANES
            c0 = part * half + cg * tile
            v_scr[:, c0:c0 + tile] = _dot(utb[:, u0:u0 + LANES], bmat_scr[u0:u0 + LANES, c0:c0 + tile])

    def scan(cg):
        re_cols = slice(cg * cg_w, (cg + 1) * cg_w)
        im_cols = slice(half + cg * cg_w, half + (cg + 1) * cg_w)
        a_re = jnp.broadcast_to(are_ref[:, re_cols], (batch, cg_w))
        a_im = jnp.broadcast_to(aim_ref[:, re_cols], (batch, cg_w))
        s_re = st_scr[:, re_cols]
        s_im = st_scr[:, im_cols]
        for t in range(tl):
            rows = slice(t * batch, (t + 1) * batch)
            s_re, s_im = (a_re * s_re - a_im * s_im + v_scr[rows, re_cols],
                          a_re * s_im + a_im * s_re + v_scr[rows, im_cols])
            v_scr[rows, re_cols] = s_re
            v_scr[rows, im_cols] = s_im
        st_scr[:, re_cols] = s_re
        st_scr[:, im_cols] = s_im

    def output_matmul(n):
        k0 = n * cg_per_out * cg_w
        kw = cg_per_out * cg_w
        o0 = n * oc
        yt_scr[:, o0:o0 + oc] = (
            _dot(v_scr[:, k0:k0 + kw].astype(BF16), cre_scr[k0:k0 + kw, o0:o0 + oc])
            - _dot(v_scr[:, half + k0:half + k0 + kw].astype(BF16), cim_scr[k0:k0 + kw, o0:o0 + oc]))

    input_matmul(0)
    for cg in range(n_cg):
        if cg + 1 < n_cg:
            input_matmul(cg + 1)
        scan(cg)
        if cg > 0 and cg % cg_per_out == 0:
            output_matmul(cg // cg_per_out - 1)
    output_matmul(n_cg // cg_per_out - 1)
    qrows = ut_scr.shape[0] // NORM_PARTS

    def gelu(q):
        rows = slice(q * qrows, (q + 1) * qrows)
        y = yt_scr[rows, :] + d_ref[...] * ut_scr[rows, :]
        yt_scr[rows, :] = 0.5 * y * (1.0 + jnp.tanh(math.sqrt(2.0 / math.pi) * (y + 0.044715 * (y * y * y))))

    def glu(q):
        rows = slice(q * qrows, (q + 1) * qrows)
        y = yt_scr[rows, :]
        y = y * _sigmoid(_dot(y.astype(BF16), wglu_ref[...]) + bglu_ref[...])
        for p in range(planes):
            ub_scr[p, rows, :] = y[:, p * LANES:(p + 1) * LANES]

    gelu(0)
    for q in range(NORM_PARTS):
        if q + 1 < NORM_PARTS:
            gelu(q + 1)
        glu(q)
    for b in range(batch):
        for p in range(planes):
            o_ref[b, :, p * LANES:(p + 1) * LANES] = (
                ub_scr[p, pl.ds(b, tl, stride=batch), :].astype(o_ref.dtype))


def _s5(u3, bbt_re, bbt_im, ct_re, ct_im, a_re, a_im, d_skip, w_glu, b_glu, group):
    batch, seq, width = u3.shape
    half = a_re.shape[1]
    state = half * group // width
    tl = S5_TIME_BLOCK
    rows = tl * batch
    blk = pl.BlockSpec((batch, tl, width), lambda ci: (0, ci, 0))
    return pl.pallas_call(
        functools.partial(_s5_kernel, tl=tl, batch=batch, width=width, half=half, state=state, group=group),
        grid=(seq // tl,),
        in_specs=[blk, _resident(bbt_re.shape), _resident(bbt_im.shape), _resident(ct_re.shape),
                  _resident(ct_im.shape), _resident(a_re.shape), _resident(a_im.shape), _resident((1, width)),
                  _resident(w_glu.shape), _resident((1, width))],
        out_specs=blk,
        out_shape=jax.ShapeDtypeStruct(u3.shape, BF16),
        scratch_shapes=[
            pltpu.VMEM((width, 2 * half), BF16),
            pltpu.VMEM((half, width), BF16),
            pltpu.VMEM((half, width), BF16),
            pltpu.VMEM((batch, 2 * half), F32),
            pltpu.VMEM((width // LANES, rows, LANES), F32),
            pltpu.VMEM((rows, width), F32),
            pltpu.VMEM((rows, 2 * half), F32),
            pltpu.VMEM((rows, width), F32),
        ],
        compiler_params=_params(("arbitrary",), 40),
        name="s5_scan",
    )(u3, bbt_re, bbt_im, ct_re, ct_im, a_re, a_im, d_skip, w_glu, b_glu)


def _out_mlp_kernel(a_ref, b_ref, x_ref, wa_ref, wb_ref, g_ref, wup_hbm, wdn_hbm, gf_ref, o_ref,
                    h_scr, wup_scr, wdn_scr, stage_scr, sem, *, layer, final_norm):
    d, ff = wup_scr.shape
    rows = stage_scr.shape[1]

    @pl.when(pl.program_id(0) == 0)
    def _():
        tiles = ([(wup_hbm, wup_scr, r, c) for r in range(d // rows) for c in range(ff // d)]
                 + [(wdn_hbm, wdn_scr, r, 0) for r in range(ff // rows)])

        def tile_copy(n):
            src, _, r, c = tiles[n]
            return pltpu.make_async_copy(src.at[layer, pl.ds(r * rows, rows), pl.ds(c * d, d)],
                                         stage_scr.at[n % 2], sem.at[n % 2])

        tile_copy(0).start()
        for n, (_, dst, r, c) in enumerate(tiles):
            if n + 1 < len(tiles):
                tile_copy(n + 1).start()
            tile_copy(n).wait()
            dst[r * rows:(r + 1) * rows, c * d:(c + 1) * d] = stage_scr[n % 2].astype(BF16)

    parts = NORM_PARTS
    qrows = x_ref.shape[0] // parts

    def project(p):
        rows = slice(p * qrows, (p + 1) * qrows)
        o_ref[rows, :] = x_ref[rows, :] + _dot(a_ref[rows, :], wa_ref[...]) + _dot(b_ref[rows, :], wb_ref[...])

    def normalise(p):
        rows = slice(p * qrows, (p + 1) * qrows)
        h_scr[rows, :] = _rms_norm(o_ref[rows, :], g_ref[...]).astype(BF16)

    project(0)
    for p in range(parts):
        if p + 1 < parts:
            project(p + 1)
        normalise(p)

    def ff_chunk(c, carry):
        c0 = pl.multiple_of(c * MLP_FF_BLOCK, MLP_FF_BLOCK)
        t = jnp.maximum(_dot(h_scr[...], wup_scr[:, pl.ds(c0, MLP_FF_BLOCK)]), 0.0)
        o_ref[...] += _dot((t * t).astype(BF16), wdn_scr[pl.ds(c0, MLP_FF_BLOCK), :])
        return carry

    lax.fori_loop(0, ff // MLP_FF_BLOCK, ff_chunk, 0, unroll=2)
    if final_norm:
        o_ref[...] = _rms_norm(o_ref[...], gf_ref[...])


def _out_mlp(a, b, x2, wa, wb, g, w_up, w_down, layer, g_final, final_norm):
    n, d = x2.shape
    ff = w_up.shape[2]
    tm = MLP_ROW_BLOCK
    row = lambda i: (i, 0)
    return pl.pallas_call(
        functools.partial(_out_mlp_kernel, layer=layer, final_norm=final_norm),
        grid=(n // tm,),
        in_specs=[
            pl.BlockSpec((tm, a.shape[1]), row),
            pl.BlockSpec((tm, b.shape[1]), row),
            pl.BlockSpec((tm, d), row),
            _resident(wa.shape), _resident(wb.shape), _resident((1, d)),
            pl.BlockSpec(memory_space=pl.ANY), pl.BlockSpec(memory_space=pl.ANY), _resident((1, d)),
        ],
        out_specs=pl.BlockSpec((tm, d), row),
        out_shape=jax.ShapeDtypeStruct((n, d), F32),
        scratch_shapes=[pltpu.VMEM((tm, d), BF16), pltpu.VMEM((d, ff), BF16), pltpu.VMEM((ff, d), BF16),
                        pltpu.VMEM((2, MLP_STAGE_ROWS, d), F32), pltpu.SemaphoreType.DMA((2,))],
        compiler_params=_params(("arbitrary",), 58),
        name="out_mlp",
    )(a, b, x2, wa, wb, g, w_up, w_down, g_final)


def _rope_table_kernel(inv_ref, cos_ref, sin_ref):
    rows, lanes = cos_ref.shape
    pos = lax.broadcasted_iota(jnp.int32, (rows, lanes), 0).astype(F32)
    lane = lax.broadcasted_iota(jnp.int32, (rows, lanes), 1)
    ang = pos * inv_ref[...]
    cos_ref[...] = jnp.cos(ang)
    sin = jnp.sin(ang)
    sin_ref[...] = jnp.where(lane < lanes // 2, -sin, sin)


def _rope_tables(seq, head_dim):
    inv = 1.0 / (10000.0 ** jnp.linspace(0.0, 1.0, head_dim // 2, dtype=F32))
    inv2 = jnp.concatenate([inv, inv])[None, :]
    out = jax.ShapeDtypeStruct((seq, head_dim), F32)
    return pl.pallas_call(_rope_table_kernel, out_shape=[out, out], name="rope_tables")(inv2)


def _odd_in_kernel(x_ref, g_ref, wconv_ref, wq_ref, wk_ref, wv_ref, wg_ref, cw_ref, cos_ref, sin_ref,
                   conv_ref, q_ref, k_ref, v_ref, gate_ref, z_scr, h_scr, *, blocks_per_seq, taps, head_dim):
    i = pl.program_id(0)
    tm = x_ref.shape[0]
    cw = conv_ref.shape[1]
    pad = SUBLANES

    @pl.when(i % blocks_per_seq == 0)
    def _():
        z_scr[0:pad, :] = jnp.zeros((pad, cw), F32)

    qrows = tm // NORM_PARTS

    def normalise(p):
        rows = slice(p * qrows, (p + 1) * qrows)
        h_scr[rows, :] = _rms_norm(x_ref[rows, :], g_ref[...]).astype(BF16)

    def project(p):
        r0 = p * qrows
        rows = slice(r0, r0 + qrows)
        zrows = slice(pad + r0, pad + r0 + qrows)
        h = h_scr[rows, :]
        hc = _dot(h, wconv_ref[:, 0:cw])
        gate_b = _dot(h, wconv_ref[:, cw:2 * cw])
        gate_c = _dot(h, wconv_ref[:, 2 * cw:3 * cw])
        z_scr[zrows, :] = gate_c * hc
        conv = cw_ref[taps - 1:taps, :] * z_scr[zrows, :]
        for j in range(taps - 1):
            shift = taps - 1 - j
            conv = conv + cw_ref[j:j + 1, :] * z_scr[pad + r0 - shift:pad + r0 - shift + qrows, :]
        conv_ref[rows, :] = (gate_b * conv).astype(conv_ref.dtype)

        cos = cos_ref[rows, :]
        sin = sin_ref[rows, :]

        def rotate(w_ref, out_ref, scale):
            x = _dot(h, w_ref[...])
            for hd in range(x.shape[1] // head_dim):
                xh = x[:, hd * head_dim:(hd + 1) * head_dim]
                rot = xh * cos + pltpu.roll(xh, head_dim // 2, 1) * sin
                out_ref[rows, hd * head_dim:(hd + 1) * head_dim] = (rot * scale).astype(out_ref.dtype)

        rotate(wq_ref, q_ref, 1.0)
        rotate(wk_ref, k_ref, head_dim ** -0.5)
        v_ref[rows, :] = _dot(h, wv_ref[...]).astype(v_ref.dtype)
        gate_ref[rows, :] = _dot(h, wg_ref[...]).astype(gate_ref.dtype)

    normalise(0)
    for p in range(NORM_PARTS):
        if p + 1 < NORM_PARTS:
            normalise(p + 1)
        project(p)
    z_scr[0:pad, :] = z_scr[tm:tm + pad, :]


def _odd_in(x2, g, wconv, wq, wk, wv, wg, conv_w, cos, sin, seq):
    n, d = x2.shape
    tm = IN_ROW_BLOCK
    nb = seq // tm
    cw = conv_w.shape[1]
    rw = wq.shape[1]
    head_dim = cos.shape[1]
    row = lambda i: (i, 0)
    tab = pl.BlockSpec((tm, head_dim), lambda i: (i % nb, 0))
    o = lambda w: jax.ShapeDtypeStruct((n, w), BF16)
    return pl.pallas_call(
        functools.partial(_odd_in_kernel, blocks_per_seq=nb, taps=conv_w.shape[0], head_dim=head_dim),
        grid=(n // tm,),
        in_specs=[pl.BlockSpec((tm, d), row), _resident((1, d)), _resident(wconv.shape),
                  _resident(wq.shape), _resident(wk.shape), _resident(wv.shape), _resident(wg.shape),
                  _resident(conv_w.shape), tab, tab],
        out_specs=[pl.BlockSpec((tm, cw), row)] + [pl.BlockSpec((tm, rw), row)] * 4,
        out_shape=[o(cw), o(rw), o(rw), o(rw), o(rw)],
        scratch_shapes=[pltpu.VMEM((tm + 2 * SUBLANES, cw), F32), pltpu.VMEM((tm, d), BF16)],
        compiler_params=_params(("arbitrary",), 40),
        name="odd_in",
    )(x2, g, wconv, wq, wk, wv, wg, conv_w, cos, sin)


def _retention_kernel(q_ref, k_ref, v_ref, g_ref, o_ref, state_scr, decay_scr, *, chunk, heads):
    ri = pl.program_id(1)
    head_dim = q_ref.shape[2] // heads
    log_gamma = [math.log(1.0 - 2.0 ** (-5.0 - h)) for h in range(heads)]

    @pl.when(ri == 0)
    def _():
        state_scr[...] = jnp.zeros_like(state_scr)

    @pl.when((pl.program_id(0) == 0) & (ri == 0))
    def _():
        ridx = lax.broadcasted_iota(jnp.int32, (chunk, chunk), 0)
        cidx = lax.broadcasted_iota(jnp.int32, (chunk, chunk), 1)
        rel = (ridx - cidx).astype(F32)
        for h in range(heads):
            decay_scr[h] = jnp.where(rel >= 0, jnp.exp(log_gamma[h] * jnp.maximum(rel, 0.0)), 0.0)

    idx = lax.broadcasted_iota(jnp.int32, (chunk, 1), 0).astype(F32)
    for h in range(heads):
        query_decay = jnp.exp(log_gamma[h] * (idx + 1.0))
        key_decay = jnp.exp(log_gamma[h] * (chunk - 1.0 - idx))
        chunk_decay = math.exp(log_gamma[h] * chunk)
        cols = slice(h * head_dim, (h + 1) * head_dim)
        state = state_scr[h]
        for c in range(q_ref.shape[1] // chunk):
            rows = slice(c * chunk, (c + 1) * chunk)
            q = q_ref[0, rows, cols]
            k = k_ref[0, rows, cols]
            v = v_ref[0, rows, cols]
            s = _dot_nt(q, k) * decay_scr[h]
            ret = _dot(s.astype(BF16), v) + _dot(q, state.astype(BF16)) * query_decay
            kd = (k.astype(F32) * key_decay).astype(BF16)
            state = state * chunk_decay + _dot_tn(kd, v)
            mu = jnp.mean(ret, axis=-1, keepdims=True)
            dev = ret - mu
            var = jnp.mean(dev * dev, axis=-1, keepdims=True)
            g = g_ref[0, rows, cols].astype(F32)
            o_ref[0, rows, cols] = (g * _sigmoid(g) * (dev * lax.rsqrt(var + EPS))).astype(o_ref.dtype)
        state_scr[h] = state


def _retention(q3, k3, v3, g3):
    b, seq, rw = q3.shape
    head_dim = rw // RET_HEADS
    tr = RET_ROW_BLOCK
    blk = pl.BlockSpec((1, tr, rw), lambda bi, ri: (bi, ri, 0))
    return pl.pallas_call(
        functools.partial(_retention_kernel, chunk=RET_CHUNK, heads=RET_HEADS),
        grid=(b, seq // tr),
        in_specs=[blk, blk, blk, blk],
        out_specs=blk,
        out_shape=jax.ShapeDtypeStruct(q3.shape, BF16),
        scratch_shapes=[pltpu.VMEM((RET_HEADS, head_dim, head_dim), F32),
                        pltpu.VMEM((RET_HEADS, RET_CHUNK, RET_CHUNK), F32)],
        compiler_params=_params(("arbitrary", "arbitrary"), 32),
        name="retention",
    )(q3, k3, v3, g3)


def _even_layer(x2, batch, seq, norm_g, w_in, b_forget, log_dt, lam_re, lam_im, b_re, b_im, c_re, c_im,
                d_skip, w_glu, b_glu, w_out, mlp_g, w_up, w_down, layer, g_final, final_norm):
    d = x2.shape[1]
    heads = b_forget.shape[0]
    fw = heads * FOX_HEAD_DIM
    sw = d_skip.shape[0]
    wqkv = w_in[:, :3 * fw].astype(BF16)
    wf = jnp.pad(w_in[:, 3 * fw:3 * fw + heads], ((0, 0), (0, LANES - heads))).astype(BF16)
    wu = w_in[:, 3 * fw + heads:].astype(BF16)
    bf = jnp.pad(b_forget.astype(F32), (0, LANES - heads))[None, :]
    qscale = jnp.concatenate([jnp.full((fw,), LOG2_E * FOX_HEAD_DIM ** -0.5, F32), jnp.ones((2 * fw,), F32)])[None, :]
    qkv, u, c4 = _even_in(x2, norm_g[None, :], wqkv, qscale, wu, wf, bf, seq)
    fox = _fox(qkv.reshape(batch, seq, 3 * fw), c4, heads)

    a_re, a_im, bb_re, bb_im = _s5_params(log_dt, lam_re, lam_im, b_re, b_im)
    groups, state, group = b_re.shape
    tile_b = lambda t: jnp.tile(t.transpose(0, 2, 1).reshape(groups * group, state), (1, LANES // state))
    tile_c = lambda t: jnp.tile(t.transpose(0, 2, 1).reshape(groups * state, group), (1, LANES // group))
    s5 = _s5(u.reshape(batch, seq, sw), tile_b(bb_re), tile_b(bb_im), tile_c(c_re), tile_c(c_im), a_re, a_im,
             d_skip[None, :], w_glu.astype(BF16), b_glu[None, :], group)

    w_out = w_out.astype(BF16)
    return _out_mlp(fox.reshape(batch * seq, fw), s5.reshape(batch * seq, sw), x2, w_out[:fw], w_out[fw:],
                    mlp_g[None, :], w_up, w_down, layer, g_final[None, :], final_norm)


def _odd_layer(x2, batch, seq, norm_g, w_in, conv_w, w_out, mlp_g, w_up, w_down, layer, g_final, final_norm):
    d = x2.shape[1]
    cw = conv_w.shape[1]
    rw = d - cw
    head_dim = rw // RET_HEADS
    perm = jnp.concatenate([jnp.arange(0, head_dim, 2), jnp.arange(1, head_dim, 2)])
    perm = (jnp.arange(RET_HEADS)[:, None] * head_dim + perm[None, :]).reshape(-1)
    w_in = w_in.astype(BF16)
    wconv = w_in[:, :3 * cw]
    wq = w_in[:, 3 * cw:3 * cw + rw][:, perm]
    wk = w_in[:, 3 * cw + rw:3 * cw + 2 * rw][:, perm]
    wv = w_in[:, 3 * cw + 2 * rw:3 * cw + 3 * rw]
    wg = w_in[:, 3 * cw + 3 * rw:]
    cos, sin = _rope_tables(seq, head_dim)
    conv, q, k, v, gate = _odd_in(x2, norm_g[None, :], wconv, wq, wk, wv, wg, conv_w.astype(F32), cos, sin, seq)
    to3 = lambda t: t.reshape(batch, seq, rw)
    ret = _retention(to3(q), to3(k), to3(v), to3(gate))
    w_out = w_out.astype(BF16)
    return _out_mlp(conv, ret.reshape(batch * seq, rw), x2, w_out[:cw], w_out[cw:],
                    mlp_g[None, :], w_up, w_down, layer, g_final[None, :], final_norm)


def kernel(x, even_norm_mix, even_w_in, even_b_forget, even_s5_log_dt, even_s5_lambda_re, even_s5_lambda_im, even_s5_b_re, even_s5_b_im, even_s5_c_re, even_s5_c_im, even_s5_d, even_s5_w_glu, even_s5_b_glu, even_w_out, odd_norm_mix, odd_w_in, odd_conv_w, odd_w_out, mlp_norm, mlp_w_up, mlp_w_down, final_norm):
    batch, seq, d = x.shape
    depth = mlp_norm.shape[0]
    x2 = x.reshape(batch * seq, d)
    for layer in range(depth):
        j = layer // 2
        last = layer == depth - 1
        if layer % 2 == 0:
            x2 = _even_layer(x2, batch, seq, even_norm_mix[j], even_w_in[j], even_b_forget[j],
                             even_s5_log_dt[j], even_s5_lambda_re[j], even_s5_lambda_im[j],
                             even_s5_b_re[j], even_s5_b_im[j], even_s5_c_re[j], even_s5_c_im[j],
                             even_s5_d[j], even_s5_w_glu[j], even_s5_b_glu[j], even_w_out[j],
                             mlp_norm[layer], mlp_w_up, mlp_w_down, layer, final_norm, last)
        else:
            x2 = _odd_layer(x2, batch, seq, odd_norm_mix[j], odd_w_in[j], odd_conv_w[j], odd_w_out[j],
                            mlp_norm[layer], mlp_w_up, mlp_w_down, layer, final_norm, last)
    return x2.reshape(batch, seq, d)
```

```python
import functools
import math

import jax
import jax.numpy as jnp
from jax import lax
from jax.experimental import pallas as pl
from jax.experimental.pallas import tpu as pltpu

F32 = jnp.float32
BF16 = jnp.bfloat16
EPS = 1e-6

LANES = 128
SUBLANES = 8
FOX_HEAD_DIM = 64
RET_HEADS = 4
RET_CHUNK = 256
ROW_BLOCK = 512
IN_ROW_BLOCK = 1024
S5_TIME_BLOCK = 128
RET_ROW_BLOCK = 1024
MLP_FF_BLOCK = 1024
MLP_ROW_BLOCK = 1024
MLP_STAGE_ROWS = 512
NORM_PARTS = 4
MIB = 1024 * 1024
LOG2_E = math.log2(math.e)
FOX_PAIRS_PER_STEP = 4


def _params(semantics, vmem_mib):
    return pltpu.CompilerParams(dimension_semantics=semantics, vmem_limit_bytes=vmem_mib * MIB)


def _resident(shape):
    return pl.BlockSpec(shape, lambda *_: (0,) * len(shape), pipeline_mode=pl.Buffered(1))


def _rms_norm(x, g):
    return x * lax.rsqrt(jnp.mean(x * x, axis=-1, keepdims=True) + EPS) * g


def _sigmoid(x):
    return 1.0 / (1.0 + jnp.exp(-x))


def _dot(a, b):
    return jnp.dot(a, b, preferred_element_type=F32)


def _dot_nt(a, b):
    return lax.dot_general(a, b, (((1,), (1,)), ((), ())), preferred_element_type=F32)


def _dot_tn(a, b):
    return lax.dot_general(a, b, (((0,), (0,)), ((), ())), preferred_element_type=F32)


def _even_in_kernel(x_ref, g_ref, wqkv_ref, qscale_ref, wu_ref, wf_ref, bf_ref,
                    qkv_ref, u_ref, c_ref, carry_ref, h_scr, *, blocks_per_seq):
    i = pl.program_id(0)

    @pl.when(i % blocks_per_seq == 0)
    def _():
        carry_ref[...] = jnp.zeros_like(carry_ref)

    qrows = x_ref.shape[0] // NORM_PARTS

    def normalise(p):
        rows = slice(p * qrows, (p + 1) * qrows)
        h_scr[rows, :] = _rms_norm(x_ref[rows, :], g_ref[...]).astype(BF16)

    def project(p):
        rows = slice(p * qrows, (p + 1) * qrows)
        h = h_scr[rows, :]
        qkv_ref[rows, :] = (_dot(h, wqkv_ref[...]) * qscale_ref[...]).astype(BF16)
        u_ref[rows, :] = _dot(h, wu_ref[...])
        return _dot(h, wf_ref[...]) + bf_ref[...]

    normalise(0)
    fls = []
    for p in range(NORM_PARTS):
        if p + 1 < NORM_PARTS:
            normalise(p + 1)
        fls.append(project(p))
    fl = jnp.concatenate(fls, axis=0)
    logf = jnp.minimum(fl, 0.0) - jnp.log(1.0 + jnp.exp(-jnp.abs(fl)))
    x = logf.T[:SUBLANES, :]
    tm = x.shape[1]
    lane = lax.broadcasted_iota(jnp.int32, x.shape, 1)
    shift = 1
    while shift < tm:
        x = x + jnp.where(lane >= shift, pltpu.roll(x, shift, 1), 0.0)
        shift *= 2
    cs = x + jnp.concatenate([carry_ref[...]] * (tm // LANES), axis=1)
    carry_ref[...] = jnp.broadcast_to(cs[:, tm - 1:], carry_ref.shape)
    cb = c_ref.shape[3]
    for j in range(tm // cb):
        c_ref[0, j] = cs[:, j * cb:(j + 1) * cb]


def _even_in(x2, g, wqkv, qscale, wu, wf, bf, seq):
    n, d = x2.shape
    tm = IN_ROW_BLOCK
    nb = seq // tm
    cb = ROW_BLOCK
    row = lambda i: (i, 0)
    return pl.pallas_call(
        functools.partial(_even_in_kernel, blocks_per_seq=nb),
        grid=(n // tm,),
        in_specs=[
            pl.BlockSpec((tm, d), row),
            _resident((1, d)),
            _resident(wqkv.shape),
            _resident((1, wqkv.shape[1])),
            _resident(wu.shape),
            _resident(wf.shape),
            _resident((1, LANES)),
        ],
        out_specs=[
            pl.BlockSpec((tm, wqkv.shape[1]), row),
            pl.BlockSpec((tm, wu.shape[1]), row),
            pl.BlockSpec((1, tm // cb, SUBLANES, cb), lambda i: (i // nb, i % nb, 0, 0)),
        ],
        out_shape=[
            jax.ShapeDtypeStruct((n, wqkv.shape[1]), BF16),
            jax.ShapeDtypeStruct((n, wu.shape[1]), F32),
            jax.ShapeDtypeStruct((n // seq, seq // cb, SUBLANES, cb), F32),
        ],
        scratch_shapes=[pltpu.VMEM((SUBLANES, LANES), F32), pltpu.VMEM((tm, d), BF16)],
        compiler_params=_params(("arbitrary",), 40),
        name="even_in",
    )(x2, g, wqkv, qscale, wu, wf, bf)


def _fox_kernel(q_ref, k_ref, v_ref, c_ref, o_ref, m_scr, acc_scr, qm_scr, mask_scr, *, blk, pairs):
    grp = pl.program_id(1)
    qi = pl.program_id(2)
    lane = lax.broadcasted_iota(jnp.int32, (1, LANES), 1)
    in_head = [(lane >= FOX_HEAD_DIM * hh) & (lane < FOX_HEAD_DIM * (hh + 1)) for hh in range(2)]
    for pr in range(pairs):
        for hh in range(2):
            qm_scr[2 * pr + hh] = jnp.where(in_head[hh], q_ref[0, :, pr * LANES:(pr + 1) * LANES], 0)

    first = lane < FOX_HEAD_DIM
    ones = [jnp.where(in_head[hh], 1.0, 0.0).astype(BF16) for hh in range(2)]
    half = blk // 2

    @pl.when((pl.program_id(0) == 0) & (grp == 0) & (qi == 0))
    def _():
        row = lax.broadcasted_iota(jnp.int32, (half, half), 0)
        col = lax.broadcasted_iota(jnp.int32, (half, half), 1)
        mask_scr[...] = jnp.where(row >= col, 0.0, -jnp.inf)

    def block(j, r0, nr, nk, diagonal):
        k0 = pl.multiple_of(j * blk, blk)
        rows = slice(r0, r0 + nr)
        for pr in range(pairs):
            k = k_ref[0, pl.ds(k0, nk), pr * LANES:(pr + 1) * LANES]
            v = v_ref[0, pl.ds(k0, nk), pr * LANES:(pr + 1) * LANES]
            probs, alphas, weights = [], [], []
            for hh in range(2):
                slot = 2 * pr + hh
                h = 2 * (grp * pairs + pr) + hh
                c_q = c_ref[0, qi, pl.ds(h, 1), :][:, 0:1]
                c_k = c_ref[0, j, pl.ds(h, 1), pl.ds(0, nk)]
                s = _dot_nt(qm_scr[slot, rows, :], k) + (c_q - c_k) * LOG2_E
                if diagonal:
                    tail = s[:, nk - half:] + mask_scr[...]
                    s = tail if nk == half else jnp.concatenate([s[:, :nk - half], tail], axis=1)
                    m_new = jnp.broadcast_to(jnp.max(s, axis=-1, keepdims=True), (nr, LANES))
                else:
                    m_prev = m_scr[slot, rows, :]
                    m_new = jnp.maximum(m_prev, jnp.max(s, axis=-1, keepdims=True))
                    alphas.append(jnp.exp2(m_prev - m_new))
                probs.append(jnp.exp2((s - jnp.concatenate([m_new] * (nk // LANES), axis=1)).astype(BF16)))
                m_scr[slot, rows, :] = m_new
                weights.append(jnp.concatenate(
                    [jnp.where(in_head[hh], v, 0), jnp.broadcast_to(ones[hh], (nk, LANES))], axis=1))
            pv = _dot(jnp.concatenate(probs, axis=1), jnp.concatenate(weights, axis=0))
            if diagonal:
                acc_scr[pr, rows, :] = pv
            else:
                alpha = jnp.where(first, alphas[0], alphas[1])
                acc_scr[pr, rows, :] = jnp.concatenate([alpha, alpha], axis=1) * acc_scr[pr, rows, :] + pv

    block(qi, 0, half, half, True)
    block(qi, half, half, blk, True)

    def off_diagonal_pair(jj, carry):
        block(2 * jj, 0, blk, blk, False)
        block(2 * jj + 1, 0, blk, blk, False)
        return carry

    lax.fori_loop(0, qi // 2, off_diagonal_pair, 0)

    @pl.when(qi % 2 == 1)
    def _():
        block(qi - 1, 0, blk, blk, False)

    for pr in range(pairs):
        acc = acc_scr[pr]
        o_ref[0, :, pr * LANES:(pr + 1) * LANES] = (acc[:, :LANES] / acc[:, LANES:]).astype(o_ref.dtype)


def _fox(qkv3, c4, heads):
    b, seq, _ = qkv3.shape
    blk = ROW_BLOCK
    pairs = FOX_PAIRS_PER_STEP
    width = pairs * LANES
    groups = heads * FOX_HEAD_DIM // width
    return pl.pallas_call(
        functools.partial(_fox_kernel, blk=blk, pairs=pairs),
        grid=(b, groups, seq // blk),
        in_specs=[
            pl.BlockSpec((1, blk, width), lambda bi, g, qi: (bi, qi, g)),
            pl.BlockSpec((1, seq, width), lambda bi, g, qi: (bi, 0, groups + g)),
            pl.BlockSpec((1, seq, width), lambda bi, g, qi: (bi, 0, 2 * groups + g)),
            pl.BlockSpec((1,) + c4.shape[1:], lambda bi, g, qi: (bi, 0, 0, 0)),
        ],
        out_specs=pl.BlockSpec((1, blk, width), lambda bi, g, qi: (bi, qi, g)),
        out_shape=jax.ShapeDtypeStruct((b, seq, groups * width), BF16),
        scratch_shapes=[
            pltpu.VMEM((2 * pairs, blk, LANES), F32),
            pltpu.VMEM((pairs, blk, 2 * LANES), F32),
            pltpu.VMEM((2 * pairs, blk, LANES), BF16),
            pltpu.VMEM((blk // 2, blk // 2), F32),
        ],
        compiler_params=_params(("arbitrary", "arbitrary", "arbitrary"), 48),
        name="fox_attention",
    )(qkv3, qkv3, qkv3, c4)


def _s5_param_kernel(log_dt_ref, lr_ref, li_ref, br_ref, bi_ref, ar_ref, ai_ref, bbr_ref, bbi_ref):
    dt = jnp.exp(log_dt_ref[...])
    lr = lr_ref[...]
    li = li_ref[...]
    mag = jnp.exp(lr * dt)
    a_re = mag * jnp.cos(li * dt)
    a_im = mag * jnp.sin(li * dt)
    den = lr * lr + li * li
    n_re = a_re - 1.0
    coef_re = (n_re * lr + a_im * li) / den
    coef_im = (a_im * lr - n_re * li) / den
    br = br_ref[...]
    bi = bi_ref[...]
    ar_ref[...] = a_re
    ai_ref[...] = a_im
    bbr_ref[...] = coef_re * br - coef_im * bi
    bbi_ref[...] = coef_re * bi + coef_im * br


def _s5_params(log_dt, lam_re, lam_im, b_re, b_im):
    groups, state, width = b_re.shape
    rep = lambda t: jnp.repeat(t, width, axis=1)
    flat = (groups, state * width)
    out = jax.ShapeDtypeStruct(flat, F32)
    a_re, a_im, bb_re, bb_im = pl.pallas_call(
        _s5_param_kernel,
        out_shape=[out, out, out, out],
        name="s5_discretise",
    )(rep(jnp.broadcast_to(log_dt[:, None], (groups, state))), rep(lam_re), rep(lam_im),
      b_re.reshape(flat), b_im.reshape(flat))
    unrep = lambda t: t.reshape(groups, state, width)[:, :, 0].reshape(1, groups * state)
    return unrep(a_re), unrep(a_im), bb_re.reshape(b_re.shape), bb_im.reshape(b_re.shape)


def _s5_kernel(u_ref, bbt_re_ref, bbt_im_ref, ct_re_ref, ct_im_ref, are_ref, aim_ref, d_ref, wglu_ref, bglu_ref,
               o_ref, bmat_scr, cre_scr, cim_scr, st_scr, ub_scr, ut_scr, v_scr, yt_scr,
               *, tl, batch, width, half, state, group):
    ci = pl.program_id(0)
    tile = 2 * LANES

    @pl.when(ci == 0)
    def _():
        st_scr[...] = jnp.zeros_like(st_scr)
        bmat_scr[...] = jnp.zeros_like(bmat_scr)
        cre_scr[...] = jnp.zeros_like(cre_scr)
        cim_scr[...] = jnp.zeros_like(cim_scr)
        rb = LANES // state * group
        r = lax.broadcasted_iota(jnp.int32, (rb, LANES), 0)
        l = lax.broadcasted_iota(jnp.int32, (rb, LANES), 1)
        keep_b = (r // group) == (l // state)
        for m in range(half // LANES):
            rows = slice(m * rb, (m + 1) * rb)
            for part, src in enumerate((bbt_re_ref, bbt_im_ref)):
                cols = slice(part * half + m * LANES, part * half + (m + 1) * LANES)
                bmat_scr[rows, cols] = jnp.where(keep_b, src[rows, :], 0.0).astype(BF16)
        rc = LANES // group * state
        r = lax.broadcasted_iota(jnp.int32, (rc, LANES), 0)
        l = lax.broadcasted_iota(jnp.int32, (rc, LANES), 1)
        keep_c = (r // state) == (l // group)
        for n in range(width // LANES):
            rows = slice(n * rc, (n + 1) * rc)
            cols = slice(n * LANES, (n + 1) * LANES)
            cre_scr[rows, cols] = jnp.where(keep_c, ct_re_ref[rows, :], 0.0).astype(BF16)
            cim_scr[rows, cols] = jnp.where(keep_c, ct_im_ref[rows, :], 0.0).astype(BF16)

    planes = width // LANES
    for b in range(batch):
        for p in range(planes):
            ub_scr[p, b * tl:(b + 1) * tl, :] = u_ref[b, :, p * LANES:(p + 1) * LANES]
    for t in range(tl):
        for p in range(planes):
            ut_scr[t * batch:(t + 1) * batch, p * LANES:(p + 1) * LANES] = (
                ub_scr[p, pl.ds(t, batch, stride=tl), :])

    ut = ut_scr[...]
    utb = ut.astype(BF16)
    u_cols_per_tile = tile * width // half
    cg_w = tile
    n_cg = half // cg_w
    oc = LANES
    cg_per_out = oc * half // width // cg_w

    def input_matmul(cg):
        for part in range(2):
            u0 = (cg * u_cols_per_tile) // LANES * LANES
            c0 = part * half + cg * tile
            v_scr[:, c0:c0 + tile] = _dot(utb[:, u0:u0 + LANES], bmat_scr[u0:u0 + LANES, c0:c0 + tile])

    def scan(cg):
        re_cols = slice(cg * cg_w, (cg + 1) * cg_w)
        im_cols = slice(half + cg * cg_w, half + (cg + 1) * cg_w)
        a_re = jnp.broadcast_to(are_ref[:, re_cols], (batch, cg_w))
        a_im = jnp.broadcast_to(aim_ref[:, re_cols], (batch, cg_w))
        s_re = st_scr[:, re_cols]
        s_im = st_scr[:, im_cols]
        for t in range(tl):
            rows = slice(t * batch, (t + 1) * batch)
            s_re, s_im = (a_re * s_re - a_im * s_im + v_scr[rows, re_cols],
                          a_re * s_im + a_im * s_re + v_scr[rows, im_cols])
            v_scr[rows, re_cols] = s_re
            v_scr[rows, im_cols] = s_im
        st_scr[:, re_cols] = s_re
        st_scr[:, im_cols] = s_im

    def output_matmul(n):
        k0 = n * cg_per_out * cg_w
        kw = cg_per_out * cg_w
        o0 = n * oc
        yt_scr[:, o0:o0 + oc] = (
            _dot(v_scr[:, k0:k0 + kw].astype(BF16), cre_scr[k0:k0 + kw, o0:o0 + oc])
            - _dot(v_scr[:, half + k0:half + k0 + kw].astype(BF16), cim_scr[k0:k0 + kw, o0:o0 + oc]))

    input_matmul(0)
    for cg in range(n_cg):
        if cg + 1 < n_cg:
            input_matmul(cg + 1)
        scan(cg)
        if cg > 0 and cg % cg_per_out == 0:
            output_matmul(cg // cg_per_out - 1)
    output_matmul(n_cg // cg_per_out - 1)
    qrows = ut_scr.shape[0] // NORM_PARTS

    def gelu(q):
        rows = slice(q * qrows, (q + 1) * qrows)
        y = yt_scr[rows, :] + d_ref[...] * ut_scr[rows, :]
        yt_scr[rows, :] = 0.5 * y * (1.0 + jnp.tanh(math.sqrt(2.0 / math.pi) * (y + 0.044715 * (y * y * y))))

    def glu(q):
        rows = slice(q * qrows, (q + 1) * qrows)
        y = yt_scr[rows, :]
        y = y * _sigmoid(_dot(y.astype(BF16), wglu_ref[...]) + bglu_ref[...])
        for p in range(planes):
            ub_scr[p, rows, :] = y[:, p * LANES:(p + 1) * LANES]

    gelu(0)
    for q in range(NORM_PARTS):
        if q + 1 < NORM_PARTS:
            gelu(q + 1)
        glu(q)
    for b in range(batch):
        for p in range(planes):
            o_ref[b, :, p * LANES:(p + 1) * LANES] = (
                ub_scr[p, pl.ds(b, tl, stride=batch), :].astype(o_ref.dtype))


def _s5(u3, bbt_re, bbt_im, ct_re, ct_im, a_re, a_im, d_skip, w_glu, b_glu, group):
    batch, seq, width = u3.shape
    half = a_re.shape[1]
    state = half * group // width
    tl = S5_TIME_BLOCK
    rows = tl * batch
    blk = pl.BlockSpec((batch, tl, width), lambda ci: (0, ci, 0))
    return pl.pallas_call(
        functools.partial(_s5_kernel, tl=tl, batch=batch, width=width, half=half, state=state, group=group),
        grid=(seq // tl,),
        in_specs=[blk, _resident(bbt_re.shape), _resident(bbt_im.shape), _resident(ct_re.shape),
                  _resident(ct_im.shape), _resident(a_re.shape), _resident(a_im.shape), _resident((1, width)),
                  _resident(w_glu.shape), _resident((1, width))],
        out_specs=blk,
        out_shape=jax.ShapeDtypeStruct(u3.shape, BF16),
        scratch_shapes=[
            pltpu.VMEM((width, 2 * half), BF16),
            pltpu.VMEM((half, width), BF16),
            pltpu.VMEM((half, width), BF16),
            pltpu.VMEM((batch, 2 * half), F32),
            pltpu.VMEM((width // LANES, rows, LANES), F32),
            pltpu.VMEM((rows, width), F32),
            pltpu.VMEM((rows, 2 * half), F32),
            pltpu.VMEM((rows, width), F32),
        ],
        compiler_params=_params(("arbitrary",), 40),
        name="s5_scan",
    )(u3, bbt_re, bbt_im, ct_re, ct_im, a_re, a_im, d_skip, w_glu, b_glu)


def _out_mlp_kernel(a_ref, b_ref, x_ref, wa_ref, wb_ref, g_ref, wup_hbm, wdn_hbm, gf_ref, o_ref,
                    h_scr, wup_scr, wdn_scr, stage_scr, sem, *, layer, final_norm):
    d, ff = wup_scr.shape
    rows = stage_scr.shape[1]

    @pl.when(pl.program_id(0) == 0)
    def _():
        tiles = ([(wup_hbm, wup_scr, r, c) for r in range(d // rows) for c in range(ff // d)]
                 + [(wdn_hbm, wdn_scr, r, 0) for r in range(ff // rows)])

        def tile_copy(n):
            src, _, r, c = tiles[n]
            return pltpu.make_async_copy(src.at[layer, pl.ds(r * rows, rows), pl.ds(c * d, d)],
                                         stage_scr.at[n % 2], sem.at[n % 2])

        tile_copy(0).start()
        for n, (_, dst, r, c) in enumerate(tiles):
            if n + 1 < len(tiles):
                tile_copy(n + 1).start()
            tile_copy(n).wait()
            dst[r * rows:(r + 1) * rows, c * d:(c + 1) * d] = stage_scr[n % 2].astype(BF16)

    parts = NORM_PARTS
    qrows = x_ref.shape[0] // parts

    def project(p):
        rows = slice(p * qrows, (p + 1) * qrows)
        o_ref[rows, :] = x_ref[rows, :] + _dot(a_ref[rows, :], wa_ref[...]) + _dot(b_ref[rows, :], wb_ref[...])

    def normalise(p):
        rows = slice(p * qrows, (p + 1) * qrows)
        h_scr[rows, :] = _rms_norm(o_ref[rows, :], g_ref[...]).astype(BF16)

    project(0)
    for p in range(parts):
        if p + 1 < parts:
            project(p + 1)
        normalise(p)

    def ff_chunk(c, carry):
        c0 = pl.multiple_of(c * MLP_FF_BLOCK, MLP_FF_BLOCK)
        t = jnp.maximum(_dot(h_scr[...], wup_scr[:, pl.ds(c0, MLP_FF_BLOCK)]), 0.0)
        o_ref[...] += _dot((t * t).astype(BF16), wdn_scr[pl.ds(c0, MLP_FF_BLOCK), :])
        return carry

    lax.fori_loop(0, ff // MLP_FF_BLOCK, ff_chunk, 0, unroll=2)
    if final_norm:
        o_ref[...] = _rms_norm(o_ref[...], gf_ref[...])


def _out_mlp(a, b, x2, wa, wb, g, w_up, w_down, layer, g_final, final_norm):
    n, d = x2.shape
    ff = w_up.shape[2]
    tm = MLP_ROW_BLOCK
    row = lambda i: (i, 0)
    return pl.pallas_call(
        functools.partial(_out_mlp_kernel, layer=layer, final_norm=final_norm),
        grid=(n // tm,),
        in_specs=[
            pl.BlockSpec((tm, a.shape[1]), row),
            pl.BlockSpec((tm, b.shape[1]), row),
            pl.BlockSpec((tm, d), row),
            _resident(wa.shape), _resident(wb.shape), _resident((1, d)),
            pl.BlockSpec(memory_space=pl.ANY), pl.BlockSpec(memory_space=pl.ANY), _resident((1, d)),
        ],
        out_specs=pl.BlockSpec((tm, d), row),
        out_shape=jax.ShapeDtypeStruct((n, d), F32),
        scratch_shapes=[pltpu.VMEM((tm, d), BF16), pltpu.VMEM((d, ff), BF16), pltpu.VMEM((ff, d), BF16),
                        pltpu.VMEM((2, MLP_STAGE_ROWS, d), F32), pltpu.SemaphoreType.DMA((2,))],
        compiler_params=_params(("arbitrary",), 58),
        name="out_mlp",
    )(a, b, x2, wa, wb, g, w_up, w_down, g_final)


def _rope_table_kernel(inv_ref, cos_ref, sin_ref):
    rows, lanes = cos_ref.shape
    pos = lax.broadcasted_iota(jnp.int32, (rows, lanes), 0).astype(F32)
    lane = lax.broadcasted_iota(jnp.int32, (rows, lanes), 1)
    ang = pos * inv_ref[...]
    cos_ref[...] = jnp.cos(ang)
    sin = jnp.sin(ang)
    sin_ref[...] = jnp.where(lane < lanes // 2, -sin, sin)


def _rope_tables(seq, head_dim):
    inv = 1.0 / (10000.0 ** jnp.linspace(0.0, 1.0, head_dim // 2, dtype=F32))
    inv2 = jnp.concatenate([inv, inv])[None, :]
    out = jax.ShapeDtypeStruct((seq, head_dim), F32)
    return pl.pallas_call(_rope_table_kernel, out_shape=[out, out], name="rope_tables")(inv2)


def _odd_in_kernel(x_ref, g_ref, wconv_ref, wq_ref, wk_ref, wv_ref, wg_ref, cw_ref, cos_ref, sin_ref,
                   conv_ref, q_ref, k_ref, v_ref, gate_ref, z_scr, h_scr, *, blocks_per_seq, taps, head_dim):
    i = pl.program_id(0)
    tm = x_ref.shape[0]
    cw = conv_ref.shape[1]
    pad = SUBLANES

    @pl.when(i % blocks_per_seq == 0)
    def _():
        z_scr[0:pad, :] = jnp.zeros((pad, cw), F32)

    qrows = tm // NORM_PARTS

    def normalise(p):
        rows = slice(p * qrows, (p + 1) * qrows)
        h_scr[rows, :] = _rms_norm(x_ref[rows, :], g_ref[...]).astype(BF16)

    def project(p):
        r0 = p * qrows
        rows = slice(r0, r0 + qrows)
        zrows = slice(pad + r0, pad + r0 + qrows)
        h = h_scr[rows, :]
        hc = _dot(h, wconv_ref[:, 0:cw])
        gate_b = _dot(h, wconv_ref[:, cw:2 * cw])
        gate_c = _dot(h, wconv_ref[:, 2 * cw:3 * cw])
        z_scr[zrows, :] = gate_c * hc
        conv = cw_ref[taps - 1:taps, :] * z_scr[zrows, :]
        for j in range(taps - 1):
            shift = taps - 1 - j
            conv = conv + cw_ref[j:j + 1, :] * z_scr[pad + r0 - shift:pad + r0 - shift + qrows, :]
        conv_ref[rows, :] = (gate_b * conv).astype(conv_ref.dtype)

        cos = cos_ref[rows, :]
        sin = sin_ref[rows, :]

        def rotate(w_ref, out_ref, scale):
            x = _dot(h, w_ref[...])
            for hd in range(x.shape[1] // head_dim):
                xh = x[:, hd * head_dim:(hd + 1) * head_dim]
                rot = xh * cos + pltpu.roll(xh, head_dim // 2, 1) * sin
                out_ref[rows, hd * head_dim:(hd + 1) * head_dim] = (rot * scale).astype(out_ref.dtype)

        rotate(wq_ref, q_ref, 1.0)
        rotate(wk_ref, k_ref, head_dim ** -0.5)
        v_ref[rows, :] = _dot(h, wv_ref[...]).astype(v_ref.dtype)
        gate_ref[rows, :] = _dot(h, wg_ref[...]).astype(gate_ref.dtype)

    normalise(0)
    for p in range(NORM_PARTS):
        if p + 1 < NORM_PARTS:
            normalise(p + 1)
        project(p)
    z_scr[0:pad, :] = z_scr[tm:tm + pad, :]


def _odd_in(x2, g, wconv, wq, wk, wv, wg, conv_w, cos, sin, seq):
    n, d = x2.shape
    tm = IN_ROW_BLOCK
    nb = seq // tm
    cw = conv_w.shape[1]
    rw = wq.shape[1]
    head_dim = cos.shape[1]
    row = lambda i: (i, 0)
    tab = pl.BlockSpec((tm, head_dim), lambda i: (i % nb, 0))
    o = lambda w: jax.ShapeDtypeStruct((n, w), BF16)
    return pl.pallas_call(
        functools.partial(_odd_in_kernel, blocks_per_seq=nb, taps=conv_w.shape[0], head_dim=head_dim),
        grid=(n // tm,),
        in_specs=[pl.BlockSpec((tm, d), row), _resident((1, d)), _resident(wconv.shape),
                  _resident(wq.shape), _resident(wk.shape), _resident(wv.shape), _resident(wg.shape),
                  _resident(conv_w.shape), tab, tab],
        out_specs=[pl.BlockSpec((tm, cw), row)] + [pl.BlockSpec((tm, rw), row)] * 4,
        out_shape=[o(cw), o(rw), o(rw), o(rw), o(rw)],
        scratch_shapes=[pltpu.VMEM((tm + 2 * SUBLANES, cw), F32), pltpu.VMEM((tm, d), BF16)],
        compiler_params=_params(("arbitrary",), 40),
        name="odd_in",
    )(x2, g, wconv, wq, wk, wv, wg, conv_w, cos, sin)


def _retention_kernel(q_ref, k_ref, v_ref, g_ref, o_ref, state_scr, decay_scr, *, chunk, heads):
    ri = pl.program_id(1)
    head_dim = q_ref.shape[2] // heads
    log_gamma = [math.log(1.0 - 2.0 ** (-5.0 - h)) for h in range(heads)]

    @pl.when(ri == 0)
    def _():
        state_scr[...] = jnp.zeros_like(state_scr)

    @pl.when((pl.program_id(0) == 0) & (ri == 0))
    def _():
        ridx = lax.broadcasted_iota(jnp.int32, (chunk, chunk), 0)
        cidx = lax.broadcasted_iota(jnp.int32, (chunk, chunk), 1)
        rel = (ridx - cidx).astype(F32)
        for h in range(heads):
            decay_scr[h] = jnp.where(rel >= 0, jnp.exp(log_gamma[h] * jnp.maximum(rel, 0.0)), 0.0)

    idx = lax.broadcasted_iota(jnp.int32, (chunk, 1), 0).astype(F32)
    for h in range(heads):
        query_decay = jnp.exp(log_gamma[h] * (idx + 1.0))
        key_decay = jnp.exp(log_gamma[h] * (chunk - 1.0 - idx))
        chunk_decay = math.exp(log_gamma[h] * chunk)
        cols = slice(h * head_dim, (h + 1) * head_dim)
        state = state_scr[h]
        for c in range(q_ref.shape[1] // chunk):
            rows = slice(c * chunk, (c + 1) * chunk)
            q = q_ref[0, rows, cols]
            k = k_ref[0, rows, cols]
            v = v_ref[0, rows, cols]
            s = _dot_nt(q, k) * decay_scr[h]
            ret = _dot(s.astype(BF16), v) + _dot(q, state.astype(BF16)) * query_decay
            kd = (k.astype(F32) * key_decay).astype(BF16)
            state = state * chunk_decay + _dot_tn(kd, v)
            mu = jnp.mean(ret, axis=-1, keepdims=True)
            dev = ret - mu
            var = jnp.mean(dev * dev, axis=-1, keepdims=True)
            g = g_ref[0, rows, cols].astype(F32)
            o_ref[0, rows, cols] = (g * _sigmoid(g) * (dev * lax.rsqrt(var + EPS))).astype(o_ref.dtype)
        state_scr[h] = state


def _retention(q3, k3, v3, g3):
    b, seq, rw = q3.shape
    head_dim = rw // RET_HEADS
    tr = RET_ROW_BLOCK
    blk = pl.BlockSpec((1, tr, rw), lambda bi, ri: (bi, ri, 0))
    return pl.pallas_call(
        functools.partial(_retention_kernel, chunk=RET_CHUNK, heads=RET_HEADS),
        grid=(b, seq // tr),
        in_specs=[blk, blk, blk, blk],
        out_specs=blk,
        out_shape=jax.ShapeDtypeStruct(q3.shape, BF16),
        scratch_shapes=[pltpu.VMEM((RET_HEADS, head_dim, head_dim), F32),
                        pltpu.VMEM((RET_HEADS, RET_CHUNK, RET_CHUNK), F32)],
        compiler_params=_params(("arbitrary", "arbitrary"), 32),
        name="retention",
    )(q3, k3, v3, g3)


def _even_layer(x2, batch, seq, norm_g, w_in, b_forget, log_dt, lam_re, lam_im, b_re, b_im, c_re, c_im,
                d_skip, w_glu, b_glu, w_out, mlp_g, w_up, w_down, layer, g_final, final_norm):
    d = x2.shape[1]
    heads = b_forget.shape[0]
    fw = heads * FOX_HEAD_DIM
    sw = d_skip.shape[0]
    wqkv = w_in[:, :3 * fw].astype(BF16)
    wf = jnp.pad(w_in[:, 3 * fw:3 * fw + heads], ((0, 0), (0, LANES - heads))).astype(BF16)
    wu = w_in[:, 3 * fw + heads:].astype(BF16)
    bf = jnp.pad(b_forget.astype(F32), (0, LANES - heads))[None, :]
    qscale = jnp.concatenate([jnp.full((fw,), LOG2_E * FOX_HEAD_DIM ** -0.5, F32), jnp.ones((2 * fw,), F32)])[None, :]
    qkv, u, c4 = _even_in(x2, norm_g[None, :], wqkv, qscale, wu, wf, bf, seq)
    fox = _fox(qkv.reshape(batch, seq, 3 * fw), c4, heads)

    a_re, a_im, bb_re, bb_im = _s5_params(log_dt, lam_re, lam_im, b_re, b_im)
    groups, state, group = b_re.shape
    tile_b = lambda t: jnp.tile(t.transpose(0, 2, 1).reshape(groups * group, state), (1, LANES // state))
    tile_c = lambda t: jnp.tile(t.transpose(0, 2, 1).reshape(groups * state, group), (1, LANES // group))
    s5 = _s5(u.reshape(batch, seq, sw), tile_b(bb_re), tile_b(bb_im), tile_c(c_re), tile_c(c_im), a_re, a_im,
             d_skip[None, :], w_glu.astype(BF16), b_glu[None, :], group)

    w_out = w_out.astype(BF16)
    return _out_mlp(fox.reshape(batch * seq, fw), s5.reshape(batch * seq, sw), x2, w_out[:fw], w_out[fw:],
                    mlp_g[None, :], w_up, w_down, layer, g_final[None, :], final_norm)


def _odd_layer(x2, batch, seq, norm_g, w_in, conv_w, w_out, mlp_g, w_up, w_down, layer, g_final, final_norm):
    d = x2.shape[1]
    cw = conv_w.shape[1]
    rw = d - cw
    head_dim = rw // RET_HEADS
    perm = jnp.concatenate([jnp.arange(0, head_dim, 2), jnp.arange(1, head_dim, 2)])
    perm = (jnp.arange(RET_HEADS)[:, None] * head_dim + perm[None, :]).reshape(-1)
    w_in = w_in.astype(BF16)
    wconv = w_in[:, :3 * cw]
    wq = w_in[:, 3 * cw:3 * cw + rw][:, perm]
    wk = w_in[:, 3 * cw + rw:3 * cw + 2 * rw][:, perm]
    wv = w_in[:, 3 * cw + 2 * rw:3 * cw + 3 * rw]
    wg = w_in[:, 3 * cw + 3 * rw:]
    cos, sin = _rope_tables(seq, head_dim)
    conv, q, k, v, gate = _odd_in(x2, norm_g[None, :], wconv, wq, wk, wv, wg, conv_w.astype(F32), cos, sin, seq)
    to3 = lambda t: t.reshape(batch, seq, rw)
    ret = _retention(to3(q), to3(k), to3(v), to3(gate))
    w_out = w_out.astype(BF16)
    return _out_mlp(conv, ret.reshape(batch * seq, rw), x2, w_out[:cw], w_out[cw:],
                    mlp_g[None, :], w_up, w_down, layer, g_final[None, :], final_norm)


def kernel(x, even_norm_mix, even_w_in, even_b_forget, even_s5_log_dt, even_s5_lambda_re, even_s5_lambda_im, even_s5_b_re, even_s5_b_im, even_s5_c_re, even_s5_c_im, even_s5_d, even_s5_w_glu, even_s5_b_glu, even_w_out, odd_norm_mix, odd_w_in, odd_conv_w, odd_w_out, mlp_norm, mlp_w_up, mlp_w_down, final_norm):
    batch, seq, d = x.shape
    depth = mlp_norm.shape[0]
    x2 = x.reshape(batch * seq, d)
    for layer in range(depth):
        j = layer // 2
        last = layer == depth - 1
        if layer % 2 == 0:
            x2 = _even_layer(x2, batch, seq, even_norm_mix[j], even_w_in[j], even_b_forget[j],
                             even_s5_log_dt[j], even_s5_lambda_re[j], even_s5_lambda_im[j],
                             even_s5_b_re[j], even_s5_b_im[j], even_s5_c_re[j], even_s5_c_im[j],
                             even_s5_d[j], even_s5_w_glu[j], even_s5_b_glu[j], even_w_out[j],
                             mlp_norm[layer], mlp_w_up, mlp_w_down, layer, final_norm, last)
        else:
            x2 = _odd_layer(x2, batch, seq, odd_norm_mix[j], odd_w_in[j], odd_conv_w[j], odd_w_out[j],
                            mlp_norm[layer], mlp_w_up, mlp_w_down, layer, final_norm, last)
    return x2.reshape(batch, seq, d)
```

```python
import functools
import math

import jax
import jax.numpy as jnp
from jax import lax
from jax.experimental import pallas as pl
from jax.experimental.pallas import tpu as pltpu

F32 = jnp.float32
BF16 = jnp.bfloat16
EPS = 1e-6

LANES = 128
SUBLANES = 8
FOX_HEAD_DIM = 64
RET_HEADS = 4
RET_CHUNK = 256
ROW_BLOCK = 512
IN_ROW_BLOCK = 1024
EVEN_ROW_BLOCK = 2048
S5_TIME_BLOCK = 128
RET_ROW_BLOCK = 1024
MLP_FF_BLOCK = 1024
MLP_ROW_BLOCK = 1024
MLP_STAGE_ROWS = 512
NORM_PARTS = 4
MIB = 1024 * 1024
LOG2_E = math.log2(math.e)
FOX_PAIRS_PER_STEP = 4


def _params(semantics, vmem_mib):
    return pltpu.CompilerParams(dimension_semantics=semantics, vmem_limit_bytes=vmem_mib * MIB)


def _resident(shape):
    return pl.BlockSpec(shape, lambda *_: (0,) * len(shape), pipeline_mode=pl.Buffered(1))


def _rms_norm(x, g):
    return x * lax.rsqrt(jnp.mean(x * x, axis=-1, keepdims=True) + EPS) * g


def _sigmoid(x):
    return 1.0 / (1.0 + jnp.exp(-x))


def _dot(a, b):
    return jnp.dot(a, b, preferred_element_type=F32)


def _dot_nt(a, b):
    return lax.dot_general(a, b, (((1,), (1,)), ((), ())), preferred_element_type=F32)


def _dot_tn(a, b):
    return lax.dot_general(a, b, (((0,), (0,)), ((), ())), preferred_element_type=F32)


def _even_in_kernel(x_ref, g_ref, wqkv_ref, qscale_ref, wu_ref, wf_ref, bf_ref,
                    qkv_ref, u_ref, c_ref, carry_ref, h_scr, *, blocks_per_seq):
    i = pl.program_id(0)

    @pl.when(i % blocks_per_seq == 0)
    def _():
        carry_ref[...] = jnp.zeros_like(carry_ref)

    qrows = x_ref.shape[0] // NORM_PARTS

    def normalise(p):
        rows = slice(p * qrows, (p + 1) * qrows)
        h_scr[rows, :] = _rms_norm(x_ref[rows, :], g_ref[...]).astype(BF16)

    def project(p):
        rows = slice(p * qrows, (p + 1) * qrows)
        h = h_scr[rows, :]
        qkv_ref[rows, :] = (_dot(h, wqkv_ref[...]) * qscale_ref[...]).astype(BF16)
        u_ref[rows, :] = _dot(h, wu_ref[...])
        return _dot(h, wf_ref[...]) + bf_ref[...]

    normalise(0)
    fls = []
    for p in range(NORM_PARTS):
        if p + 1 < NORM_PARTS:
            normalise(p + 1)
        fls.append(project(p))
    fl = jnp.concatenate(fls, axis=0)
    logf = jnp.minimum(fl, 0.0) - jnp.log(1.0 + jnp.exp(-jnp.abs(fl)))
    x = logf.T[:SUBLANES, :]
    tm = x.shape[1]
    lane = lax.broadcasted_iota(jnp.int32, x.shape, 1)
    shift = 1
    while shift < tm:
        x = x + jnp.where(lane >= shift, pltpu.roll(x, shift, 1), 0.0)
        shift *= 2
    cs = x + jnp.concatenate([carry_ref[...]] * (tm // LANES), axis=1)
    carry_ref[...] = jnp.broadcast_to(cs[:, tm - 1:], carry_ref.shape)
    cb = c_ref.shape[3]
    for j in range(tm // cb):
        c_ref[0, j] = cs[:, j * cb:(j + 1) * cb]


def _even_in(x2, g, wqkv, qscale, wu, wf, bf, seq):
    n, d = x2.shape
    tm = EVEN_ROW_BLOCK
    nb = seq // tm
    cb = ROW_BLOCK
    row = lambda i: (i, 0)
    return pl.pallas_call(
        functools.partial(_even_in_kernel, blocks_per_seq=nb),
        grid=(n // tm,),
        in_specs=[
            pl.BlockSpec((tm, d), row),
            _resident((1, d)),
            _resident(wqkv.shape),
            _resident((1, wqkv.shape[1])),
            _resident(wu.shape),
            _resident(wf.shape),
            _resident((1, LANES)),
        ],
        out_specs=[
            pl.BlockSpec((tm, wqkv.shape[1]), row),
            pl.BlockSpec((tm, wu.shape[1]), row),
            pl.BlockSpec((1, tm // cb, SUBLANES, cb), lambda i: (i // nb, i % nb, 0, 0)),
        ],
        out_shape=[
            jax.ShapeDtypeStruct((n, wqkv.shape[1]), BF16),
            jax.ShapeDtypeStruct((n, wu.shape[1]), F32),
            jax.ShapeDtypeStruct((n // seq, seq // cb, SUBLANES, cb), F32),
        ],
        scratch_shapes=[pltpu.VMEM((SUBLANES, LANES), F32), pltpu.VMEM((tm, d), BF16)],
        compiler_params=_params(("arbitrary",), 48),
        name="even_in",
    )(x2, g, wqkv, qscale, wu, wf, bf)


def _fox_kernel(q_ref, k_ref, v_ref, c_ref, o_ref, m_scr, acc_scr, qm_scr, mask_scr, *, blk, pairs):
    grp = pl.program_id(1)
    qi = pl.program_id(2)
    lane = lax.broadcasted_iota(jnp.int32, (1, LANES), 1)
    in_head = [(lane >= FOX_HEAD_DIM * hh) & (lane < FOX_HEAD_DIM * (hh + 1)) for hh in range(2)]
    for pr in range(pairs):
        for hh in range(2):
            qm_scr[2 * pr + hh] = jnp.where(in_head[hh], q_ref[0, :, pr * LANES:(pr + 1) * LANES], 0)

    first = lane < FOX_HEAD_DIM
    ones = [jnp.where(in_head[hh], 1.0, 0.0).astype(BF16) for hh in range(2)]
    half = blk // 2

    @pl.when((pl.program_id(0) == 0) & (grp == 0) & (qi == 0))
    def _():
        row = lax.broadcasted_iota(jnp.int32, (half, half), 0)
        col = lax.broadcasted_iota(jnp.int32, (half, half), 1)
        mask_scr[...] = jnp.where(row >= col, 0.0, -jnp.inf)

    def block(j, r0, nr, nk, diagonal):
        k0 = pl.multiple_of(j * blk, blk)
        rows = slice(r0, r0 + nr)
        for pr in range(pairs):
            k = k_ref[0, pl.ds(k0, nk), pr * LANES:(pr + 1) * LANES]
            v = v_ref[0, pl.ds(k0, nk), pr * LANES:(pr + 1) * LANES]
            probs, alphas, weights = [], [], []
            for hh in range(2):
                slot = 2 * pr + hh
                h = 2 * (grp * pairs + pr) + hh
                c_q = c_ref[0, qi, pl.ds(h, 1), :][:, 0:1]
                c_k = c_ref[0, j, pl.ds(h, 1), pl.ds(0, nk)]
                s = _dot_nt(qm_scr[slot, rows, :], k) + (c_q - c_k) * LOG2_E
                if diagonal:
                    tail = s[:, nk - half:] + mask_scr[...]
                    s = tail if nk == half else jnp.concatenate([s[:, :nk - half], tail], axis=1)
                    m_new = jnp.broadcast_to(jnp.max(s, axis=-1, keepdims=True), (nr, LANES))
                else:
                    m_prev = m_scr[slot, rows, :]
                    m_new = jnp.maximum(m_prev, jnp.max(s, axis=-1, keepdims=True))
                    alphas.append(jnp.exp2(m_prev - m_new))
                probs.append(jnp.exp2((s - jnp.concatenate([m_new] * (nk // LANES), axis=1)).astype(BF16)))
                m_scr[slot, rows, :] = m_new
                weights.append(jnp.concatenate(
                    [jnp.where(in_head[hh], v, 0), jnp.broadcast_to(ones[hh], (nk, LANES))], axis=1))
            pv = _dot(jnp.concatenate(probs, axis=1), jnp.concatenate(weights, axis=0))
            if diagonal:
                acc_scr[pr, rows, :] = pv
            else:
                alpha = jnp.where(first, alphas[0], alphas[1])
                acc_scr[pr, rows, :] = jnp.concatenate([alpha, alpha], axis=1) * acc_scr[pr, rows, :] + pv

    block(qi, 0, half, half, True)
    block(qi, half, half, blk, True)

    def off_diagonal_pair(jj, carry):
        block(2 * jj, 0, blk, blk, False)
        block(2 * jj + 1, 0, blk, blk, False)
        return carry

    lax.fori_loop(0, qi // 2, off_diagonal_pair, 0)

    @pl.when(qi % 2 == 1)
    def _():
        block(qi - 1, 0, blk, blk, False)

    for pr in range(pairs):
        acc = acc_scr[pr]
        o_ref[0, :, pr * LANES:(pr + 1) * LANES] = (acc[:, :LANES] / acc[:, LANES:]).astype(o_ref.dtype)


def _fox(qkv3, c4, heads):
    b, seq, _ = qkv3.shape
    blk = ROW_BLOCK
    pairs = FOX_PAIRS_PER_STEP
    width = pairs * LANES
    groups = heads * FOX_HEAD_DIM // width
    return pl.pallas_call(
        functools.partial(_fox_kernel, blk=blk, pairs=pairs),
        grid=(b, groups, seq // blk),
        in_specs=[
            pl.BlockSpec((1, blk, width), lambda bi, g, qi: (bi, qi, g)),
            pl.BlockSpec((1, seq, width), lambda bi, g, qi: (bi, 0, groups + g)),
            pl.BlockSpec((1, seq, width), lambda bi, g, qi: (bi, 0, 2 * groups + g)),
            pl.BlockSpec((1,) + c4.shape[1:], lambda bi, g, qi: (bi, 0, 0, 0)),
        ],
        out_specs=pl.BlockSpec((1, blk, width), lambda bi, g, qi: (bi, qi, g)),
        out_shape=jax.ShapeDtypeStruct((b, seq, groups * width), BF16),
        scratch_shapes=[
            pltpu.VMEM((2 * pairs, blk, LANES), F32),
            pltpu.VMEM((pairs, blk, 2 * LANES), F32),
            pltpu.VMEM((2 * pairs, blk, LANES), BF16),
            pltpu.VMEM((blk // 2, blk // 2), F32),
        ],
        compiler_params=_params(("arbitrary", "arbitrary", "arbitrary"), 48),
        name="fox_attention",
    )(qkv3, qkv3, qkv3, c4)


def _s5_param_kernel(log_dt_ref, lr_ref, li_ref, br_ref, bi_ref, ar_ref, ai_ref, bbr_ref, bbi_ref):
    dt = jnp.exp(log_dt_ref[...])
    lr = lr_ref[...]
    li = li_ref[...]
    mag = jnp.exp(lr * dt)
    a_re = mag * jnp.cos(li * dt)
    a_im = mag * jnp.sin(li * dt)
    den = lr * lr + li * li
    n_re = a_re - 1.0
    coef_re = (n_re * lr + a_im * li) / den
    coef_im = (a_im * lr - n_re * li) / den
    br = br_ref[...]
    bi = bi_ref[...]
    ar_ref[...] = a_re
    ai_ref[...] = a_im
    bbr_ref[...] = coef_re * br - coef_im * bi
    bbi_ref[...] = coef_re * bi + coef_im * br


def _s5_params(log_dt, lam_re, lam_im, b_re, b_im):
    groups, state, width = b_re.shape
    rep = lambda t: jnp.repeat(t, width, axis=1)
    flat = (groups, state * width)
    out = jax.ShapeDtypeStruct(flat, F32)
    a_re, a_im, bb_re, bb_im = pl.pallas_call(
        _s5_param_kernel,
        out_shape=[out, out, out, out],
        name="s5_discretise",
    )(rep(jnp.broadcast_to(log_dt[:, None], (groups, state))), rep(lam_re), rep(lam_im),
      b_re.reshape(flat), b_im.reshape(flat))
    unrep = lambda t: t.reshape(groups, state, width)[:, :, 0].reshape(1, groups * state)
    return unrep(a_re), unrep(a_im), bb_re.reshape(b_re.shape), bb_im.reshape(b_re.shape)


def _s5_kernel(u_ref, bbt_re_ref, bbt_im_ref, ct_re_ref, ct_im_ref, are_ref, aim_ref, d_ref, wglu_ref, bglu_ref,
               o_ref, bmat_scr, cre_scr, cim_scr, st_scr, ub_scr, ut_scr, v_scr, yt_scr,
               *, tl, batch, width, half, state, group):
    ci = pl.program_id(0)
    tile = 2 * LANES

    @pl.when(ci == 0)
    def _():
        st_scr[...] = jnp.zeros_like(st_scr)
        bmat_scr[...] = jnp.zeros_like(bmat_scr)
        cre_scr[...] = jnp.zeros_like(cre_scr)
        cim_scr[...] = jnp.zeros_like(cim_scr)
        rb = LANES // state * group
        r = lax.broadcasted_iota(jnp.int32, (rb, LANES), 0)
        l = lax.broadcasted_iota(jnp.int32, (rb, LANES), 1)
        keep_b = (r // group) == (l // state)
        for m in range(half // LANES):
            rows = slice(m * rb, (m + 1) * rb)
            for part, src in enumerate((bbt_re_ref, bbt_im_ref)):
                cols = slice(part * half + m * LANES, part * half + (m + 1) * LANES)
                bmat_scr[rows, cols] = jnp.where(keep_b, src[rows, :], 0.0).astype(BF16)
        rc = LANES // group * state
        r = lax.broadcasted_iota(jnp.int32, (rc, LANES), 0)
        l = lax.broadcasted_iota(jnp.int32, (rc, LANES), 1)
        keep_c = (r // state) == (l // group)
        for n in range(width // LANES):
            rows = slice(n * rc, (n + 1) * rc)
            cols = slice(n * LANES, (n + 1) * LANES)
            cre_scr[rows, cols] = jnp.where(keep_c, ct_re_ref[rows, :], 0.0).astype(BF16)
            cim_scr[rows, cols] = jnp.where(keep_c, ct_im_ref[rows, :], 0.0).astype(BF16)

    planes = width // LANES
    for b in range(batch):
        for p in range(planes):
            ub_scr[p, b * tl:(b + 1) * tl, :] = u_ref[b, :, p * LANES:(p + 1) * LANES]
    for t in range(tl):
        for p in range(planes):
            ut_scr[t * batch:(t + 1) * batch, p * LANES:(p + 1) * LANES] = (
                ub_scr[p, pl.ds(t, batch, stride=tl), :])

    ut = ut_scr[...]
    utb = ut.astype(BF16)
    u_cols_per_tile = tile * width // half
    cg_w = tile
    n_cg = half // cg_w
    oc = LANES
    cg_per_out = oc * half // width // cg_w

    def input_matmul(cg):
        for part in range(2):
            u0 = (cg * u_cols_per_tile) // LANES * LANES
            c0 = part * half + cg * tile
            v_scr[:, c0:c0 + tile] = _dot(utb[:, u0:u0 + LANES], bmat_scr[u0:u0 + LANES, c0:c0 + tile])

    def scan(cg):
        re_cols = slice(cg * cg_w, (cg + 1) * cg_w)
        im_cols = slice(half + cg * cg_w, half + (cg + 1) * cg_w)
        a_re = jnp.broadcast_to(are_ref[:, re_cols], (batch, cg_w))
        a_im = jnp.broadcast_to(aim_ref[:, re_cols], (batch, cg_w))
        s_re = st_scr[:, re_cols]
        s_im = st_scr[:, im_cols]
        for t in range(tl):
            rows = slice(t * batch, (t + 1) * batch)
            s_re, s_im = (a_re * s_re - a_im * s_im + v_scr[rows, re_cols],
                          a_re * s_im + a_im * s_re + v_scr[rows, im_cols])
            v_scr[rows, re_cols] = s_re
            v_scr[rows, im_cols] = s_im
        st_scr[:, re_cols] = s_re
        st_scr[:, im_cols] = s_im

    def output_matmul(n):
        k0 = n * cg_per_out * cg_w
        kw = cg_per_out * cg_w
        o0 = n * oc
        yt_scr[:, o0:o0 + oc] = (
            _dot(v_scr[:, k0:k0 + kw].astype(BF16), cre_scr[k0:k0 + kw, o0:o0 + oc])
            - _dot(v_scr[:, half + k0:half + k0 + kw].astype(BF16), cim_scr[k0:k0 + kw, o0:o0 + oc]))

    input_matmul(0)
    for cg in range(n_cg):
        if cg + 1 < n_cg:
            input_matmul(cg + 1)
        scan(cg)
        if cg > 0 and cg % cg_per_out == 0:
            output_matmul(cg // cg_per_out - 1)
    output_matmul(n_cg // cg_per_out - 1)
    qrows = ut_scr.shape[0] // NORM_PARTS

    def gelu(q):
        rows = slice(q * qrows, (q + 1) * qrows)
        y = yt_scr[rows, :] + d_ref[...] * ut_scr[rows, :]
        yt_scr[rows, :] = 0.5 * y * (1.0 + jnp.tanh(math.sqrt(2.0 / math.pi) * (y + 0.044715 * (y * y * y))))

    def glu(q):
        rows = slice(q * qrows, (q + 1) * qrows)
        y = yt_scr[rows, :]
        y = y * _sigmoid(_dot(y.astype(BF16), wglu_ref[...]) + bglu_ref[...])
        for p in range(planes):
            ub_scr[p, rows, :] = y[:, p * LANES:(p + 1) * LANES]

    gelu(0)
    for q in range(NORM_PARTS):
        if q + 1 < NORM_PARTS:
            gelu(q + 1)
        glu(q)
    for b in range(batch):
        for p in range(planes):
            o_ref[b, :, p * LANES:(p + 1) * LANES] = (
                ub_scr[p, pl.ds(b, tl, stride=batch), :].astype(o_ref.dtype))


def _s5(u3, bbt_re, bbt_im, ct_re, ct_im, a_re, a_im, d_skip, w_glu, b_glu, group):
    batch, seq, width = u3.shape
    half = a_re.shape[1]
    state = half * group // width
    tl = S5_TIME_BLOCK
    rows = tl * batch
    blk = pl.BlockSpec((batch, tl, width), lambda ci: (0, ci, 0))
    return pl.pallas_call(
        functools.partial(_s5_kernel, tl=tl, batch=batch, width=width, half=half, state=state, group=group),
        grid=(seq // tl,),
        in_specs=[blk, _resident(bbt_re.shape), _resident(bbt_im.shape), _resident(ct_re.shape),
                  _resident(ct_im.shape), _resident(a_re.shape), _resident(a_im.shape), _resident((1, width)),
                  _resident(w_glu.shape), _resident((1, width))],
        out_specs=blk,
        out_shape=jax.ShapeDtypeStruct(u3.shape, BF16),
        scratch_shapes=[
            pltpu.VMEM((width, 2 * half), BF16),
            pltpu.VMEM((half, width), BF16),
            pltpu.VMEM((half, width), BF16),
            pltpu.VMEM((batch, 2 * half), F32),
            pltpu.VMEM((width // LANES, rows, LANES), F32),
            pltpu.VMEM((rows, width), F32),
            pltpu.VMEM((rows, 2 * half), F32),
            pltpu.VMEM((rows, width), F32),
        ],
        compiler_params=_params(("arbitrary",), 40),
        name="s5_scan",
    )(u3, bbt_re, bbt_im, ct_re, ct_im, a_re, a_im, d_skip, w_glu, b_glu)


def _out_mlp_kernel(a_ref, b_ref, x_ref, wa_ref, wb_ref, g_ref, wup_hbm, wdn_hbm, gf_ref, o_ref,
                    h_scr, wup_scr, wdn_scr, stage_scr, sem, *, layer, final_norm):
    d, ff = wup_scr.shape
    rows = stage_scr.shape[1]

    @pl.when(pl.program_id(0) == 0)
    def _():
        tiles = ([(wup_hbm, wup_scr, r, c) for r in range(d // rows) for c in range(ff // d)]
                 + [(wdn_hbm, wdn_scr, r, 0) for r in range(ff // rows)])

        def tile_copy(n):
            src, _, r, c = tiles[n]
            return pltpu.make_async_copy(src.at[layer, pl.ds(r * rows, rows), pl.ds(c * d, d)],
                                         stage_scr.at[n % 2], sem.at[n % 2])

        tile_copy(0).start()
        for n, (_, dst, r, c) in enumerate(tiles):
            if n + 1 < len(tiles):
                tile_copy(n + 1).start()
            tile_copy(n).wait()
            dst[r * rows:(r + 1) * rows, c * d:(c + 1) * d] = stage_scr[n % 2].astype(BF16)

    parts = NORM_PARTS
    qrows = x_ref.shape[0] // parts

    def project(p):
        rows = slice(p * qrows, (p + 1) * qrows)
        o_ref[rows, :] = x_ref[rows, :] + _dot(a_ref[rows, :], wa_ref[...]) + _dot(b_ref[rows, :], wb_ref[...])

    def normalise(p):
        rows = slice(p * qrows, (p + 1) * qrows)
        h_scr[rows, :] = _rms_norm(o_ref[rows, :], g_ref[...]).astype(BF16)

    project(0)
    for p in range(parts):
        if p + 1 < parts:
            project(p + 1)
        normalise(p)

    def ff_chunk(c, carry):
        c0 = pl.multiple_of(c * MLP_FF_BLOCK, MLP_FF_BLOCK)
        t = jnp.maximum(_dot(h_scr[...], wup_scr[:, pl.ds(c0, MLP_FF_BLOCK)]), 0.0)
        o_ref[...] += _dot((t * t).astype(BF16), wdn_scr[pl.ds(c0, MLP_FF_BLOCK), :])
        return carry

    lax.fori_loop(0, ff // MLP_FF_BLOCK, ff_chunk, 0, unroll=2)
    if final_norm:
        o_ref[...] = _rms_norm(o_ref[...], gf_ref[...])


def _out_mlp(a, b, x2, wa, wb, g, w_up, w_down, layer, g_final, final_norm):
    n, d = x2.shape
    ff = w_up.shape[2]
    tm = MLP_ROW_BLOCK
    row = lambda i: (i, 0)
    return pl.pallas_call(
        functools.partial(_out_mlp_kernel, layer=layer, final_norm=final_norm),
        grid=(n // tm,),
        in_specs=[
            pl.BlockSpec((tm, a.shape[1]), row),
            pl.BlockSpec((tm, b.shape[1]), row),
            pl.BlockSpec((tm, d), row),
            _resident(wa.shape), _resident(wb.shape), _resident((1, d)),
            pl.BlockSpec(memory_space=pl.ANY), pl.BlockSpec(memory_space=pl.ANY), _resident((1, d)),
        ],
        out_specs=pl.BlockSpec((tm, d), row),
        out_shape=jax.ShapeDtypeStruct((n, d), F32),
        scratch_shapes=[pltpu.VMEM((tm, d), BF16), pltpu.VMEM((d, ff), BF16), pltpu.VMEM((ff, d), BF16),
                        pltpu.VMEM((2, MLP_STAGE_ROWS, d), F32), pltpu.SemaphoreType.DMA((2,))],
        compiler_params=_params(("arbitrary",), 58),
        name="out_mlp",
    )(a, b, x2, wa, wb, g, w_up, w_down, g_final)


def _rope_table_kernel(inv_ref, cos_ref, sin_ref):
    rows, lanes = cos_ref.shape
    pos = lax.broadcasted_iota(jnp.int32, (rows, lanes), 0).astype(F32)
    lane = lax.broadcasted_iota(jnp.int32, (rows, lanes), 1)
    ang = pos * inv_ref[...]
    cos_ref[...] = jnp.cos(ang)
    sin = jnp.sin(ang)
    sin_ref[...] = jnp.where(lane < lanes // 2, -sin, sin)


def _rope_tables(seq, head_dim):
    inv = 1.0 / (10000.0 ** jnp.linspace(0.0, 1.0, head_dim // 2, dtype=F32))
    inv2 = jnp.concatenate([inv, inv])[None, :]
    out = jax.ShapeDtypeStruct((seq, head_dim), F32)
    return pl.pallas_call(_rope_table_kernel, out_shape=[out, out], name="rope_tables")(inv2)


def _odd_in_kernel(x_ref, g_ref, wconv_ref, wq_ref, wk_ref, wv_ref, wg_ref, cw_ref, cos_ref, sin_ref,
                   conv_ref, q_ref, k_ref, v_ref, gate_ref, z_scr, h_scr, *, blocks_per_seq, taps, head_dim):
    i = pl.program_id(0)
    tm = x_ref.shape[0]
    cw = conv_ref.shape[1]
    pad = SUBLANES

    @pl.when(i % blocks_per_seq == 0)
    def _():
        z_scr[0:pad, :] = jnp.zeros((pad, cw), F32)

    qrows = tm // NORM_PARTS

    def normalise(p):
        rows = slice(p * qrows, (p + 1) * qrows)
        h_scr[rows, :] = _rms_norm(x_ref[rows, :], g_ref[...]).astype(BF16)

    def project(p):
        r0 = p * qrows
        rows = slice(r0, r0 + qrows)
        zrows = slice(pad + r0, pad + r0 + qrows)
        h = h_scr[rows, :]
        hc = _dot(h, wconv_ref[:, 0:cw])
        gate_b = _dot(h, wconv_ref[:, cw:2 * cw])
        gate_c = _dot(h, wconv_ref[:, 2 * cw:3 * cw])
        z_scr[zrows, :] = gate_c * hc
        conv = cw_ref[taps - 1:taps, :] * z_scr[zrows, :]
        for j in range(taps - 1):
            shift = taps - 1 - j
            conv = conv + cw_ref[j:j + 1, :] * z_scr[pad + r0 - shift:pad + r0 - shift + qrows, :]
        conv_ref[rows, :] = (gate_b * conv).astype(conv_ref.dtype)

        cos = cos_ref[rows, :]
        sin = sin_ref[rows, :]

        def rotate(w_ref, out_ref, scale):
            x = _dot(h, w_ref[...])
            for hd in range(x.shape[1] // head_dim):
                xh = x[:, hd * head_dim:(hd + 1) * head_dim]
                rot = xh * cos + pltpu.roll(xh, head_dim // 2, 1) * sin
                out_ref[rows, hd * head_dim:(hd + 1) * head_dim] = (rot * scale).astype(out_ref.dtype)

        rotate(wq_ref, q_ref, 1.0)
        rotate(wk_ref, k_ref, head_dim ** -0.5)
        v_ref[rows, :] = _dot(h, wv_ref[...]).astype(v_ref.dtype)
        gate_ref[rows, :] = _dot(h, wg_ref[...]).astype(gate_ref.dtype)

    normalise(0)
    for p in range(NORM_PARTS):
        if p + 1 < NORM_PARTS:
            normalise(p + 1)
        project(p)
    z_scr[0:pad, :] = z_scr[tm:tm + pad, :]


def _odd_in(x2, g, wconv, wq, wk, wv, wg, conv_w, cos, sin, seq):
    n, d = x2.shape
    tm = IN_ROW_BLOCK
    nb = seq // tm
    cw = conv_w.shape[1]
    rw = wq.shape[1]
    head_dim = cos.shape[1]
    row = lambda i: (i, 0)
    tab = pl.BlockSpec((tm, head_dim), lambda i: (i % nb, 0))
    o = lambda w: jax.ShapeDtypeStruct((n, w), BF16)
    return pl.pallas_call(
        functools.partial(_odd_in_kernel, blocks_per_seq=nb, taps=conv_w.shape[0], head_dim=head_dim),
        grid=(n // tm,),
        in_specs=[pl.BlockSpec((tm, d), row), _resident((1, d)), _resident(wconv.shape),
                  _resident(wq.shape), _resident(wk.shape), _resident(wv.shape), _resident(wg.shape),
                  _resident(conv_w.shape), tab, tab],
        out_specs=[pl.BlockSpec((tm, cw), row)] + [pl.BlockSpec((tm, rw), row)] * 4,
        out_shape=[o(cw), o(rw), o(rw), o(rw), o(rw)],
        scratch_shapes=[pltpu.VMEM((tm + 2 * SUBLANES, cw), F32), pltpu.VMEM((tm, d), BF16)],
        compiler_params=_params(("arbitrary",), 40),
        name="odd_in",
    )(x2, g, wconv, wq, wk, wv, wg, conv_w, cos, sin)


def _retention_kernel(q_ref, k_ref, v_ref, g_ref, o_ref, state_scr, decay_scr, *, chunk, heads):
    ri = pl.program_id(1)
    head_dim = q_ref.shape[2] // heads
    log_gamma = [math.log(1.0 - 2.0 ** (-5.0 - h)) for h in range(heads)]

    @pl.when(ri == 0)
    def _():
        state_scr[...] = jnp.zeros_like(state_scr)

    @pl.when((pl.program_id(0) == 0) & (ri == 0))
    def _():
        ridx = lax.broadcasted_iota(jnp.int32, (chunk, chunk), 0)
        cidx = lax.broadcasted_iota(jnp.int32, (chunk, chunk), 1)
        rel = (ridx - cidx).astype(F32)
        for h in range(heads):
            decay_scr[h] = jnp.where(rel >= 0, jnp.exp(log_gamma[h] * jnp.maximum(rel, 0.0)), 0.0)

    idx = lax.broadcasted_iota(jnp.int32, (chunk, 1), 0).astype(F32)
    for h in range(heads):
        query_decay = jnp.exp(log_gamma[h] * (idx + 1.0))
        key_decay = jnp.exp(log_gamma[h] * (chunk - 1.0 - idx))
        chunk_decay = math.exp(log_gamma[h] * chunk)
        cols = slice(h * head_dim, (h + 1) * head_dim)
        state = state_scr[h]
        for c in range(q_ref.shape[1] // chunk):
            rows = slice(c * chunk, (c + 1) * chunk)
            q = q_ref[0, rows, cols]
            k = k_ref[0, rows, cols]
            v = v_ref[0, rows, cols]
            s = _dot_nt(q, k) * decay_scr[h]
            ret = _dot(s.astype(BF16), v) + _dot(q, state.astype(BF16)) * query_decay
            kd = (k.astype(F32) * key_decay).astype(BF16)
            state = state * chunk_decay + _dot_tn(kd, v)
            mu = jnp.mean(ret, axis=-1, keepdims=True)
            dev = ret - mu
            var = jnp.mean(dev * dev, axis=-1, keepdims=True)
            g = g_ref[0, rows, cols].astype(F32)
            o_ref[0, rows, cols] = (g * _sigmoid(g) * (dev * lax.rsqrt(var + EPS))).astype(o_ref.dtype)
        state_scr[h] = state


def _retention(q3, k3, v3, g3):
    b, seq, rw = q3.shape
    head_dim = rw // RET_HEADS
    tr = RET_ROW_BLOCK
    blk = pl.BlockSpec((1, tr, rw), lambda bi, ri: (bi, ri, 0))
    return pl.pallas_call(
        functools.partial(_retention_kernel, chunk=RET_CHUNK, heads=RET_HEADS),
        grid=(b, seq // tr),
        in_specs=[blk, blk, blk, blk],
        out_specs=blk,
        out_shape=jax.ShapeDtypeStruct(q3.shape, BF16),
        scratch_shapes=[pltpu.VMEM((RET_HEADS, head_dim, head_dim), F32),
                        pltpu.VMEM((RET_HEADS, RET_CHUNK, RET_CHUNK), F32)],
        compiler_params=_params(("arbitrary", "arbitrary"), 32),
        name="retention",
    )(q3, k3, v3, g3)


def _even_layer(x2, batch, seq, norm_g, w_in, b_forget, log_dt, lam_re, lam_im, b_re, b_im, c_re, c_im,
                d_skip, w_glu, b_glu, w_out, mlp_g, w_up, w_down, layer, g_final, final_norm):
    d = x2.shape[1]
    heads = b_forget.shape[0]
    fw = heads * FOX_HEAD_DIM
    sw = d_skip.shape[0]
    wqkv = w_in[:, :3 * fw].astype(BF16)
    wf = jnp.pad(w_in[:, 3 * fw:3 * fw + heads], ((0, 0), (0, LANES - heads))).astype(BF16)
    wu = w_in[:, 3 * fw + heads:].astype(BF16)
    bf = jnp.pad(b_forget.astype(F32), (0, LANES - heads))[None, :]
    qscale = jnp.concatenate([jnp.full((fw,), LOG2_E * FOX_HEAD_DIM ** -0.5, F32), jnp.ones((2 * fw,), F32)])[None, :]
    qkv, u, c4 = _even_in(x2, norm_g[None, :], wqkv, qscale, wu, wf, bf, seq)
    fox = _fox(qkv.reshape(batch, seq, 3 * fw), c4, heads)

    a_re, a_im, bb_re, bb_im = _s5_params(log_dt, lam_re, lam_im, b_re, b_im)
    groups, state, group = b_re.shape
    tile_b = lambda t: jnp.tile(t.transpose(0, 2, 1).reshape(groups * group, state), (1, LANES // state))
    tile_c = lambda t: jnp.tile(t.transpose(0, 2, 1).reshape(groups * state, group), (1, LANES // group))
    s5 = _s5(u.reshape(batch, seq, sw), tile_b(bb_re), tile_b(bb_im), tile_c(c_re), tile_c(c_im), a_re, a_im,
             d_skip[None, :], w_glu.astype(BF16), b_glu[None, :], group)

    w_out = w_out.astype(BF16)
    return _out_mlp(fox.reshape(batch * seq, fw), s5.reshape(batch * seq, sw), x2, w_out[:fw], w_out[fw:],
                    mlp_g[None, :], w_up, w_down, layer, g_final[None, :], final_norm)


def _odd_layer(x2, batch, seq, norm_g, w_in, conv_w, w_out, mlp_g, w_up, w_down, layer, g_final, final_norm):
    d = x2.shape[1]
    cw = conv_w.shape[1]
    rw = d - cw
    head_dim = rw // RET_HEADS
    perm = jnp.concatenate([jnp.arange(0, head_dim, 2), jnp.arange(1, head_dim, 2)])
    perm = (jnp.arange(RET_HEADS)[:, None] * head_dim + perm[None, :]).reshape(-1)
    w_in = w_in.astype(BF16)
    wconv = w_in[:, :3 * cw]
    wq = w_in[:, 3 * cw:3 * cw + rw][:, perm]
    wk = w_in[:, 3 * cw + rw:3 * cw + 2 * rw][:, perm]
    wv = w_in[:, 3 * cw + 2 * rw:3 * cw + 3 * rw]
    wg = w_in[:, 3 * cw + 3 * rw:]
    cos, sin = _rope_tables(seq, head_dim)
    conv, q, k, v, gate = _odd_in(x2, norm_g[None, :], wconv, wq, wk, wv, wg, conv_w.astype(F32), cos, sin, seq)
    to3 = lambda t: t.reshape(batch, seq, rw)
    ret = _retention(to3(q), to3(k), to3(v), to3(gate))
    w_out = w_out.astype(BF16)
    return _out_mlp(conv, ret.reshape(batch * seq, rw), x2, w_out[:cw], w_out[cw:],
                    mlp_g[None, :], w_up, w_down, layer, g_final[None, :], final_norm)


def kernel(x, even_norm_mix, even_w_in, even_b_forget, even_s5_log_dt, even_s5_lambda_re, even_s5_lambda_im, even_s5_b_re, even_s5_b_im, even_s5_c_re, even_s5_c_im, even_s5_d, even_s5_w_glu, even_s5_b_glu, even_w_out, odd_norm_mix, odd_w_in, odd_conv_w, odd_w_out, mlp_norm, mlp_w_up, mlp_w_down, final_norm):
    batch, seq, d = x.shape
    depth = mlp_norm.shape[0]
    x2 = x.reshape(batch * seq, d)
    for layer in range(depth):
        j = layer // 2
        last = layer == depth - 1
        if layer % 2 == 0:
            x2 = _even_layer(x2, batch, seq, even_norm_mix[j], even_w_in[j], even_b_forget[j],
                             even_s5_log_dt[j], even_s5_lambda_re[j], even_s5_lambda_im[j],
                             even_s5_b_re[j], even_s5_b_im[j], even_s5_c_re[j], even_s5_c_im[j],
                             even_s5_d[j], even_s5_w_glu[j], even_s5_b_glu[j], even_w_out[j],
                             mlp_norm[layer], mlp_w_up, mlp_w_down, layer, final_norm, last)
        else:
            x2 = _odd_layer(x2, batch, seq, odd_norm_mix[j], odd_w_in[j], odd_conv_w[j], odd_w_out[j],
                            mlp_norm[layer], mlp_w_up, mlp_w_down, layer, final_norm, last)
    return x2.reshape(batch, seq, d)
```
